```python
import math
import jax, jax.numpy as jnp
from jax import lax
import numpy as np

D_MODEL = 2048
BATCH = 4
SEQ = 2048
DEPTH = 1
DEC_BATCH = 128
DEC_SEQ = 8
PAST_LEN = 16384
PAGE_SIZE = 128

MIX_WIDTH = D_MODEL
GLA_WIDTH = MIX_WIDTH // 2
SSD_WIDTH = MIX_WIDTH - GLA_WIDTH
GLA_HEADS = 4
GLA_HEAD_V = GLA_WIDTH // GLA_HEADS
GLA_HEAD_K = GLA_HEAD_V // 2
GLA_KDIM = GLA_HEADS * GLA_HEAD_K
GLA_LOWRANK = 16
GLA_GATE_NORMALIZER = 16.0
SSD_HEADDIM = 64
SSD_HEADS = SSD_WIDTH // SSD_HEADDIM
SSD_STATE = 128
SSD_GROUPS = 2
SSD_HPG = SSD_HEADS // SSD_GROUPS
SSD_CONV = 4
SSD_CONV_CH = SSD_WIDTH + 2 * SSD_GROUPS * SSD_STATE
FFN_DIM = 5632
FFN_CONV = 3
CHUNK = 64
EPS = 1e-6

IN_SIZES = (GLA_KDIM, GLA_KDIM, GLA_WIDTH, GLA_WIDTH, GLA_LOWRANK,
            SSD_WIDTH, SSD_CONV_CH, SSD_HEADS)
IN_COLS = sum(IN_SIZES)

kernel_name = "hymba_gla_ssd_convffn_step"


def _rmsnorm(x, w):
    xf = x.astype(jnp.float32)
    y = xf * lax.rsqrt(jnp.mean(xf * xf, axis=-1, keepdims=True) + EPS)
    return (y * w.astype(jnp.float32)).astype(x.dtype)


def _causal_dwconv(x, buf, w, b):
    width = w.shape[0]
    L = x.shape[1]
    xp = jnp.concatenate([buf.astype(x.dtype), x], axis=1)
    out = b + sum(xp[:, j:j + L] * w[j] for j in range(width))
    return out, xp[:, xp.shape[1] - (width - 1):]


def _to_chunks(a, c):
    B, L = a.shape[:2]
    return jnp.moveaxis(a.reshape((B, L // c, c) + a.shape[2:]), 1, 0)


def _from_chunks(a):
    n, B, c = a.shape[:3]
    return jnp.moveaxis(a, 0, 1).reshape((B, n * c) + a.shape[3:])


def _gla_chunked(q, k, v, lg, s0):
    L = q.shape[1]
    c = math.gcd(L, CHUNK)
    causal = jnp.tril(jnp.ones((c, c), dtype=bool))[None, :, :, None, None]

    def step(S, inp):
        qc, kc, vc, gc = inp
        b = jnp.cumsum(gc, axis=1)
        o_inter = jnp.einsum('bthk,bhkv->bthv', qc * jnp.exp(b), S)
        diff = b[:, :, None] - b[:, None, :]
        dec = jnp.exp(jnp.where(causal, diff, -jnp.inf))
        att = jnp.einsum('bthk,btshk,bshk->bhts', qc, dec, kc)
        o_intra = jnp.einsum('bhts,bshv->bthv', att, vc)
        b_last = b[:, -1]
        k_dec = kc * jnp.exp(b_last[:, None] - b)
        S_new = jnp.exp(b_last)[..., None] * S + jnp.einsum('bshk,bshv->bhkv', k_dec, vc)
        return S_new, o_inter + o_intra

    S, o = lax.scan(step, s0, tuple(_to_chunks(a, c) for a in (q, k, v, lg)))
    return _from_chunks(o), S


def _ssd_chunked(x, dt, la, Bm, Cm, h0):
    L = x.shape[1]
    c = math.gcd(L, CHUNK)
    causal = jnp.tril(jnp.ones((c, c), dtype=bool))[None, :, :, None, None]

    def step(h, inp):
        xc, dtc, lac, bc, cc = inp
        cum = jnp.cumsum(lac, axis=1)
        y_inter = jnp.einsum('btgn,bghpn->btghp', cc, h) * jnp.exp(cum)[..., None]
        diff = cum[:, :, None] - cum[:, None, :]
        dec = jnp.exp(jnp.where(causal, diff, -jnp.inf))
        cb = jnp.einsum('btgn,bsgn->btsg', cc, bc)
        y_intra = jnp.einsum('btsg,btsgh,bsgh,bsghp->btghp', cb, dec, dtc, xc)
        last = cum[:, -1]
        wdec = jnp.exp(last[:, None] - cum) * dtc
        h_new = jnp.exp(last)[..., None, None] * h + jnp.einsum('bsgh,bsgn,bsghp->bghpn', wdec, bc, xc)
        return h_new, y_inter + y_intra

    h, y = lax.scan(step, h0, tuple(_to_chunks(a, c) for a in (x, dt, la, Bm, Cm)))
    return _from_chunks(y), h


def _mixer(xn, s_gla, s_ssd, s_conv, p):
    B, L, _ = xn.shape
    f32 = jnp.float32
    proj = xn @ p['w_in']
    split_pts = np.cumsum(IN_SIZES)[:-1].tolist()
    q, k, v, g, lr, z, xbc, dtr = jnp.split(proj, split_pts, axis=-1)

    q = q.reshape(B, L, GLA_HEADS, GLA_HEAD_K).astype(f32) * (GLA_HEAD_K ** -0.5)
    k = k.reshape(B, L, GLA_HEADS, GLA_HEAD_K).astype(f32)
    v = v.reshape(B, L, GLA_HEADS, GLA_HEAD_V).astype(f32)
    lg = jax.nn.log_sigmoid((lr @ p['gla_w_gate2'] + p['gla_b_gate']).astype(f32)) / GLA_GATE_NORMALIZER
    lg = lg.reshape(B, L, GLA_HEADS, GLA_HEAD_K)
    o, S = _gla_chunked(q, k, v, lg, s_gla.astype(f32))
    o = _rmsnorm(o, p['gla_norm']) * jax.nn.silu(g.reshape(B, L, GLA_HEADS, GLA_HEAD_V).astype(f32))
    o_gla = o.reshape(B, L, GLA_WIDTH)

    xbc, conv_new = _causal_dwconv(xbc, s_conv, p['ssd_conv_w'], p['ssd_conv_b'])
    xbc = jax.nn.silu(xbc)
    xs, Bm, Cm = jnp.split(xbc, [SSD_WIDTH, SSD_WIDTH + SSD_GROUPS * SSD_STATE], axis=-1)
    xs = xs.reshape(B, L, SSD_GROUPS, SSD_HPG, SSD_HEADDIM).astype(f32)
    Bm = Bm.reshape(B, L, SSD_GROUPS, SSD_STATE).astype(f32)
    Cm = Cm.reshape(B, L, SSD_GROUPS, SSD_STATE).astype(f32)
    dt = jax.nn.softplus(dtr.astype(f32) + p['ssd_dt_bias'].astype(f32)).reshape(B, L, SSD_GROUPS, SSD_HPG)
    A = -jnp.exp(p['ssd_A_log'].astype(f32)).reshape(SSD_GROUPS, SSD_HPG)
    h0 = s_ssd.astype(f32).reshape(B, SSD_GROUPS, SSD_HPG, SSD_HEADDIM, SSD_STATE)
    y, h = _ssd_chunked(xs, dt, dt * A, Bm, Cm, h0)
    y = y + p['ssd_D'].astype(f32).reshape(SSD_GROUPS, SSD_HPG)[..., None] * xs
    y = y.reshape(B, L, SSD_GROUPS, SSD_WIDTH // SSD_GROUPS) * jax.nn.silu(
        z.reshape(B, L, SSD_GROUPS, SSD_WIDTH // SSD_GROUPS).astype(f32))
    y = _rmsnorm(y, p['ssd_norm'].reshape(SSD_GROUPS, SSD_WIDTH // SSD_GROUPS)).reshape(B, L, SSD_WIDTH)

    mix = jnp.concatenate([o_gla, y], axis=-1).astype(xn.dtype) @ p['w_out']
    h = h.reshape(B, SSD_HEADS, SSD_HEADDIM, SSD_STATE)
    return mix, S.astype(xn.dtype), h.astype(xn.dtype), conv_new.astype(xn.dtype)


def _conv_ffn(xn, s_ffn, p):
    u = xn @ p['ffn_w_in']
    u, buf = _causal_dwconv(u, s_ffn, p['ffn_conv_w'], p['ffn_conv_b'])
    a, b = jnp.split(u, 2, axis=-1)
    return (jax.nn.silu(a) * b) @ p['ffn_w_out'], buf.astype(xn.dtype)


def _layer(x, s_gla, s_ssd, s_conv, s_ffn, p):
    xn = _rmsnorm(x, p['n_mix_pre'])
    mix, g_new, h_new, c_new = _mixer(xn, s_gla, s_ssd, s_conv, p)
    x = x + _rmsnorm(mix, p['n_mix_post'])
    xn = _rmsnorm(x, p['n_ffn_pre'])
    f, f_new = _conv_ffn(xn, s_ffn, p)
    x = x + _rmsnorm(f, p['n_ffn_post'])
    return x, g_new, h_new, c_new, f_new


def setup_inputs(seed: int = 0) -> dict:
    key = jax.random.key(seed)
    ks = jax.random.split(key, 26)
    nrm = jax.random.normal
    f32 = jnp.float32
    dt = jnp.exp(jax.random.uniform(ks[16], (DEPTH, SSD_HEADS), f32,
                                    minval=math.log(1e-3), maxval=math.log(1e-1)))
    return {
        'x_prompt': nrm(ks[0], (BATCH, SEQ, D_MODEL), f32),
        'x_sample': nrm(ks[1], (DEC_BATCH, DEC_SEQ, D_MODEL), f32),
        'state_gla': 0.5 * nrm(ks[2], (DEPTH, DEC_BATCH, GLA_HEADS, GLA_HEAD_K, GLA_HEAD_V), f32),
        'state_ssd': 0.5 * nrm(ks[3], (DEPTH, DEC_BATCH, SSD_HEADS, SSD_HEADDIM, SSD_STATE), f32),
        'state_ssd_conv': nrm(ks[4], (DEPTH, DEC_BATCH, SSD_CONV - 1, SSD_CONV_CH), f32),
        'state_ffn_conv': nrm(ks[5], (DEPTH, DEC_BATCH, FFN_CONV - 1, 2 * FFN_DIM), f32),
        'norm_mix_pre': 1.0 + 0.02 * nrm(ks[6], (DEPTH, D_MODEL), f32),
        'norm_mix_post': 1.0 + 0.02 * nrm(ks[7], (DEPTH, D_MODEL), f32),
        'norm_ffn_pre': 1.0 + 0.02 * nrm(ks[8], (DEPTH, D_MODEL), f32),
        'norm_ffn_post': 1.0 + 0.02 * nrm(ks[9], (DEPTH, D_MODEL), f32),
        'w_in': nrm(ks[10], (DEPTH, D_MODEL, IN_COLS), f32) * D_MODEL ** -0.5,
        'gla_w_gate2': nrm(ks[11], (DEPTH, GLA_LOWRANK, GLA_KDIM), f32) * GLA_LOWRANK ** -0.5,
        'gla_b_gate': 0.1 * nrm(ks[12], (DEPTH, GLA_KDIM), f32),
        'gla_norm': 1.0 + 0.02 * nrm(ks[13], (DEPTH, GLA_HEAD_V), f32),
        'ssd_conv_w': nrm(ks[14], (DEPTH, SSD_CONV, SSD_CONV_CH), f32) * SSD_CONV ** -0.5,
        'ssd_conv_b': 0.01 * nrm(ks[15], (DEPTH, SSD_CONV_CH), f32),
        'ssd_dt_bias': dt + jnp.log(-jnp.expm1(-dt)),
        'ssd_A_log': jnp.log(jax.random.uniform(ks[17], (DEPTH, SSD_HEADS), f32, minval=1.0, maxval=16.0)),
        'ssd_D': 1.0 + 0.1 * nrm(ks[18], (DEPTH, SSD_HEADS), f32),
        'ssd_norm': 1.0 + 0.02 * nrm(ks[19], (DEPTH, SSD_WIDTH), f32),
        'w_out': nrm(ks[20], (DEPTH, MIX_WIDTH, D_MODEL), f32) * MIX_WIDTH ** -0.5,
        'ffn_w_in': nrm(ks[21], (DEPTH, D_MODEL, 2 * FFN_DIM), f32) * D_MODEL ** -0.5,
        'ffn_conv_w': nrm(ks[22], (DEPTH, FFN_CONV, 2 * FFN_DIM), f32) * FFN_CONV ** -0.5,
        'ffn_conv_b': 0.01 * nrm(ks[23], (DEPTH, 2 * FFN_DIM), f32),
        'ffn_w_out': nrm(ks[24], (DEPTH, FFN_DIM, D_MODEL), f32) * FFN_DIM ** -0.5,
    }


def reference(x_prompt, x_sample, state_gla, state_ssd, state_ssd_conv, state_ffn_conv,
              norm_mix_pre, norm_mix_post, norm_ffn_pre, norm_ffn_post, w_in,
              gla_w_gate2, gla_b_gate, gla_norm, ssd_conv_w, ssd_conv_b, ssd_dt_bias,
              ssd_A_log, ssd_D, ssd_norm, w_out, ffn_w_in, ffn_conv_w, ffn_conv_b, ffn_w_out):
    dt_ = x_prompt.dtype
    xp, xs = x_prompt, x_sample
    gp, hp, cp, fp = [], [], [], []
    gs, hs, cs, fs = [], [], [], []
    for l in range(DEPTH):
        p = dict(n_mix_pre=norm_mix_pre[l], n_mix_post=norm_mix_post[l],
                 n_ffn_pre=norm_ffn_pre[l], n_ffn_post=norm_ffn_post[l], w_in=w_in[l],
                 gla_w_gate2=gla_w_gate2[l], gla_b_gate=gla_b_gate[l], gla_norm=gla_norm[l],
                 ssd_conv_w=ssd_conv_w[l], ssd_conv_b=ssd_conv_b[l], ssd_dt_bias=ssd_dt_bias[l],
                 ssd_A_log=ssd_A_log[l], ssd_D=ssd_D[l], ssd_norm=ssd_norm[l], w_out=w_out[l],
                 ffn_w_in=ffn_w_in[l], ffn_conv_w=ffn_conv_w[l], ffn_conv_b=ffn_conv_b[l],
                 ffn_w_out=ffn_w_out[l])
        B = xp.shape[0]
        xp, g1, h1, c1, f1 = _layer(
            xp,
            jnp.zeros((B, GLA_HEADS, GLA_HEAD_K, GLA_HEAD_V), dt_),
            jnp.zeros((B, SSD_HEADS, SSD_HEADDIM, SSD_STATE), dt_),
            jnp.zeros((B, SSD_CONV - 1, SSD_CONV_CH), dt_),
            jnp.zeros((B, FFN_CONV - 1, 2 * FFN_DIM), dt_),
            p)
        xs, g2, h2, c2, f2 = _layer(xs, state_gla[l], state_ssd[l], state_ssd_conv[l],
                                    state_ffn_conv[l], p)
        gp.append(g1); hp.append(h1); cp.append(c1); fp.append(f1)
        gs.append(g2); hs.append(h2); cs.append(c2); fs.append(f2)
    return (xp, xs,
            jnp.stack(gp), jnp.stack(hp), jnp.stack(cp), jnp.stack(fp),
            jnp.stack(gs), jnp.stack(hs), jnp.stack(cs), jnp.stack(fs))
```

```python
import functools

import numpy as np
import jax
import jax.numpy as jnp
from jax import lax
from jax.experimental import pallas as pl
from jax.experimental.pallas import tpu as pltpu

F32 = jnp.float32
BF16 = jnp.bfloat16
EPS = 1e-6

GLA_HEADS = 4
GLA_HEAD_K = 128
GLA_HEAD_V = 256
GLA_KDIM = GLA_HEADS * GLA_HEAD_K
GLA_WIDTH = GLA_HEADS * GLA_HEAD_V
GLA_LOWRANK = 16
GLA_GATE_NORMALIZER = 16.0
SSD_HEADS = 16
SSD_HEADDIM = 64
SSD_STATE = 128
SSD_GROUPS = 2
SSD_WIDTH = SSD_HEADS * SSD_HEADDIM
SSD_GROUP_WIDTH = SSD_WIDTH // SSD_GROUPS
SSD_CONV = 4
SSD_BC = 2 * SSD_GROUPS * SSD_STATE
SSD_CONV_CH = SSD_WIDTH + SSD_BC
FFN_CONV = 3
CHUNK = 64

COL_Q = 0
COL_K = GLA_KDIM
COL_V = 2 * GLA_KDIM
COL_G = COL_V + GLA_WIDTH
COL_Z = COL_G + GLA_WIDTH
COL_XS = COL_Z + SSD_WIDTH
COL_BC = COL_XS + SSD_WIDTH
COL_SMALL = COL_BC + SSD_BC
SMALL_W = 128
DTR_OFF = GLA_LOWRANK
IN_COLS_PAD = 6144

VMEM_LIMIT = 56 * 1024 * 1024


def _cparams(sem):
    return pltpu.CompilerParams(dimension_semantics=sem, vmem_limit_bytes=VMEM_LIMIT)


def _split3(x):
    hi = x.astype(BF16)
    r = x - hi.astype(F32)
    mid = r.astype(BF16)
    lo = (r - mid.astype(F32)).astype(BF16)
    return hi, mid, lo


def _dot(a, b):
    return jnp.dot(a, b, preferred_element_type=F32)


def _dot_nt(a, b):
    return lax.dot_general(a, b, (((1,), (1,)), ((), ())), preferred_element_type=F32)


def _dot_tn(a, b):
    return lax.dot_general(a, b, (((0,), (0,)), ((), ())), preferred_element_type=F32)


def _dot_sel(sel_bf16, x_f32):
    hi, mid, lo = _split3(x_f32)
    return _dot(sel_bf16, hi) + _dot(sel_bf16, mid) + _dot(sel_bf16, lo)


def _silu(x):
    return x / (1.0 + jnp.exp(-x))


def _softplus(x):
    return jnp.maximum(x, 0.0) + jnp.log1p(jnp.exp(-jnp.abs(x)))


def _rms(x, w):
    return x * lax.rsqrt(jnp.mean(x * x, axis=-1, keepdims=True) + EPS) * w


def _chunk_consts(nb, c):
    R = nb * c
    idx = np.arange(R)
    seq, pos = idx // c, idx % c
    same = seq[:, None] == seq[None, :]
    t, u = pos[:, None], pos[None, :]
    blocks = [same & (u <= t), same & (u > t)]
    masks = [np.eye(R, dtype=bool)]
    m = c // 2
    while m >= 1:
        blk = pos // (2 * m)
        rho = blk * 2 * m + m - 1
        upper = pos > rho
        a_up = upper[:, None] & (u > rho[:, None]) & (u <= t)
        a_lo = (~upper)[:, None] & (u > t) & (u <= rho[:, None])
        blocks.append(same & (a_up | a_lo))
        masks.append(same & upper[:, None] & (~upper)[None, :] & (blk[:, None] == blk[None, :]))
        m //= 2
    sel = np.concatenate(blocks, 0).astype(np.float32)
    lvl = np.stack(masks).astype(np.float32)
    causal = (same & (u <= t)).astype(np.float32)
    return sel, lvl, causal


def _inproj_kernel(x_ref, nw_ref, w_ref, o_ref, xn_ref):
    @pl.when(pl.program_id(1) == 0)
    def _():
        xn_ref[...] = _rms(x_ref[...], nw_ref[...]).astype(BF16)

    o_ref[...] = _dot(xn_ref[...], w_ref[...])


def _inproj(x2d, nw, w_bf16, tm, tn):
    M, D = x2d.shape
    N = w_bf16.shape[1]
    return pl.pallas_call(
        _inproj_kernel,
        grid=(M // tm, N // tn),
        in_specs=[pl.BlockSpec((tm, D), lambda i, j: (i, 0)),
                  pl.BlockSpec((1, D), lambda i, j: (0, 0)),
                  pl.BlockSpec((D, tn), lambda i, j: (0, j))],
        out_specs=pl.BlockSpec((tm, tn), lambda i, j: (i, j)),
        out_shape=jax.ShapeDtypeStruct((M, N), F32),
        scratch_shapes=[pltpu.VMEM((tm, D), BF16)],
        compiler_params=_cparams(("parallel", "arbitrary")),
        name="inproj",
    )(x2d, nw, w_bf16)


def _gla_kernel(q_ref, k_ref, v_ref, g_ref, sm_ref, s0_ref, sel_ref, lvl_ref, w2_ref, bg_ref, gn_ref,
                o_ref, sn_ref, s_scr, *, nb, c, nlev):
    R = nb * c
    ci = pl.program_id(1)

    @pl.when(ci == 0)
    def _():
        s_scr[...] = s0_ref[...]

    zg = jnp.dot(sm_ref[...], w2_ref[...], precision=lax.Precision.HIGHEST,
                 preferred_element_type=F32) + bg_ref[...]
    lg = -_softplus(-zg) * (1.0 / GLA_GATE_NORMALIZER)
    E = _dot_sel(sel_ref[...], lg)
    b = E[0:R]
    eb = jnp.exp(b)
    q = q_ref[...] * (GLA_HEAD_K ** -0.5)
    k = k_ref[...]
    qe = q * eb
    ke = k * jnp.exp(E[R:2 * R])
    ql, kl = [q.astype(BF16)], [k.astype(BF16)]
    for l in range(nlev):
        p = jnp.exp(E[(2 + l) * R:(3 + l) * R])
        ql.append((q * p).astype(BF16))
        kl.append((k * p).astype(BF16))
    vb = v_ref[...].astype(BF16)
    g = g_ref[...]
    gn = gn_ref[...]
    rows = lax.broadcasted_iota(jnp.int32, (R, 1), 0)

    for h in range(GLA_HEADS):
        ks = slice(h * GLA_HEAD_K, (h + 1) * GLA_HEAD_K)
        vs = slice(h * GLA_HEAD_V, (h + 1) * GLA_HEAD_V)
        att = jnp.zeros((R, R), F32)
        for l in range(nlev + 1):
            att = att + _dot_nt(ql[l][:, ks], kl[l][:, ks]) * lvl_ref[l]
        o = _dot(att.astype(BF16), vb[:, vs])
        for s in range(nb):
            if nb > 1:
                rm = jnp.logical_and(rows >= s * c, rows < (s + 1) * c).astype(F32)
                qs = (qe[:, ks] * rm).astype(BF16)
                kd = (ke[:, ks] * rm).astype(BF16)
            else:
                qs = qe[:, ks].astype(BF16)
                kd = ke[:, ks].astype(BF16)
            S = s_scr[s, h]
            o = o + _dot(qs, S.astype(BF16))
            upd = _dot_tn(kd, vb[:, vs])
            d = eb[s * c + c - 1:s * c + c, ks]
            dcol = jnp.transpose(jnp.broadcast_to(d, (GLA_HEAD_K, GLA_HEAD_K)))
            s_scr[s, h] = jnp.concatenate([dcol, dcol], axis=1) * S + upd
        gh = g[:, vs]
        o_ref[:, vs] = (_rms(o, gn) * _silu(gh)).astype(BF16)

    @pl.when(ci == pl.num_programs(1) - 1)
    def _():
        sn_ref[...] = s_scr[...]


def _gla(proj, s0, w2p, bg, gn, *, B, L, nb, c):
    R = nb * c
    ncl = L // c
    nlev = int(np.log2(c))
    sel, lvl, _ = _chunk_consts(nb, c)
    sel = jnp.asarray(sel, BF16)
    lvl = jnp.asarray(lvl, F32)
    M = proj.shape[0]

    def rowblk(w, col):
        return pl.BlockSpec((R, w), lambda bi, ci: (bi * ncl + ci, col // w))

    const2 = lambda a: pl.BlockSpec(a.shape, lambda bi, ci: (0,) * a.ndim)
    sspec = pl.BlockSpec((nb, GLA_HEADS, GLA_HEAD_K, GLA_HEAD_V), lambda bi, ci: (bi, 0, 0, 0))
    kern = functools.partial(_gla_kernel, nb=nb, c=c, nlev=nlev)
    return pl.pallas_call(
        kern,
        grid=(B // nb, ncl),
        in_specs=[rowblk(GLA_KDIM, COL_Q), rowblk(GLA_KDIM, COL_K), rowblk(GLA_WIDTH, COL_V),
                  rowblk(GLA_WIDTH, COL_G), rowblk(SMALL_W, COL_SMALL), sspec,
                  const2(sel), const2(lvl), const2(w2p), const2(bg), const2(gn)],
        out_specs=[pl.BlockSpec((R, GLA_WIDTH), lambda bi, ci: (bi * ncl + ci, 0)), sspec],
        out_shape=[jax.ShapeDtypeStruct((M, GLA_WIDTH), BF16),
                   jax.ShapeDtypeStruct((B, GLA_HEADS, GLA_HEAD_K, GLA_HEAD_V), F32)],
        scratch_shapes=[pltpu.VMEM((nb, GLA_HEADS, GLA_HEAD_K, GLA_HEAD_V), F32)],
        compiler_params=_cparams(("parallel", "arbitrary")),
        name="gla",
    )(proj, proj, proj, proj, proj, s0, sel, lvl, w2p, bg, gn)


def _conv_taps(u3, prev, w, bias, width):
    n, L, C = u3.shape
    pos = lax.broadcasted_iota(jnp.int32, (1, L, 1), 1)
    acc = bias + w[width - 1:width] * u3
    for j in range(width - 1):
        d = width - 1 - j
        r = pltpu.roll(u3, d, axis=1)
        for tt in range(d):
            r = jnp.where(pos == tt, prev[:, j + tt:j + tt + 1, :], r)
        acc = acc + w[j:j + 1] * r
    return acc


def _ssd_kernel(z_ref, xs_ref, bc_ref, sm_ref, cst_ref, h0_ref, sel_ref, cm_ref, ex_ref,
                cw_ref, cb_ref, dtb_ref, alog_ref, de_ref, nw_ref,
                y_ref, hn_ref, ht_scr, cx_scr, cbc_scr, *, nb, c):
    R = nb * c
    ci = pl.program_id(1)
    HP = SSD_WIDTH

    @pl.when(ci == 0)
    def _():
        cx_scr[...] = cst_ref[:, :, 0:SSD_WIDTH]
        cbc_scr[...] = cst_ref[:, :, SSD_WIDTH:SSD_CONV_CH]
        for s in range(nb):
            ht_scr[s] = jnp.transpose(h0_ref[s].reshape(HP, SSD_STATE))

    cw = cw_ref[...]
    cbias = cb_ref[...]
    xraw = xs_ref[...].reshape(nb, c, SSD_WIDTH)
    bcraw = bc_ref[...].reshape(nb, c, SSD_BC)
    xs = _silu(_conv_taps(xraw, cx_scr[...], cw[:, 0:SSD_WIDTH], cbias[:, 0:SSD_WIDTH], SSD_CONV))
    bca = _silu(_conv_taps(bcraw, cbc_scr[...], cw[:, SSD_WIDTH:SSD_CONV_CH],
                           cbias[:, SSD_WIDTH:SSD_CONV_CH], SSD_CONV))
    cx_scr[...] = xraw[:, c - (SSD_CONV - 1):c, :]
    cbc_scr[...] = bcraw[:, c - (SSD_CONV - 1):c, :]
    xs = xs.reshape(R, SSD_WIDTH)
    bca = bca.reshape(R, SSD_BC)
    Bm = bca[:, 0:SSD_GROUPS * SSD_STATE]
    Cm = bca[:, SSD_GROUPS * SSD_STATE:SSD_BC]

    dt = _softplus(sm_ref[...] + dtb_ref[...])
    la = dt * (-jnp.exp(alog_ref[...]))
    cs = _dot_sel(sel_ref[...], la)
    cum = cs[0:R]
    ex = ex_ref[...]
    stack = jnp.concatenate([cs, dt], axis=0)
    s_hi, s_mid, s_lo = _split3(stack)
    st_e = _dot(s_hi, ex) + _dot(s_mid, ex) + _dot(s_lo, ex)
    cum_e = st_e[0:R]
    lmc_e = st_e[R:2 * R]
    dt_e = st_e[2 * R:3 * R]
    ecum_e = jnp.exp(cum_e)
    xdt = xs * dt_e
    xw = (xdt * jnp.exp(lmc_e)).astype(BF16)
    xdtb = xdt.astype(BF16)
    Bb = Bm.astype(BF16)
    Cb = Cm.astype(BF16)

    pad = jnp.zeros((128 - R, 128), F32) if R < 128 else None
    cum_p = cum if pad is None else jnp.concatenate([cum, pad], axis=0)
    cumT = jnp.transpose(cum_p)
    cmask = cm_ref[...] > 0.5
    rows = lax.broadcasted_iota(jnp.int32, (R, 1), 0)
    lane = lax.broadcasted_iota(jnp.int32, (1, 128), 1)
    lo_half = lane < SSD_HEADDIM

    hpg = SSD_HEADS // SSD_GROUPS
    for gi in range(SSD_GROUPS):
        ns = slice(gi * SSD_STATE, (gi + 1) * SSD_STATE)
        gs = slice(gi * SSD_GROUP_WIDTH, (gi + 1) * SSD_GROUP_WIDTH)
        cbm = _dot_nt(Cb[:, ns], Bb[:, ns])
        yg = []
        for j in range(hpg // 2):
            ls = slice(gi * SSD_GROUP_WIDTH + j * 128, gi * SSD_GROUP_WIDTH + (j + 1) * 128)
            xpair = xdt[:, ls]
            acc = None
            for e in range(2):
                hh = DTR_OFF + gi * hpg + 2 * j + e
                diff = cum[:, hh:hh + 1] - cumT[hh:hh + 1, 0:R]
                mh = cbm * jnp.exp(jnp.where(cmask, diff, -jnp.inf))
                xh = jnp.where(lo_half if e == 0 else jnp.logical_not(lo_half), xpair, 0.0)
                t = _dot(mh.astype(BF16), xh.astype(BF16))
                acc = t if acc is None else acc + t
            yg.append(acc)
        y_intra = jnp.concatenate(yg, axis=1)
        y_inter = jnp.zeros((R, SSD_GROUP_WIDTH), F32)
        for s in range(nb):
            if nb > 1:
                rm = jnp.logical_and(rows >= s * c, rows < (s + 1) * c).astype(F32)
                cg = (Cm[:, ns] * rm).astype(BF16)
                bg_ = (Bm[:, ns] * rm).astype(BF16)
            else:
                cg = Cb[:, ns]
                bg_ = Bb[:, ns]
            hT = ht_scr[s, :, gs]
            y_inter = y_inter + _dot(cg, hT.astype(BF16))
            upd = _dot_tn(bg_, xw[:, gs])
            dl = ecum_e[s * c + c - 1:s * c + c, gs]
            ht_scr[s, :, gs] = dl * hT + upd
        y = y_intra + y_inter * ecum_e[:, gs] + de_ref[:, gs] * xs[:, gs]
        y = y * _silu(z_ref[:, gs])
        y_ref[:, gs] = _rms(y, nw_ref[:, gs]).astype(BF16)

    @pl.when(ci == pl.num_programs(1) - 1)
    def _():
        for s in range(nb):
            hn_ref[s] = jnp.transpose(ht_scr[s]).reshape(SSD_HEADS, SSD_HEADDIM, SSD_STATE)


def _ssd(proj, cst, h0, cw, cb, dtb, alog, de, nw, *, B, L, nb, c):
    R = nb * c
    ncl = L // c
    sel, _, causal = _chunk_consts(nb, c)
    sel = jnp.asarray(sel[:2 * R], BF16)
    causal = jnp.asarray(causal, F32)
    exn = np.zeros((SMALL_W, SSD_WIDTH), np.float32)
    for h in range(SSD_HEADS):
        exn[DTR_OFF + h, h * SSD_HEADDIM:(h + 1) * SSD_HEADDIM] = 1.0
    ex = jnp.asarray(exn, BF16)
    M = proj.shape[0]

    def rowblk(w, col):
        return pl.BlockSpec((R, w), lambda bi, ci: (bi * ncl + ci, col // w))

    const2 = lambda a: pl.BlockSpec(a.shape, lambda bi, ci: (0,) * a.ndim)
    hspec = pl.BlockSpec((nb, SSD_HEADS, SSD_HEADDIM, SSD_STATE), lambda bi, ci: (bi, 0, 0, 0))
    kern = functools.partial(_ssd_kernel, nb=nb, c=c)
    return pl.pallas_call(
        kern,
        grid=(B // nb, ncl),
        in_specs=[rowblk(SSD_WIDTH, COL_Z), rowblk(SSD_WIDTH, COL_XS), rowblk(SSD_BC, COL_BC),
                  rowblk(SMALL_W, COL_SMALL),
                  pl.BlockSpec((nb, SSD_CONV - 1, SSD_CONV_CH), lambda bi, ci: (bi, 0, 0)), hspec,
                  const2(sel), const2(causal), const2(ex),
                  const2(cw), const2(cb), const2(dtb), const2(alog), const2(de), const2(nw)],
        out_specs=[pl.BlockSpec((R, SSD_WIDTH), lambda bi, ci: (bi * ncl + ci, 0)), hspec],
        out_shape=[jax.ShapeDtypeStruct((M, SSD_WIDTH), BF16),
                   jax.ShapeDtypeStruct((B, SSD_HEADS, SSD_HEADDIM, SSD_STATE), F32)],
        scratch_shapes=[pltpu.VMEM((nb, SSD_STATE, SSD_WIDTH), F32),
                        pltpu.VMEM((nb, SSD_CONV - 1, SSD_WIDTH), F32),
                        pltpu.VMEM((nb, SSD_CONV - 1, SSD_BC), F32)],
        compiler_params=_cparams(("parallel", "arbitrary")),
        name="ssd",
    )(proj, proj, proj, proj, cst, h0, sel, causal, ex, cw, cb, dtb, alog, de, nw)


def _outproj_kernel(o_ref, y_ref, x_ref, wt_ref, wb_ref, npost_ref, npre_ref, h_ref, xn_ref):
    mix = _dot(o_ref[...], wt_ref[...]) + _dot(y_ref[...], wb_ref[...])
    h = x_ref[...] + _rms(mix, npost_ref[...])
    h_ref[...] = h
    xn_ref[...] = _rms(h, npre_ref[...]).astype(BF16)


def _outproj(og, ys, x2d, w_out_bf16, npost, npre, tm):
    M, D = x2d.shape
    half = w_out_bf16.shape[0] // 2
    return pl.pallas_call(
        _outproj_kernel,
        grid=(M // tm,),
        in_specs=[pl.BlockSpec((tm, half), lambda i: (i, 0)),
                  pl.BlockSpec((tm, half), lambda i: (i, 0)),
                  pl.BlockSpec((tm, D), lambda i: (i, 0)),
                  pl.BlockSpec((half, D), lambda i: (0, 0)),
                  pl.BlockSpec((half, D), lambda i: (1, 0)),
                  pl.BlockSpec((1, D), lambda i: (0, 0)),
                  pl.BlockSpec((1, D), lambda i: (0, 0))],
        out_specs=[pl.BlockSpec((tm, D), lambda i: (i, 0)),
                   pl.BlockSpec((tm, D), lambda i: (i, 0))],
        out_shape=[jax.ShapeDtypeStruct((M, D), F32), jax.ShapeDtypeStruct((M, D), BF16)],
        compiler_params=_cparams(("parallel",)),
        name="outproj",
    )(og, ys, x2d, w_out_bf16, w_out_bf16, npost, npre)


def _ffn_kernel(xn_ref, h_ref, sa_ref, sb_ref, wa_ref, wb_ref, cwa_ref, cwb_ref, cba_ref, cbb_ref,
                wo_ref, npost_ref, y_ref, na_ref, nb_ref, acc_scr, ca_scr, cb_scr, *, nseq, L, tps):
    i = pl.program_id(0)
    j = pl.program_id(1)
    tm = nseq * L
    tf = wa_ref.shape[1]

    @pl.when(i % tps == 0)
    def _():
        ca_scr[j] = sa_ref[...]
        cb_scr[j] = sb_ref[...]

    @pl.when(j == 0)
    def _():
        acc_scr[...] = jnp.zeros_like(acc_scr)

    xn = xn_ref[...]
    ua = _dot(xn, wa_ref[...]).reshape(nseq, L, tf)
    ub = _dot(xn, wb_ref[...]).reshape(nseq, L, tf)
    a = _conv_taps(ua, ca_scr[j], cwa_ref[...], cba_ref[...], FFN_CONV)
    b = _conv_taps(ub, cb_scr[j], cwb_ref[...], cbb_ref[...], FFN_CONV)
    na = ua[:, L - (FFN_CONV - 1):L, :]
    nb_ = ub[:, L - (FFN_CONV - 1):L, :]
    ca_scr[j] = na
    cb_scr[j] = nb_
    na_ref[...] = na
    nb_ref[...] = nb_
    act = (_silu(a) * b).reshape(tm, tf).astype(BF16)
    acc_scr[...] += _dot(act, wo_ref[...])

    @pl.when(j == pl.num_programs(1) - 1)
    def _():
        y_ref[...] = h_ref[...] + _rms(acc_scr[...], npost_ref[...])


def _ffn(xn2, h2d, st, w_in_bf16, cw, cb, w_out_bf16, npost, *, nseq, L, tps, tf):
    M, D = h2d.shape
    F = w_out_bf16.shape[0]
    nj = F // tf
    tm = nseq * L
    W1 = FFN_CONV - 1
    kern = functools.partial(_ffn_kernel, nseq=nseq, L=L, tps=tps)
    stspec_a = pl.BlockSpec((nseq, W1, tf), lambda i, j: (i // tps, 0, j))
    stspec_b = pl.BlockSpec((nseq, W1, tf), lambda i, j: (i // tps, 0, j + nj))
    return pl.pallas_call(
        kern,
        grid=(M // tm, nj),
        in_specs=[pl.BlockSpec((tm, D), lambda i, j: (i, 0)),
                  pl.BlockSpec((tm, D), lambda i, j: (i, 0)),
                  stspec_a, stspec_b,
                  pl.BlockSpec((D, tf), lambda i, j: (0, j)),
                  pl.BlockSpec((D, tf), lambda i, j: (0, j + nj)),
                  pl.BlockSpec((FFN_CONV, tf), lambda i, j: (0, j)),
                  pl.BlockSpec((FFN_CONV, tf), lambda i, j: (0, j + nj)),
                  pl.BlockSpec((1, tf), lambda i, j: (0, j)),
                  pl.BlockSpec((1, tf), lambda i, j: (0, j + nj)),
                  pl.BlockSpec((tf, D), lambda i, j: (j, 0)),
                  pl.BlockSpec((1, D), lambda i, j: (0, 0))],
        out_specs=[pl.BlockSpec((tm, D), lambda i, j: (i, 0)),
                   pl.BlockSpec((nseq, W1, tf), lambda i, j: (i, 0, j)),
                   pl.BlockSpec((nseq, W1, tf), lambda i, j: (i, 0, j))],
        out_shape=[jax.ShapeDtypeStruct((M, D), F32),
                   jax.ShapeDtypeStruct((M // L, W1, F), F32),
                   jax.ShapeDtypeStruct((M // L, W1, F), F32)],
        scratch_shapes=[pltpu.VMEM((tm, D), F32),
                        pltpu.VMEM((nj, nseq, W1, tf), F32),
                        pltpu.VMEM((nj, nseq, W1, tf), F32)],
        compiler_params=_cparams(("arbitrary", "arbitrary")),
        name="ffn",
    )(xn2, h2d, st, st, w_in_bf16, w_in_bf16, cw, cw, cb, cb, w_out_bf16, npost)


def _layer(x, s_gla, s_ssd, s_conv, s_ffn, p, *, nb, c, ffn_nseq, ffn_L, ffn_tps):
    B, L, D = x.shape
    M = B * L
    x2d = x.reshape(M, D)
    proj = _inproj(x2d, p['n_mix_pre'], p['w_in'], 512, 1536)
    og, g_new = _gla(proj, s_gla, p['w2p'], p['bg'], p['gn'], B=B, L=L, nb=nb, c=c)
    ys, h_new = _ssd(proj, s_conv, s_ssd, p['ssd_cw'], p['ssd_cb'], p['dtb'], p['alog'], p['de'],
                     p['ssd_nw'], B=B, L=L, nb=nb, c=c)
    hres, xn2 = _outproj(og, ys, x2d, p['w_out'], p['n_mix_post'], p['n_ffn_pre'], 512)
    y, fa, fb = _ffn(xn2, hres, s_ffn, p['ffn_w_in'], p['ffn_cw'], p['ffn_cb'], p['ffn_w_out'],
                     p['n_ffn_post'], nseq=ffn_nseq, L=ffn_L, tps=ffn_tps, tf=512)
    c_new = proj.reshape(B, L, -1)[:, L - (SSD_CONV - 1):, COL_XS:COL_XS + SSD_CONV_CH]
    f_new = jnp.concatenate([fa, fb], axis=-1)[ffn_tps - 1::ffn_tps]
    return y.reshape(B, L, D), g_new, h_new, c_new, f_new


def _prep_params(l, norm_mix_pre, norm_mix_post, norm_ffn_pre, norm_ffn_post, w_in, gla_w_gate2,
                 gla_b_gate, gla_norm, ssd_conv_w, ssd_conv_b, ssd_dt_bias, ssd_A_log, ssd_D, ssd_norm,
                 w_out, ffn_w_in, ffn_conv_w, ffn_conv_b, ffn_w_out):
    D = w_in.shape[1]
    sizes = (GLA_KDIM, GLA_KDIM, GLA_WIDTH, GLA_WIDTH, GLA_LOWRANK, SSD_WIDTH, SSD_CONV_CH, SSD_HEADS)
    offs = np.cumsum((0,) + sizes)
    wq, wk, wv, wg, wlr, wz, wxbc, wdt = [w_in[l][:, offs[i]:offs[i + 1]] for i in range(8)]
    npad = IN_COLS_PAD - (COL_SMALL + GLA_LOWRANK + SSD_HEADS)
    w_in_r = jnp.concatenate([wq, wk, wv, wg, wz, wxbc, wlr, wdt, jnp.zeros((D, npad), F32)],
                             axis=1).astype(BF16)
    w2p = jnp.zeros((SMALL_W, GLA_KDIM), F32).at[0:GLA_LOWRANK].set(gla_w_gate2[l])
    pad_small = lambda v: jnp.zeros((1, SMALL_W), F32).at[0, DTR_OFF:DTR_OFF + SSD_HEADS].set(v)
    row = lambda v: v.reshape(1, -1)
    return dict(
        n_mix_pre=row(norm_mix_pre[l]), n_mix_post=row(norm_mix_post[l]),
        n_ffn_pre=row(norm_ffn_pre[l]), n_ffn_post=row(norm_ffn_post[l]),
        w_in=w_in_r, w2p=w2p, bg=row(gla_b_gate[l]), gn=row(gla_norm[l]),
        ssd_cw=ssd_conv_w[l], ssd_cb=row(ssd_conv_b[l]),
        dtb=pad_small(ssd_dt_bias[l]), alog=pad_small(ssd_A_log[l]),
        de=row(jnp.repeat(ssd_D[l], SSD_HEADDIM)), ssd_nw=row(ssd_norm[l]),
        w_out=w_out[l].astype(BF16), ffn_w_in=ffn_w_in[l].astype(BF16),
        ffn_cw=ffn_conv_w[l], ffn_cb=row(ffn_conv_b[l]), ffn_w_out=ffn_w_out[l].astype(BF16))


def kernel(x_prompt, x_sample, state_gla, state_ssd, state_ssd_conv, state_ffn_conv, norm_mix_pre,
           norm_mix_post, norm_ffn_pre, norm_ffn_post, w_in, gla_w_gate2, gla_b_gate, gla_norm,
           ssd_conv_w, ssd_conv_b, ssd_dt_bias, ssd_A_log, ssd_D, ssd_norm, w_out, ffn_w_in,
           ffn_conv_w, ffn_conv_b, ffn_w_out):
    depth = w_in.shape[0]
    xp, xs = x_prompt, x_sample
    Bp, Lp, D = xp.shape
    Bs, Ls, _ = xs.shape
    F2 = ffn_w_in.shape[2]
    outs = [[] for _ in range(8)]
    for l in range(depth):
        p = _prep_params(l, norm_mix_pre, norm_mix_post, norm_ffn_pre, norm_ffn_post, w_in,
                         gla_w_gate2, gla_b_gate, gla_norm, ssd_conv_w, ssd_conv_b, ssd_dt_bias,
                         ssd_A_log, ssd_D, ssd_norm, w_out, ffn_w_in, ffn_conv_w, ffn_conv_b, ffn_w_out)
        cp = CHUNK
        xp, g1, h1, c1, f1 = _layer(
            xp,
            jnp.zeros((Bp, GLA_HEADS, GLA_HEAD_K, GLA_HEAD_V), F32),
            jnp.zeros((Bp, SSD_HEADS, SSD_HEADDIM, SSD_STATE), F32),
            jnp.zeros((Bp, SSD_CONV - 1, SSD_CONV_CH), F32),
            jnp.zeros((Bp, FFN_CONV - 1, F2), F32),
            p, nb=1, c=cp, ffn_nseq=1, ffn_L=512, ffn_tps=Lp // 512)
        nbs = CHUNK // Ls
        xs, g2, h2, c2, f2 = _layer(
            xs, state_gla[l], state_ssd[l], state_ssd_conv[l], state_ffn_conv[l],
            p, nb=nbs, c=Ls, ffn_nseq=512 // Ls, ffn_L=Ls, ffn_tps=1)
        for lst, val in zip(outs, (g1, h1, c1, f1, g2, h2, c2, f2)):
            lst.append(val)
    return (xp, xs) + tuple(jnp.stack(o) for o in outs)
```

```python
import functools

import numpy as np
import jax
import jax.numpy as jnp
from jax import lax
from jax.experimental import pallas as pl
from jax.experimental.pallas import tpu as pltpu

F32 = jnp.float32
BF16 = jnp.bfloat16
EPS = 1e-6

GLA_HEADS = 4
GLA_HEAD_K = 128
GLA_HEAD_V = 256
GLA_KDIM = GLA_HEADS * GLA_HEAD_K
GLA_WIDTH = GLA_HEADS * GLA_HEAD_V
GLA_LOWRANK = 16
GLA_GATE_NORMALIZER = 16.0
SSD_HEADS = 16
SSD_HEADDIM = 64
SSD_STATE = 128
SSD_GROUPS = 2
SSD_WIDTH = SSD_HEADS * SSD_HEADDIM
SSD_GROUP_WIDTH = SSD_WIDTH // SSD_GROUPS
SSD_CONV = 4
SSD_BC = 2 * SSD_GROUPS * SSD_STATE
SSD_CONV_CH = SSD_WIDTH + SSD_BC
FFN_CONV = 3
CHUNK = 64

A_Q = 0
A_K = GLA_KDIM
A_V = 2 * GLA_KDIM
A_G = A_V + GLA_WIDTH
A_W = A_G + GLA_WIDTH
B_Z = 0
B_XS = SSD_WIDTH
B_BC = B_XS + SSD_WIDTH
B_SMALL = B_BC + SSD_BC
SMALL_W = 128
DTR_OFF = GLA_LOWRANK
B_W = B_SMALL + SMALL_W

VMEM_LIMIT = 56 * 1024 * 1024


def _cparams(sem):
    return pltpu.CompilerParams(dimension_semantics=sem, vmem_limit_bytes=VMEM_LIMIT)


def _split3(x):
    hi = x.astype(BF16)
    r = x - hi.astype(F32)
    mid = r.astype(BF16)
    lo = (r - mid.astype(F32)).astype(BF16)
    return hi, mid, lo


def _dot(a, b):
    return jnp.dot(a, b, preferred_element_type=F32)


def _dot_nt(a, b):
    return lax.dot_general(a, b, (((1,), (1,)), ((), ())), preferred_element_type=F32)


def _dot_tn(a, b):
    return lax.dot_general(a, b, (((0,), (0,)), ((), ())), preferred_element_type=F32)


def _dot_sel(sel_bf16, x_f32):
    hi, mid, lo = _split3(x_f32)
    return _dot(sel_bf16, hi) + _dot(sel_bf16, mid) + _dot(sel_bf16, lo)


def _silu(x):
    return x / (1.0 + jnp.exp(-x))


def _softplus(x):
    return jnp.maximum(x, 0.0) + jnp.log1p(jnp.exp(-jnp.abs(x)))


def _rms(x, w):
    return x * lax.rsqrt(jnp.mean(x * x, axis=-1, keepdims=True) + EPS) * w


def _chunk_consts(nb, c):
    R = nb * c
    idx = np.arange(R)
    seq, pos = idx // c, idx % c
    same = seq[:, None] == seq[None, :]
    t, u = pos[:, None], pos[None, :]
    blocks = [same & (u <= t), same & (u > t)]
    masks = [np.eye(R, dtype=bool)]
    m = c // 2
    while m >= 1:
        blk = pos // (2 * m)
        rho = blk * 2 * m + m - 1
        upper = pos > rho
        a_up = upper[:, None] & (u > rho[:, None]) & (u <= t)
        a_lo = (~upper)[:, None] & (u > t) & (u <= rho[:, None])
        blocks.append(same & (a_up | a_lo))
        masks.append(same & upper[:, None] & (~upper)[None, :] & (blk[:, None] == blk[None, :]))
        m //= 2
    sel = np.concatenate(blocks, 0).astype(np.float32)
    lvl = np.stack(masks).astype(np.float32)
    causal = (same & (u <= t)).astype(np.float32)
    return sel, lvl, causal


def _inproj_kernel(x_ref, nw_ref, w_ref, o_ref):
    xn = _rms(x_ref[...], nw_ref[...]).astype(BF16)
    o_ref[...] = _dot(xn, w_ref[...])


def _inproj(x2d, nw, w_bf16, n_out, tm):
    M, D = x2d.shape
    return pl.pallas_call(
        _inproj_kernel,
        grid=(M // tm,),
        in_specs=[pl.BlockSpec((tm, D), lambda i: (i, 0)),
                  pl.BlockSpec((1, D), lambda i: (0, 0)),
                  pl.BlockSpec((D, n_out), lambda i: (0, 0))],
        out_specs=pl.BlockSpec((tm, n_out), lambda i: (i, 0)),
        out_shape=jax.ShapeDtypeStruct((M, n_out), F32),
        compiler_params=_cparams(("parallel",)),
        name="inproj",
    )(x2d, nw, w_bf16)


def _gla_kernel(q_ref, k_ref, v_ref, g_ref, sm_ref, s0_ref, sel_ref, lvl_ref, w2_ref, bg_ref, gn_ref,
                o_ref, sn_ref, s_scr, *, nb, c, nlev):
    R = nb * c
    ci = pl.program_id(1)

    @pl.when(ci == 0)
    def _():
        s_scr[...] = s0_ref[...]

    zg = jnp.dot(sm_ref[...], w2_ref[...], precision=lax.Precision.HIGHEST,
                 preferred_element_type=F32) + bg_ref[...]
    lg = -_softplus(-zg) * (1.0 / GLA_GATE_NORMALIZER)
    E = _dot_sel(sel_ref[...], lg)
    b = E[0:R]
    eb = jnp.exp(b)
    q = q_ref[...] * (GLA_HEAD_K ** -0.5)
    k = k_ref[...]
    qe = q * eb
    ke = k * jnp.exp(E[R:2 * R])
    ql, kl = [q.astype(BF16)], [k.astype(BF16)]
    for l in range(nlev):
        p = jnp.exp(E[(2 + l) * R:(3 + l) * R])
        ql.append((q * p).astype(BF16))
        kl.append((k * p).astype(BF16))
    vb = v_ref[...].astype(BF16)
    g = g_ref[...]
    gn = gn_ref[...]
    rows = lax.broadcasted_iota(jnp.int32, (R, 1), 0)

    for h in range(GLA_HEADS):
        ks = slice(h * GLA_HEAD_K, (h + 1) * GLA_HEAD_K)
        vs = slice(h * GLA_HEAD_V, (h + 1) * GLA_HEAD_V)
        att = jnp.zeros((R, R), F32)
        for l in range(nlev + 1):
            att = att + _dot_nt(ql[l][:, ks], kl[l][:, ks]) * lvl_ref[l]
        o = _dot(att.astype(BF16), vb[:, vs])
        for s in range(nb):
            if nb > 1:
                rm = jnp.logical_and(rows >= s * c, rows < (s + 1) * c).astype(F32)
                qs = (qe[:, ks] * rm).astype(BF16)
                kd = (ke[:, ks] * rm).astype(BF16)
            else:
                qs = qe[:, ks].astype(BF16)
                kd = ke[:, ks].astype(BF16)
            S = s_scr[s, h]
            o = o + _dot(qs, S.astype(BF16))
            upd = _dot_tn(kd, vb[:, vs])
            d = eb[s * c + c - 1:s * c + c, ks]
            dcol = jnp.transpose(jnp.broadcast_to(d, (GLA_HEAD_K, GLA_HEAD_K)))
            s_scr[s, h] = jnp.concatenate([dcol, dcol], axis=1) * S + upd
        gh = g[:, vs]
        o_ref[:, vs] = (_rms(o, gn) * _silu(gh)).astype(BF16)

    @pl.when(ci == pl.num_programs(1) - 1)
    def _():
        sn_ref[...] = s_scr[...]


def _gla(proj_a, proj_b, s0, w2p, bg, gn, *, B, L, nb, c):
    R = nb * c
    ncl = L // c
    nlev = int(np.log2(c))
    sel, lvl, _ = _chunk_consts(nb, c)
    sel = jnp.asarray(sel, BF16)
    lvl = jnp.asarray(lvl, F32)
    M = proj_a.shape[0]

    def rowblk(w, col):
        return pl.BlockSpec((R, w), lambda bi, ci: (bi * ncl + ci, col // w))

    const2 = lambda a: pl.BlockSpec(a.shape, lambda bi, ci: (0,) * a.ndim)
    sspec = pl.BlockSpec((nb, GLA_HEADS, GLA_HEAD_K, GLA_HEAD_V), lambda bi, ci: (bi, 0, 0, 0))
    kern = functools.partial(_gla_kernel, nb=nb, c=c, nlev=nlev)
    return pl.pallas_call(
        kern,
        grid=(B // nb, ncl),
        in_specs=[rowblk(GLA_KDIM, A_Q), rowblk(GLA_KDIM, A_K), rowblk(GLA_WIDTH, A_V),
                  rowblk(GLA_WIDTH, A_G), rowblk(SMALL_W, B_SMALL), sspec,
                  const2(sel), const2(lvl), const2(w2p), const2(bg), const2(gn)],
        out_specs=[pl.BlockSpec((R, GLA_WIDTH), lambda bi, ci: (bi * ncl + ci, 0)), sspec],
        out_shape=[jax.ShapeDtypeStruct((M, GLA_WIDTH), BF16),
                   jax.ShapeDtypeStruct((B, GLA_HEADS, GLA_HEAD_K, GLA_HEAD_V), F32)],
        scratch_shapes=[pltpu.VMEM((nb, GLA_HEADS, GLA_HEAD_K, GLA_HEAD_V), F32)],
        compiler_params=_cparams(("parallel", "arbitrary")),
        name="gla",
    )(proj_a, proj_a, proj_a, proj_a, proj_b, s0, sel, lvl, w2p, bg, gn)


def _conv_taps(u3, prev, w, bias, width):
    n, L, C = u3.shape
    pos = lax.broadcasted_iota(jnp.int32, (1, L, 1), 1)
    acc = bias + w[width - 1:width] * u3
    for j in range(width - 1):
        d = width - 1 - j
        r = pltpu.roll(u3, d, axis=1)
        for tt in range(d):
            r = jnp.where(pos == tt, prev[:, j + tt:j + tt + 1, :], r)
        acc = acc + w[j:j + 1] * r
    return acc


def _ssd_kernel(z_ref, xs_ref, bc_ref, sm_ref, cst_ref, h0_ref, sel_ref, cm_ref, ex_ref,
                cw_ref, cb_ref, dtb_ref, alog_ref, de_ref, nw_ref,
                y_ref, hn_ref, ht_scr, cx_scr, cbc_scr, *, nb, c):
    R = nb * c
    ci = pl.program_id(1)
    HP = SSD_WIDTH

    @pl.when(ci == 0)
    def _():
        cx_scr[...] = cst_ref[:, :, 0:SSD_WIDTH]
        cbc_scr[...] = cst_ref[:, :, SSD_WIDTH:SSD_CONV_CH]
        for s in range(nb):
            ht_scr[s] = jnp.transpose(h0_ref[s].reshape(HP, SSD_STATE))

    cw = cw_ref[...]
    cbias = cb_ref[...]
    xraw = xs_ref[...].reshape(nb, c, SSD_WIDTH)
    bcraw = bc_ref[...].reshape(nb, c, SSD_BC)
    xs = _silu(_conv_taps(xraw, cx_scr[...], cw[:, 0:SSD_WIDTH], cbias[:, 0:SSD_WIDTH], SSD_CONV))
    bca = _silu(_conv_taps(bcraw, cbc_scr[...], cw[:, SSD_WIDTH:SSD_CONV_CH],
                           cbias[:, SSD_WIDTH:SSD_CONV_CH], SSD_CONV))
    cx_scr[...] = xraw[:, c - (SSD_CONV - 1):c, :]
    cbc_scr[...] = bcraw[:, c - (SSD_CONV - 1):c, :]
    xs = xs.reshape(R, SSD_WIDTH)
    bca = bca.reshape(R, SSD_BC)
    Bm = bca[:, 0:SSD_GROUPS * SSD_STATE]
    Cm = bca[:, SSD_GROUPS * SSD_STATE:SSD_BC]

    dt = _softplus(sm_ref[...] + dtb_ref[...])
    la = dt * (-jnp.exp(alog_ref[...]))
    cs = _dot_sel(sel_ref[...], la)
    cum = cs[0:R]
    ex = ex_ref[...]
    stack = jnp.concatenate([cs, dt], axis=0)
    s_hi, s_mid, s_lo = _split3(stack)
    st_e = _dot(s_hi, ex) + _dot(s_mid, ex) + _dot(s_lo, ex)
    cum_e = st_e[0:R]
    lmc_e = st_e[R:2 * R]
    dt_e = st_e[2 * R:3 * R]
    ecum_e = jnp.exp(cum_e)
    xdt = xs * dt_e
    xw = (xdt * jnp.exp(lmc_e)).astype(BF16)
    xdtb = xdt.astype(BF16)
    Bb = Bm.astype(BF16)
    Cb = Cm.astype(BF16)

    pad = jnp.zeros((128 - R, 128), F32) if R < 128 else None
    cum_p = cum if pad is None else jnp.concatenate([cum, pad], axis=0)
    cumT = jnp.transpose(cum_p)
    cmask = cm_ref[...] > 0.5
    rows = lax.broadcasted_iota(jnp.int32, (R, 1), 0)
    lane = lax.broadcasted_iota(jnp.int32, (1, 128), 1)
    lo_half = lane < SSD_HEADDIM

    hpg = SSD_HEADS // SSD_GROUPS
    for gi in range(SSD_GROUPS):
        ns = slice(gi * SSD_STATE, (gi + 1) * SSD_STATE)
        gs = slice(gi * SSD_GROUP_WIDTH, (gi + 1) * SSD_GROUP_WIDTH)
        cbm = _dot_nt(Cb[:, ns], Bb[:, ns])
        yg = []
        for j in range(hpg // 2):
            ls = slice(gi * SSD_GROUP_WIDTH + j * 128, gi * SSD_GROUP_WIDTH + (j + 1) * 128)
            xpair = xdt[:, ls]
            acc = None
            for e in range(2):
                hh = DTR_OFF + gi * hpg + 2 * j + e
                diff = cum[:, hh:hh + 1] - cumT[hh:hh + 1, 0:R]
                mh = cbm * jnp.exp(jnp.where(cmask, diff, -jnp.inf))
                xh = jnp.where(lo_half if e == 0 else jnp.logical_not(lo_half), xpair, 0.0)
                t = _dot(mh.astype(BF16), xh.astype(BF16))
                acc = t if acc is None else acc + t
            yg.append(acc)
        y_intra = jnp.concatenate(yg, axis=1)
        y_inter = jnp.zeros((R, SSD_GROUP_WIDTH), F32)
        for s in range(nb):
            if nb > 1:
                rm = jnp.logical_and(rows >= s * c, rows < (s + 1) * c).astype(F32)
                cg = (Cm[:, ns] * rm).astype(BF16)
                bg_ = (Bm[:, ns] * rm).astype(BF16)
            else:
                cg = Cb[:, ns]
                bg_ = Bb[:, ns]
            hT = ht_scr[s, :, gs]
            y_inter = y_inter + _dot(cg, hT.astype(BF16))
            upd = _dot_tn(bg_, xw[:, gs])
            dl = ecum_e[s * c + c - 1:s * c + c, gs]
            ht_scr[s, :, gs] = dl * hT + upd
        y = y_intra + y_inter * ecum_e[:, gs] + de_ref[:, gs] * xs[:, gs]
        y = y * _silu(z_ref[:, gs])
        y_ref[:, gs] = _rms(y, nw_ref[:, gs]).astype(BF16)

    @pl.when(ci == pl.num_programs(1) - 1)
    def _():
        for s in range(nb):
            hn_ref[s] = jnp.transpose(ht_scr[s]).reshape(SSD_HEADS, SSD_HEADDIM, SSD_STATE)


def _ssd(proj, cst, h0, cw, cb, dtb, alog, de, nw, *, B, L, nb, c):
    R = nb * c
    ncl = L // c
    sel, _, causal = _chunk_consts(nb, c)
    sel = jnp.asarray(sel[:2 * R], BF16)
    causal = jnp.asarray(causal, F32)
    exn = np.zeros((SMALL_W, SSD_WIDTH), np.float32)
    for h in range(SSD_HEADS):
        exn[DTR_OFF + h, h * SSD_HEADDIM:(h + 1) * SSD_HEADDIM] = 1.0
    ex = jnp.asarray(exn, BF16)
    M = proj.shape[0]

    def rowblk(w, col):
        return pl.BlockSpec((R, w), lambda bi, ci: (bi * ncl + ci, col // w))

    const2 = lambda a: pl.BlockSpec(a.shape, lambda bi, ci: (0,) * a.ndim)
    hspec = pl.BlockSpec((nb, SSD_HEADS, SSD_HEADDIM, SSD_STATE), lambda bi, ci: (bi, 0, 0, 0))
    kern = functools.partial(_ssd_kernel, nb=nb, c=c)
    return pl.pallas_call(
        kern,
        grid=(B // nb, ncl),
        in_specs=[rowblk(SSD_WIDTH, B_Z), rowblk(SSD_WIDTH, B_XS), rowblk(SSD_BC, B_BC),
                  rowblk(SMALL_W, B_SMALL),
                  pl.BlockSpec((nb, SSD_CONV - 1, SSD_CONV_CH), lambda bi, ci: (bi, 0, 0)), hspec,
                  const2(sel), const2(causal), const2(ex),
                  const2(cw), const2(cb), const2(dtb), const2(alog), const2(de), const2(nw)],
        out_specs=[pl.BlockSpec((R, SSD_WIDTH), lambda bi, ci: (bi * ncl + ci, 0)), hspec],
        out_shape=[jax.ShapeDtypeStruct((M, SSD_WIDTH), BF16),
                   jax.ShapeDtypeStruct((B, SSD_HEADS, SSD_HEADDIM, SSD_STATE), F32)],
        scratch_shapes=[pltpu.VMEM((nb, SSD_STATE, SSD_WIDTH), F32),
                        pltpu.VMEM((nb, SSD_CONV - 1, SSD_WIDTH), F32),
                        pltpu.VMEM((nb, SSD_CONV - 1, SSD_BC), F32)],
        compiler_params=_cparams(("parallel", "arbitrary")),
        name="ssd",
    )(proj, proj, proj, proj, cst, h0, sel, causal, ex, cw, cb, dtb, alog, de, nw)


def _outproj_kernel(o_ref, y_ref, x_ref, wt_ref, wb_ref, npost_ref, npre_ref, h_ref, xn_ref):
    mix = _dot(o_ref[...], wt_ref[...]) + _dot(y_ref[...], wb_ref[...])
    h = x_ref[...] + _rms(mix, npost_ref[...])
    h_ref[...] = h
    xn_ref[...] = _rms(h, npre_ref[...]).astype(BF16)


def _outproj(og, ys, x2d, w_out_bf16, npost, npre, tm):
    M, D = x2d.shape
    half = w_out_bf16.shape[0] // 2
    return pl.pallas_call(
        _outproj_kernel,
        grid=(M // tm,),
        in_specs=[pl.BlockSpec((tm, half), lambda i: (i, 0)),
                  pl.BlockSpec((tm, half), lambda i: (i, 0)),
                  pl.BlockSpec((tm, D), lambda i: (i, 0)),
                  pl.BlockSpec((half, D), lambda i: (0, 0)),
                  pl.BlockSpec((half, D), lambda i: (1, 0)),
                  pl.BlockSpec((1, D), lambda i: (0, 0)),
                  pl.BlockSpec((1, D), lambda i: (0, 0))],
        out_specs=[pl.BlockSpec((tm, D), lambda i: (i, 0)),
                   pl.BlockSpec((tm, D), lambda i: (i, 0))],
        out_shape=[jax.ShapeDtypeStruct((M, D), F32), jax.ShapeDtypeStruct((M, D), BF16)],
        compiler_params=_cparams(("parallel",)),
        name="outproj",
    )(og, ys, x2d, w_out_bf16, w_out_bf16, npost, npre)


FFN_SUB = 512
FFN_HDR = 8
FFN_RBLK = 64


def _ffn_kernel(xn_ref, h_ref, sa_ref, sb_ref, wa_ref, wb_ref, cwa_ref, cwb_ref, cba_ref, cbb_ref,
                wo_ref, npost_ref, y_ref, na_ref, nb_ref, acc_scr, ua_scr, ub_scr, act_scr, *carry,
                nseq, L, tps):
    i = pl.program_id(0)
    j = pl.program_id(1)
    tm = nseq * L
    nsub = wa_ref.shape[1] // FFN_SUB
    W1 = FFN_CONV - 1
    H = FFN_HDR
    first = i % tps == 0
    lanes = [slice(s * FFN_SUB, (s + 1) * FFN_SUB) for s in range(nsub)]

    if tps > 1:
        ca_scr, cb_scr = carry

        @pl.when(jnp.logical_not(first))
        def _():
            for s in range(nsub):
                ua_scr[s, :, 0:H, :] = ca_scr[j, s]
                ub_scr[s, :, 0:H, :] = cb_scr[j, s]

    @pl.when(first)
    def _():
        for s in range(nsub):
            ua_scr[s, :, H - W1:H, :] = sa_ref[:, :, lanes[s]]
            ub_scr[s, :, H - W1:H, :] = sb_ref[:, :, lanes[s]]

    @pl.when(j == 0)
    def _():
        acc_scr[...] = jnp.zeros_like(acc_scr)

    for s in range(nsub):
        ua_scr[s, :, H:H + L, :] = _dot(xn_ref[...], wa_ref[:, lanes[s]]).reshape(nseq, L, FFN_SUB)
        ub_scr[s, :, H:H + L, :] = _dot(xn_ref[...], wb_ref[:, lanes[s]]).reshape(nseq, L, FFN_SUB)

    if L >= FFN_RBLK:
        blocks = [(0, nseq, r, r + FFN_RBLK) for r in range(0, L, FFN_RBLK)]
    else:
        qb = FFN_RBLK // L
        blocks = [(q, q + qb, 0, L) for q in range(0, nseq, qb)]
    for s in range(nsub):
        cwa = cwa_ref[:, lanes[s]]
        cwb = cwb_ref[:, lanes[s]]
        ba = cba_ref[:, lanes[s]]
        bb = cbb_ref[:, lanes[s]]
        for (q0, q1, r0, r1) in blocks:
            def conv(u_scr, w, bias):
                out = bias + w[W1:W1 + 1] * u_scr[s, q0:q1, H + r0:H + r1, :]
                for t in range(W1):
                    d = W1 - t
                    out = out + w[t:t + 1] * u_scr[s, q0:q1, H - d + r0:H - d + r1, :]
                return out
            a = conv(ua_scr, cwa, ba)
            b = conv(ub_scr, cwb, bb)
            act = (_silu(a) * b).reshape((q1 - q0) * (r1 - r0), FFN_SUB)
            row0 = q0 * L + r0
            act_scr[s, row0:row0 + act.shape[0], :] = act.astype(BF16)
        acc_scr[...] += _dot(act_scr[s], wo_ref[lanes[s], :])

    for s in range(nsub):
        na_ref[:, :, lanes[s]] = ua_scr[s, :, H + L - W1:H + L, :]
        nb_ref[:, :, lanes[s]] = ub_scr[s, :, H + L - W1:H + L, :]
        if tps > 1:
            ca_scr[j, s] = ua_scr[s, :, L:L + H, :]
            cb_scr[j, s] = ub_scr[s, :, L:L + H, :]

    @pl.when(j == pl.num_programs(1) - 1)
    def _():
        y_ref[...] = h_ref[...] + _rms(acc_scr[...], npost_ref[...])


def _ffn(xn2, h2d, st, w_in_bf16, cw, cb, w_out_bf16, npost, *, nseq, L, tps, tf):
    M, D = h2d.shape
    F = w_out_bf16.shape[0]
    nj = F // tf
    nsub = tf // FFN_SUB
    tm = nseq * L
    W1 = FFN_CONV - 1
    kern = functools.partial(_ffn_kernel, nseq=nseq, L=L, tps=tps)
    stspec_a = pl.BlockSpec((nseq, W1, tf), lambda i, j: (i // tps, 0, j))
    stspec_b = pl.BlockSpec((nseq, W1, tf), lambda i, j: (i // tps, 0, j + nj))
    return pl.pallas_call(
        kern,
        grid=(M // tm, nj),
        in_specs=[pl.BlockSpec((tm, D), lambda i, j: (i, 0)),
                  pl.BlockSpec((tm, D), lambda i, j: (i, 0)),
                  stspec_a, stspec_b,
                  pl.BlockSpec((D, tf), lambda i, j: (0, j)),
                  pl.BlockSpec((D, tf), lambda i, j: (0, j + nj)),
                  pl.BlockSpec((FFN_CONV, tf), lambda i, j: (0, j)),
                  pl.BlockSpec((FFN_CONV, tf), lambda i, j: (0, j + nj)),
                  pl.BlockSpec((1, tf), lambda i, j: (0, j)),
                  pl.BlockSpec((1, tf), lambda i, j: (0, j + nj)),
                  pl.BlockSpec((tf, D), lambda i, j: (j, 0)),
                  pl.BlockSpec((1, D), lambda i, j: (0, 0))],
        out_specs=[pl.BlockSpec((tm, D), lambda i, j: (i, 0)),
                   pl.BlockSpec((nseq, W1, tf), lambda i, j: (i, 0, j)),
                   pl.BlockSpec((nseq, W1, tf), lambda i, j: (i, 0, j))],
        out_shape=[jax.ShapeDtypeStruct((M, D), F32),
                   jax.ShapeDtypeStruct((M // L, W1, F), F32),
                   jax.ShapeDtypeStruct((M // L, W1, F), F32)],
        scratch_shapes=[pltpu.VMEM((tm, D), F32),
                        pltpu.VMEM((nsub, nseq, FFN_HDR + L, FFN_SUB), F32),
                        pltpu.VMEM((nsub, nseq, FFN_HDR + L, FFN_SUB), F32),
                        pltpu.VMEM((nsub, tm, FFN_SUB), BF16)] + (
                            [pltpu.VMEM((nj, nsub, nseq, FFN_HDR, FFN_SUB), F32)] * 2 if tps > 1 else []),
        compiler_params=_cparams(("arbitrary", "arbitrary")),
        name="ffn",
    )(xn2, h2d, st, st, w_in_bf16, w_in_bf16, cw, cw, cb, cb, w_out_bf16, npost)


def _layer(x, s_gla, s_ssd, s_conv, s_ffn, p, *, nb, c, ffn_nseq, ffn_L, ffn_tps):
    B, L, D = x.shape
    M = B * L
    x2d = x.reshape(M, D)
    proj_a = _inproj(x2d, p['n_mix_pre'], p['w_in_a'], A_W, 512)
    proj_b = _inproj(x2d, p['n_mix_pre'], p['w_in_b'], B_W, 512)
    og, g_new = _gla(proj_a, proj_b, s_gla, p['w2p'], p['bg'], p['gn'], B=B, L=L, nb=nb, c=c)
    ys, h_new = _ssd(proj_b, s_conv, s_ssd, p['ssd_cw'], p['ssd_cb'], p['dtb'], p['alog'], p['de'],
                     p['ssd_nw'], B=B, L=L, nb=nb, c=c)
    hres, xn2 = _outproj(og, ys, x2d, p['w_out'], p['n_mix_post'], p['n_ffn_pre'], 512)
    y, fa, fb = _ffn(xn2, hres, s_ffn, p['ffn_w_in'], p['ffn_cw'], p['ffn_cb'], p['ffn_w_out'],
                     p['n_ffn_post'], nseq=ffn_nseq, L=ffn_L, tps=ffn_tps, tf=512)
    c_new = proj_b.reshape(B, L, -1)[:, L - (SSD_CONV - 1):, B_XS:B_XS + SSD_CONV_CH]
    f_new = jnp.concatenate([fa, fb], axis=-1)[ffn_tps - 1::ffn_tps]
    return y.reshape(B, L, D), g_new, h_new, c_new, f_new


def _prep_params(l, norm_mix_pre, norm_mix_post, norm_ffn_pre, norm_ffn_post, w_in, gla_w_gate2,
                 gla_b_gate, gla_norm, ssd_conv_w, ssd_conv_b, ssd_dt_bias, ssd_A_log, ssd_D, ssd_norm,
                 w_out, ffn_w_in, ffn_conv_w, ffn_conv_b, ffn_w_out):
    D = w_in.shape[1]
    sizes = (GLA_KDIM, GLA_KDIM, GLA_WIDTH, GLA_WIDTH, GLA_LOWRANK, SSD_WIDTH, SSD_CONV_CH, SSD_HEADS)
    offs = np.cumsum((0,) + sizes)
    wb16 = w_in[l].astype(BF16)
    npad = SMALL_W - GLA_LOWRANK - SSD_HEADS
    w_in_b = jnp.concatenate([wb16[:, offs[5]:offs[7]], wb16[:, offs[4]:offs[5]], wb16[:, offs[7]:offs[8]],
                              jnp.zeros((D, npad), BF16)], axis=1)
    w2p = jnp.zeros((SMALL_W, GLA_KDIM), F32).at[0:GLA_LOWRANK].set(gla_w_gate2[l])
    pad_small = lambda v: jnp.zeros((1, SMALL_W), F32).at[0, DTR_OFF:DTR_OFF + SSD_HEADS].set(v)
    row = lambda v: v.reshape(1, -1)
    return dict(
        n_mix_pre=row(norm_mix_pre[l]), n_mix_post=row(norm_mix_post[l]),
        n_ffn_pre=row(norm_ffn_pre[l]), n_ffn_post=row(norm_ffn_post[l]),
        w_in_a=wb16, w_in_b=w_in_b, w2p=w2p, bg=row(gla_b_gate[l]), gn=row(gla_norm[l]),
        ssd_cw=ssd_conv_w[l], ssd_cb=row(ssd_conv_b[l]),
        dtb=pad_small(ssd_dt_bias[l]), alog=pad_small(ssd_A_log[l]),
        de=row(jnp.repeat(ssd_D[l], SSD_HEADDIM)), ssd_nw=row(ssd_norm[l]),
        w_out=w_out[l].astype(BF16), ffn_w_in=ffn_w_in[l].astype(BF16),
        ffn_cw=ffn_conv_w[l], ffn_cb=row(ffn_conv_b[l]), ffn_w_out=ffn_w_out[l].astype(BF16))


def kernel(x_prompt, x_sample, state_gla, state_ssd, state_ssd_conv, state_ffn_conv, norm_mix_pre,
           norm_mix_post, norm_ffn_pre, norm_ffn_post, w_in, gla_w_gate2, gla_b_gate, gla_norm,
           ssd_conv_w, ssd_conv_b, ssd_dt_bias, ssd_A_log, ssd_D, ssd_norm, w_out, ffn_w_in,
           ffn_conv_w, ffn_conv_b, ffn_w_out):
    depth = w_in.shape[0]
    xp, xs = x_prompt, x_sample
    Bp, Lp, D = xp.shape
    Bs, Ls, _ = xs.shape
    F2 = ffn_w_in.shape[2]
    outs = [[] for _ in range(8)]
    for l in range(depth):
        p = _prep_params(l, norm_mix_pre, norm_mix_post, norm_ffn_pre, norm_ffn_post, w_in,
                         gla_w_gate2, gla_b_gate, gla_norm, ssd_conv_w, ssd_conv_b, ssd_dt_bias,
                         ssd_A_log, ssd_D, ssd_norm, w_out, ffn_w_in, ffn_conv_w, ffn_conv_b, ffn_w_out)
        cp = CHUNK
        xp, g1, h1, c1, f1 = _layer(
            xp,
            jnp.zeros((Bp, GLA_HEADS, GLA_HEAD_K, GLA_HEAD_V), F32),
            jnp.zeros((Bp, SSD_HEADS, SSD_HEADDIM, SSD_STATE), F32),
            jnp.zeros((Bp, SSD_CONV - 1, SSD_CONV_CH), F32),
            jnp.zeros((Bp, FFN_CONV - 1, F2), F32),
            p, nb=1, c=cp, ffn_nseq=1, ffn_L=512, ffn_tps=Lp // 512)
        nbs = CHUNK // Ls
        xs, g2, h2, c2, f2 = _layer(
            xs, state_gla[l], state_ssd[l], state_ssd_conv[l], state_ffn_conv[l],
            p, nb=nbs, c=Ls, ffn_nseq=512 // Ls, ffn_L=Ls, ffn_tps=1)
        for lst, val in zip(outs, (g1, h1, c1, f1, g2, h2, c2, f2)):
            lst.append(val)
    return (xp, xs) + tuple(jnp.stack(o) for o in outs)
```

```python
import functools

import numpy as np
import jax
import jax.numpy as jnp
from jax import lax
from jax.experimental import pallas as pl
from jax.experimental.pallas import tpu as pltpu

F32 = jnp.float32
BF16 = jnp.bfloat16
EPS = 1e-6

GLA_HEADS = 4
GLA_HEAD_K = 128
GLA_HEAD_V = 256
GLA_KDIM = GLA_HEADS * GLA_HEAD_K
GLA_WIDTH = GLA_HEADS * GLA_HEAD_V
GLA_LOWRANK = 16
GLA_GATE_NORMALIZER = 16.0
SSD_HEADS = 16
SSD_HEADDIM = 64
SSD_STATE = 128
SSD_GROUPS = 2
SSD_WIDTH = SSD_HEADS * SSD_HEADDIM
SSD_GROUP_WIDTH = SSD_WIDTH // SSD_GROUPS
SSD_CONV = 4
SSD_BC = 2 * SSD_GROUPS * SSD_STATE
SSD_CONV_CH = SSD_WIDTH + SSD_BC
FFN_CONV = 3
CHUNK = 64

A_Q = 0
A_K = GLA_KDIM
A_V = 2 * GLA_KDIM
A_G = A_V + GLA_WIDTH
A_W = A_G + GLA_WIDTH
B_Z = 0
B_XS = SSD_WIDTH
B_BC = B_XS + SSD_WIDTH
B_SMALL = B_BC + SSD_BC
SMALL_W = 128
DTR_OFF = GLA_LOWRANK
B_W = B_SMALL + SMALL_W

VMEM_LIMIT = 56 * 1024 * 1024


def _cparams(sem):
    return pltpu.CompilerParams(dimension_semantics=sem, vmem_limit_bytes=VMEM_LIMIT)


def _split3(x):
    hi = x.astype(BF16)
    r = x - hi.astype(F32)
    mid = r.astype(BF16)
    lo = (r - mid.astype(F32)).astype(BF16)
    return hi, mid, lo


def _dot(a, b):
    return jnp.dot(a, b, preferred_element_type=F32)


def _dot_nt(a, b):
    return lax.dot_general(a, b, (((1,), (1,)), ((), ())), preferred_element_type=F32)


def _dot_tn(a, b):
    return lax.dot_general(a, b, (((0,), (0,)), ((), ())), preferred_element_type=F32)


def _dot_sel(sel3_bf16, x_f32):
    return _dot(sel3_bf16, jnp.concatenate(_split3(x_f32), axis=0))


def _silu(x):
    return x / (1.0 + jnp.exp(-x))


def _softplus(x):
    return jnp.maximum(x, 0.0) + jnp.log1p(jnp.exp(-jnp.abs(x)))


def _rms(x, w):
    return x * lax.rsqrt(jnp.mean(x * x, axis=-1, keepdims=True) + EPS) * w


def _chunk_consts(nb, c):
    R = nb * c
    idx = np.arange(R)
    seq, pos = idx // c, idx % c
    same = seq[:, None] == seq[None, :]
    t, u = pos[:, None], pos[None, :]
    blocks = [same & (u <= t), same & (u > t)]
    masks = [np.eye(R, dtype=bool)]
    m = c // 2
    while m >= 1:
        blk = pos // (2 * m)
        rho = blk * 2 * m + m - 1
        upper = pos > rho
        a_up = upper[:, None] & (u > rho[:, None]) & (u <= t)
        a_lo = (~upper)[:, None] & (u > t) & (u <= rho[:, None])
        blocks.append(same & (a_up | a_lo))
        masks.append(same & upper[:, None] & (~upper)[None, :] & (blk[:, None] == blk[None, :]))
        m //= 2
    sel = np.concatenate(blocks, 0).astype(np.float32)
    lvl = np.stack(masks).astype(np.float32)
    causal = (same & (u <= t)).astype(np.float32)
    return sel, lvl, causal


def _inproj_kernel(x_ref, nw_ref, w_ref, o_ref):
    xn = _rms(x_ref[...], nw_ref[...]).astype(BF16)
    o_ref[...] = _dot(xn, w_ref[...])


def _inproj(x2d, nw, w_bf16, n_out, tm):
    M, D = x2d.shape
    return pl.pallas_call(
        _inproj_kernel,
        grid=(M // tm,),
        in_specs=[pl.BlockSpec((tm, D), lambda i: (i, 0)),
                  pl.BlockSpec((1, D), lambda i: (0, 0)),
                  pl.BlockSpec((D, n_out), lambda i: (0, 0))],
        out_specs=pl.BlockSpec((tm, n_out), lambda i: (i, 0)),
        out_shape=jax.ShapeDtypeStruct((M, n_out), F32),
        compiler_params=_cparams(("parallel",)),
        name="inproj",
    )(x2d, nw, w_bf16)


def _gla_group(gi, q_ref, k_ref, v_ref, g_ref, sm_ref, s_in, s_out, sel_ref, lvl_ref, w2_ref, bg_ref,
               gn_ref, o_ref, *, nb, c, nlev):
    R = nb * c
    sm = sm_ref[gi]
    sm_hi = sm.astype(BF16)
    sm_lo = (sm - sm_hi.astype(F32)).astype(BF16)
    zg = _dot(jnp.concatenate([sm_hi, sm_lo, sm_hi], axis=1), w2_ref[...]) + bg_ref[...]
    lg = -_softplus(-zg) * (1.0 / GLA_GATE_NORMALIZER)
    E = _dot_sel(sel_ref[...], lg)
    b = E[0:R]
    eb = jnp.exp(b)
    q = q_ref[gi] * (GLA_HEAD_K ** -0.5)
    k = k_ref[gi]
    qe = q * eb
    ke = k * jnp.exp(E[R:2 * R])
    ql, kl = [q.astype(BF16)], [k.astype(BF16)]
    for l in range(nlev):
        p = jnp.exp(E[(2 + l) * R:(3 + l) * R])
        ql.append((q * p).astype(BF16))
        kl.append((k * p).astype(BF16))
    vb = v_ref[gi].astype(BF16)
    g = g_ref[gi]
    gn = gn_ref[...]
    rows = lax.broadcasted_iota(jnp.int32, (R, 1), 0)

    for h in range(GLA_HEADS):
        ks = slice(h * GLA_HEAD_K, (h + 1) * GLA_HEAD_K)
        vs = slice(h * GLA_HEAD_V, (h + 1) * GLA_HEAD_V)
        att = jnp.zeros((R, R), F32)
        for l in range(nlev + 1):
            att = att + _dot_nt(ql[l][:, ks], kl[l][:, ks]) * lvl_ref[l]
        o = _dot(att.astype(BF16), vb[:, vs])
        for s in range(nb):
            if nb > 1:
                rm = jnp.logical_and(rows >= s * c, rows < (s + 1) * c).astype(F32)
                qs = (qe[:, ks] * rm).astype(BF16)
                kd = (ke[:, ks] * rm).astype(BF16)
            else:
                qs = qe[:, ks].astype(BF16)
                kd = ke[:, ks].astype(BF16)
            S = s_in[gi * nb + s, h]
            o = o + _dot(qs, S.astype(BF16))
            upd = _dot_tn(kd, vb[:, vs])
            d = eb[s * c + c - 1:s * c + c, ks]
            dcol = jnp.transpose(jnp.broadcast_to(d, (GLA_HEAD_K, GLA_HEAD_K)))
            s_out[gi * nb + s, h] = jnp.concatenate([dcol, dcol], axis=1) * S + upd
        gh = g[:, vs]
        o_ref[gi, :, vs] = (_rms(o, gn) * _silu(gh)).astype(BF16)


def _gla_kernel(q_ref, k_ref, v_ref, g_ref, sm_ref, s0_ref, sel_ref, lvl_ref, w2_ref, bg_ref, gn_ref,
                o_ref, sn_ref, *scratch, ns, nb, c, nlev):
    args = (q_ref, k_ref, v_ref, g_ref, sm_ref)
    consts = (sel_ref, lvl_ref, w2_ref, bg_ref, gn_ref, o_ref)
    kw = dict(nb=nb, c=c, nlev=nlev)
    if not scratch:
        for gi in range(ns):
            _gla_group(gi, *args, s0_ref, sn_ref, *consts, **kw)
        return
    s_scr, = scratch
    ci = pl.program_id(1)

    @pl.when(ci == 0)
    def _():
        s_scr[...] = s0_ref[...]

    for gi in range(ns):
        _gla_group(gi, *args, s_scr, s_scr, *consts, **kw)

    @pl.when(ci == pl.num_programs(1) - 1)
    def _():
        sn_ref[...] = s_scr[...]


def _group_view(a, G):
    return a.reshape(G, a.shape[0] // G, a.shape[1])


def _gla(proj_a, proj_b, s0, w2cat, bg, gn, *, B, L, nb, c, ns):
    R = nb * c
    G = B // nb
    Lg = L * nb
    ncl = Lg // R
    nlev = int(np.log2(c))
    sel, lvl, _ = _chunk_consts(nb, c)
    sel = jnp.asarray(np.tile(sel, (1, 3)), BF16)
    lvl = jnp.asarray(lvl, F32)
    M = proj_a.shape[0]
    pa, pb = _group_view(proj_a, G), _group_view(proj_b, G)

    def rowblk(w, col):
        return pl.BlockSpec((ns, R, w), lambda bi, ci: (bi, ci, col // w))

    const2 = lambda a: pl.BlockSpec(a.shape, lambda bi, ci: (0,) * a.ndim)
    sshape = (ns * nb, GLA_HEADS, GLA_HEAD_K, GLA_HEAD_V)
    sspec = pl.BlockSpec(sshape, lambda bi, ci: (bi, 0, 0, 0))
    kern = functools.partial(_gla_kernel, ns=ns, nb=nb, c=c, nlev=nlev)
    og, s_new = pl.pallas_call(
        kern,
        grid=(G // ns, ncl),
        in_specs=[rowblk(GLA_KDIM, A_Q), rowblk(GLA_KDIM, A_K), rowblk(GLA_WIDTH, A_V),
                  rowblk(GLA_WIDTH, A_G), rowblk(SMALL_W, B_SMALL), sspec,
                  const2(sel), const2(lvl), const2(w2cat), const2(bg), const2(gn)],
        out_specs=[pl.BlockSpec((ns, R, GLA_WIDTH), lambda bi, ci: (bi, ci, 0)), sspec],
        out_shape=[jax.ShapeDtypeStruct((G, Lg, GLA_WIDTH), BF16),
                   jax.ShapeDtypeStruct((B, GLA_HEADS, GLA_HEAD_K, GLA_HEAD_V), F32)],
        scratch_shapes=[pltpu.VMEM(sshape, F32)] if ncl > 1 else [],
        compiler_params=_cparams(("parallel", "arbitrary")),
        name="gla",
    )(pa, pa, pa, pa, pb, s0, sel, lvl, w2cat, bg, gn)
    return og.reshape(M, GLA_WIDTH), s_new


def _conv_taps(u3, prev, w, bias, width):
    n, L, C = u3.shape
    pos = lax.broadcasted_iota(jnp.int32, (1, L, 1), 1)
    acc = bias + w[width - 1:width] * u3
    for j in range(width - 1):
        d = width - 1 - j
        r = pltpu.roll(u3, d, axis=1)
        for tt in range(d):
            r = jnp.where(pos == tt, prev[:, j + tt:j + tt + 1, :], r)
        acc = acc + w[j:j + 1] * r
    return acc


def _ssd_group(gi, z_ref, xs_ref, bc_ref, sm_ref, sel_ref, cm_ref, ex_ref, cw_ref, cb_ref, dtb_ref,
               alog_ref, de_ref, nw_ref, y_ref, ht_scr, cx_scr, cbc_scr, *, nb, c):
    R = nb * c
    sq = slice(gi * nb, (gi + 1) * nb)
    cw = cw_ref[...]
    cbias = cb_ref[...]
    xraw = xs_ref[gi].reshape(nb, c, SSD_WIDTH)
    bcraw = bc_ref[gi].reshape(nb, c, SSD_BC)
    xs = _silu(_conv_taps(xraw, cx_scr[sq], cw[:, 0:SSD_WIDTH], cbias[:, 0:SSD_WIDTH], SSD_CONV))
    bca = _silu(_conv_taps(bcraw, cbc_scr[sq], cw[:, SSD_WIDTH:SSD_CONV_CH],
                           cbias[:, SSD_WIDTH:SSD_CONV_CH], SSD_CONV))
    cx_scr[sq] = xraw[:, c - (SSD_CONV - 1):c, :]
    cbc_scr[sq] = bcraw[:, c - (SSD_CONV - 1):c, :]
    xs = xs.reshape(R, SSD_WIDTH)
    bca = bca.reshape(R, SSD_BC)
    Bm = bca[:, 0:SSD_GROUPS * SSD_STATE]
    Cm = bca[:, SSD_GROUPS * SSD_STATE:SSD_BC]

    dt = _softplus(sm_ref[gi] + dtb_ref[...])
    la = dt * (-jnp.exp(alog_ref[...]))
    cs = _dot_sel(sel_ref[...], la)
    cum = cs[0:R]
    stack = jnp.concatenate([cs, dt], axis=0)
    st_e = _dot(jnp.concatenate(_split3(stack), axis=1), ex_ref[...])
    cum_e = st_e[0:R]
    lmc_e = st_e[R:2 * R]
    dt_e = st_e[2 * R:3 * R]
    ecum_e = jnp.exp(cum_e)
    xdt = xs * dt_e
    xw = (xdt * jnp.exp(lmc_e)).astype(BF16)
    xdtb = xdt.astype(BF16)
    Bb = Bm.astype(BF16)
    Cb = Cm.astype(BF16)

    pad = jnp.zeros((128 - R, 128), F32) if R < 128 else None
    cum_p = cum if pad is None else jnp.concatenate([cum, pad], axis=0)
    cumT = jnp.transpose(cum_p)
    cmask = cm_ref[...] > 0.5
    rows = lax.broadcasted_iota(jnp.int32, (R, 1), 0)
    lane = lax.broadcasted_iota(jnp.int32, (1, 128), 1)
    lo_half = lane < SSD_HEADDIM

    hpg = SSD_HEADS // SSD_GROUPS
    for hg in range(SSD_GROUPS):
        ns = slice(hg * SSD_STATE, (hg + 1) * SSD_STATE)
        gs = slice(hg * SSD_GROUP_WIDTH, (hg + 1) * SSD_GROUP_WIDTH)
        cbm = _dot_nt(Cb[:, ns], Bb[:, ns])
        yg = []
        for j in range(hpg // 2):
            ls = slice(hg * SSD_GROUP_WIDTH + j * 128, hg * SSD_GROUP_WIDTH + (j + 1) * 128)
            xpair = xdt[:, ls]
            acc = None
            for e in range(2):
                hh = DTR_OFF + hg * hpg + 2 * j + e
                diff = cum[:, hh:hh + 1] - cumT[hh:hh + 1, 0:R]
                mh = cbm * jnp.exp(jnp.where(cmask, diff, -jnp.inf))
                xh = jnp.where(lo_half if e == 0 else jnp.logical_not(lo_half), xpair, 0.0)
                t = _dot(mh.astype(BF16), xh.astype(BF16))
                acc = t if acc is None else acc + t
            yg.append(acc)
        y_intra = jnp.concatenate(yg, axis=1)
        y_inter = jnp.zeros((R, SSD_GROUP_WIDTH), F32)
        for s in range(nb):
            if nb > 1:
                rm = jnp.logical_and(rows >= s * c, rows < (s + 1) * c).astype(F32)
                cg = (Cm[:, ns] * rm).astype(BF16)
                bg_ = (Bm[:, ns] * rm).astype(BF16)
            else:
                cg = Cb[:, ns]
                bg_ = Bb[:, ns]
            hT = ht_scr[gi * nb + s, :, gs]
            y_inter = y_inter + _dot(cg, hT.astype(BF16))
            upd = _dot_tn(bg_, xw[:, gs])
            dl = ecum_e[s * c + c - 1:s * c + c, gs]
            ht_scr[gi * nb + s, :, gs] = dl * hT + upd
        y = y_intra + y_inter * ecum_e[:, gs] + de_ref[:, gs] * xs[:, gs]
        y = y * _silu(z_ref[gi, :, gs])
        y_ref[gi, :, gs] = _rms(y, nw_ref[:, gs]).astype(BF16)


def _ssd_kernel(z_ref, xs_ref, bc_ref, sm_ref, cst_ref, h0_ref, sel_ref, cm_ref, ex_ref,
                cw_ref, cb_ref, dtb_ref, alog_ref, de_ref, nw_ref,
                y_ref, hn_ref, ht_scr, cx_scr, cbc_scr, *, ns, nb, c):
    ci = pl.program_id(1)

    @pl.when(ci == 0)
    def _():
        cx_scr[...] = cst_ref[:, :, 0:SSD_WIDTH]
        cbc_scr[...] = cst_ref[:, :, SSD_WIDTH:SSD_CONV_CH]
        for s in range(ns * nb):
            ht_scr[s] = jnp.transpose(h0_ref[s].reshape(SSD_WIDTH, SSD_STATE))

    for gi in range(ns):
        _ssd_group(gi, z_ref, xs_ref, bc_ref, sm_ref, sel_ref, cm_ref, ex_ref, cw_ref, cb_ref, dtb_ref,
                   alog_ref, de_ref, nw_ref, y_ref, ht_scr, cx_scr, cbc_scr, nb=nb, c=c)

    @pl.when(ci == pl.num_programs(1) - 1)
    def _():
        for s in range(ns * nb):
            hn_ref[s] = jnp.transpose(ht_scr[s]).reshape(SSD_HEADS, SSD_HEADDIM, SSD_STATE)


def _ssd(proj_b, cst, h0, cw, cb, dtb, alog, de, nw, *, B, L, nb, c, ns):
    R = nb * c
    G = B // nb
    Lg = L * nb
    ncl = Lg // R
    sel, _, causal = _chunk_consts(nb, c)
    sel = jnp.asarray(np.tile(sel[:2 * R], (1, 3)), BF16)
    causal = jnp.asarray(causal, F32)
    exn = np.zeros((SMALL_W, SSD_WIDTH), np.float32)
    for h in range(SSD_HEADS):
        exn[DTR_OFF + h, h * SSD_HEADDIM:(h + 1) * SSD_HEADDIM] = 1.0
    ex = jnp.asarray(np.tile(exn, (3, 1)), BF16)
    M = proj_b.shape[0]
    pb = _group_view(proj_b, G)

    def rowblk(w, col):
        return pl.BlockSpec((ns, R, w), lambda bi, ci: (bi, ci, col // w))

    const2 = lambda a: pl.BlockSpec(a.shape, lambda bi, ci: (0,) * a.ndim)
    nsq = ns * nb
    hspec = pl.BlockSpec((nsq, SSD_HEADS, SSD_HEADDIM, SSD_STATE), lambda bi, ci: (bi, 0, 0, 0))
    kern = functools.partial(_ssd_kernel, ns=ns, nb=nb, c=c)
    ys, h_new = pl.pallas_call(
        kern,
        grid=(G // ns, ncl),
        in_specs=[rowblk(SSD_WIDTH, B_Z), rowblk(SSD_WIDTH, B_XS), rowblk(SSD_BC, B_BC),
                  rowblk(SMALL_W, B_SMALL),
                  pl.BlockSpec((nsq, SSD_CONV - 1, SSD_CONV_CH), lambda bi, ci: (bi, 0, 0)), hspec,
                  const2(sel), const2(causal), const2(ex),
                  const2(cw), const2(cb), const2(dtb), const2(alog), const2(de), const2(nw)],
        out_specs=[pl.BlockSpec((ns, R, SSD_WIDTH), lambda bi, ci: (bi, ci, 0)), hspec],
        out_shape=[jax.ShapeDtypeStruct((G, Lg, SSD_WIDTH), BF16),
                   jax.ShapeDtypeStruct((B, SSD_HEADS, SSD_HEADDIM, SSD_STATE), F32)],
        scratch_shapes=[pltpu.VMEM((nsq, SSD_STATE, SSD_WIDTH), F32),
                        pltpu.VMEM((nsq, SSD_CONV - 1, SSD_WIDTH), F32),
                        pltpu.VMEM((nsq, SSD_CONV - 1, SSD_BC), F32)],
        compiler_params=_cparams(("parallel", "arbitrary")),
        name="ssd",
    )(pb, pb, pb, pb, cst, h0, sel, causal, ex, cw, cb, dtb, alog, de, nw)
    return ys.reshape(M, SSD_WIDTH), h_new


def _outproj_kernel(o_ref, y_ref, x_ref, wt_ref, wb_ref, npost_ref, npre_ref, h_ref, xn_ref):
    mix = _dot(o_ref[...], wt_ref[...]) + _dot(y_ref[...], wb_ref[...])
    h = x_ref[...] + _rms(mix, npost_ref[...])
    h_ref[...] = h
    xn_ref[...] = _rms(h, npre_ref[...]).astype(BF16)


def _outproj(og, ys, x2d, w_out_bf16, npost, npre, tm):
    M, D = x2d.shape
    half = w_out_bf16.shape[0] // 2
    return pl.pallas_call(
        _outproj_kernel,
        grid=(M // tm,),
        in_specs=[pl.BlockSpec((tm, half), lambda i: (i, 0)),
                  pl.BlockSpec((tm, half), lambda i: (i, 0)),
                  pl.BlockSpec((tm, D), lambda i: (i, 0)),
                  pl.BlockSpec((half, D), lambda i: (0, 0)),
                  pl.BlockSpec((half, D), lambda i: (1, 0)),
                  pl.BlockSpec((1, D), lambda i: (0, 0)),
                  pl.BlockSpec((1, D), lambda i: (0, 0))],
        out_specs=[pl.BlockSpec((tm, D), lambda i: (i, 0)),
                   pl.BlockSpec((tm, D), lambda i: (i, 0))],
        out_shape=[jax.ShapeDtypeStruct((M, D), F32), jax.ShapeDtypeStruct((M, D), BF16)],
        compiler_params=_cparams(("parallel",)),
        name="outproj",
    )(og, ys, x2d, w_out_bf16, w_out_bf16, npost, npre)


FFN_SUB = 512
FFN_HDR = 8
FFN_RBLK = 64


def _ffn_kernel(xn_ref, h_ref, sa_ref, sb_ref, wa_ref, wb_ref, cwa_ref, cwb_ref, cba_ref, cbb_ref,
                wo_ref, npost_ref, y_ref, na_ref, nb_ref, acc_scr, ua_scr, ub_scr, act_scr, *carry,
                nseq, L, tps):
    i = pl.program_id(0)
    j = pl.program_id(1)
    tm = nseq * L
    nsub = wa_ref.shape[1] // FFN_SUB
    W1 = FFN_CONV - 1
    H = FFN_HDR
    first = i % tps == 0
    lanes = [slice(s * FFN_SUB, (s + 1) * FFN_SUB) for s in range(nsub)]

    if tps > 1:
        ca_scr, cb_scr = carry

        @pl.when(jnp.logical_not(first))
        def _():
            for s in range(nsub):
                ua_scr[s, :, 0:H, :] = ca_scr[j, s]
                ub_scr[s, :, 0:H, :] = cb_scr[j, s]

    @pl.when(first)
    def _():
        for s in range(nsub):
            ua_scr[s, :, H - W1:H, :] = sa_ref[:, :, lanes[s]]
            ub_scr[s, :, H - W1:H, :] = sb_ref[:, :, lanes[s]]

    @pl.when(j == 0)
    def _():
        acc_scr[...] = jnp.zeros_like(acc_scr)

    for s in range(nsub):
        ua_scr[s, :, H:H + L, :] = _dot(xn_ref[...], wa_ref[:, lanes[s]]).reshape(nseq, L, FFN_SUB)
        ub_scr[s, :, H:H + L, :] = _dot(xn_ref[...], wb_ref[:, lanes[s]]).reshape(nseq, L, FFN_SUB)

    if L >= FFN_RBLK:
        blocks = [(0, nseq, r, r + FFN_RBLK) for r in range(0, L, FFN_RBLK)]
    else:
        qb = FFN_RBLK // L
        blocks = [(q, q + qb, 0, L) for q in range(0, nseq, qb)]
    for s in range(nsub):
        cwa = cwa_ref[:, lanes[s]]
        cwb = cwb_ref[:, lanes[s]]
        ba = cba_ref[:, lanes[s]]
        bb = cbb_ref[:, lanes[s]]
        for (q0, q1, r0, r1) in blocks:
            def conv(u_scr, w, bias):
                out = bias + w[W1:W1 + 1] * u_scr[s, q0:q1, H + r0:H + r1, :]
                for t in range(W1):
                    d = W1 - t
                    out = out + w[t:t + 1] * u_scr[s, q0:q1, H - d + r0:H - d + r1, :]
                return out
            a = conv(ua_scr, cwa, ba)
            b = conv(ub_scr, cwb, bb)
            act = (_silu(a) * b).reshape((q1 - q0) * (r1 - r0), FFN_SUB)
            row0 = q0 * L + r0
            act_scr[s, row0:row0 + act.shape[0], :] = act.astype(BF16)
        acc_scr[...] += _dot(act_scr[s], wo_ref[lanes[s], :])

    for s in range(nsub):
        na_ref[:, :, lanes[s]] = ua_scr[s, :, H + L - W1:H + L, :]
        nb_ref[:, :, lanes[s]] = ub_scr[s, :, H + L - W1:H + L, :]
        if tps > 1:
            ca_scr[j, s] = ua_scr[s, :, L:L + H, :]
            cb_scr[j, s] = ub_scr[s, :, L:L + H, :]

    @pl.when(j == pl.num_programs(1) - 1)
    def _():
        y_ref[...] = h_ref[...] + _rms(acc_scr[...], npost_ref[...])


def _ffn(xn2, h2d, st, w_in_bf16, cw, cb, w_out_bf16, npost, *, nseq, L, tps, tf):
    M, D = h2d.shape
    F = w_out_bf16.shape[0]
    nj = F // tf
    nsub = tf // FFN_SUB
    tm = nseq * L
    W1 = FFN_CONV - 1
    kern = functools.partial(_ffn_kernel, nseq=nseq, L=L, tps=tps)
    stspec_a = pl.BlockSpec((nseq, W1, tf), lambda i, j: (i // tps, 0, j))
    stspec_b = pl.BlockSpec((nseq, W1, tf), lambda i, j: (i // tps, 0, j + nj))
    return pl.pallas_call(
        kern,
        grid=(M // tm, nj),
        in_specs=[pl.BlockSpec((tm, D), lambda i, j: (i, 0)),
                  pl.BlockSpec((tm, D), lambda i, j: (i, 0)),
                  stspec_a, stspec_b,
                  pl.BlockSpec((D, tf), lambda i, j: (0, j)),
                  pl.BlockSpec((D, tf), lambda i, j: (0, j + nj)),
                  pl.BlockSpec((FFN_CONV, tf), lambda i, j: (0, j)),
                  pl.BlockSpec((FFN_CONV, tf), lambda i, j: (0, j + nj)),
                  pl.BlockSpec((1, tf), lambda i, j: (0, j)),
                  pl.BlockSpec((1, tf), lambda i, j: (0, j + nj)),
                  pl.BlockSpec((tf, D), lambda i, j: (j, 0)),
                  pl.BlockSpec((1, D), lambda i, j: (0, 0))],
        out_specs=[pl.BlockSpec((tm, D), lambda i, j: (i, 0)),
                   pl.BlockSpec((nseq, W1, tf), lambda i, j: (i, 0, j)),
                   pl.BlockSpec((nseq, W1, tf), lambda i, j: (i, 0, j))],
        out_shape=[jax.ShapeDtypeStruct((M, D), F32),
                   jax.ShapeDtypeStruct((M // L, W1, F), F32),
                   jax.ShapeDtypeStruct((M // L, W1, F), F32)],
        scratch_shapes=[pltpu.VMEM((tm, D), F32),
                        pltpu.VMEM((nsub, nseq, FFN_HDR + L, FFN_SUB), F32),
                        pltpu.VMEM((nsub, nseq, FFN_HDR + L, FFN_SUB), F32),
                        pltpu.VMEM((nsub, tm, FFN_SUB), BF16)] + (
                            [pltpu.VMEM((nj, nsub, nseq, FFN_HDR, FFN_SUB), F32)] * 2 if tps > 1 else []),
        compiler_params=_cparams(("arbitrary", "arbitrary")),
        name="ffn",
    )(xn2, h2d, st, st, w_in_bf16, w_in_bf16, cw, cw, cb, cb, w_out_bf16, npost)


def _layer(x, s_gla, s_ssd, s_conv, s_ffn, p, *, nb, c, ns, ffn_nseq, ffn_L, ffn_tps):
    B, L, D = x.shape
    M = B * L
    x2d = x.reshape(M, D)
    proj_a = _inproj(x2d, p['n_mix_pre'], p['w_in_a'], A_W, 512)
    proj_b = _inproj(x2d, p['n_mix_pre'], p['w_in_b'], B_W, 512)
    og, g_new = _gla(proj_a, proj_b, s_gla, p['w2cat'], p['bg'], p['gn'], B=B, L=L, nb=nb, c=c, ns=ns)
    ys, h_new = _ssd(proj_b, s_conv, s_ssd, p['ssd_cw'], p['ssd_cb'], p['dtb'], p['alog'], p['de'],
                     p['ssd_nw'], B=B, L=L, nb=nb, c=c, ns=ns)
    hres, xn2 = _outproj(og, ys, x2d, p['w_out'], p['n_mix_post'], p['n_ffn_pre'], 512)
    y, fa, fb = _ffn(xn2, hres, s_ffn, p['ffn_w_in'], p['ffn_cw'], p['ffn_cb'], p['ffn_w_out'],
                     p['n_ffn_post'], nseq=ffn_nseq, L=ffn_L, tps=ffn_tps, tf=512)
    c_new = proj_b.reshape(B, L, -1)[:, L - (SSD_CONV - 1):, B_XS:B_XS + SSD_CONV_CH]
    f_new = jnp.concatenate([fa, fb], axis=-1)[ffn_tps - 1::ffn_tps]
    return y.reshape(B, L, D), g_new, h_new, c_new, f_new


def _prep_params(l, norm_mix_pre, norm_mix_post, norm_ffn_pre, norm_ffn_post, w_in, gla_w_gate2,
                 gla_b_gate, gla_norm, ssd_conv_w, ssd_conv_b, ssd_dt_bias, ssd_A_log, ssd_D, ssd_norm,
                 w_out, ffn_w_in, ffn_conv_w, ffn_conv_b, ffn_w_out):
    D = w_in.shape[1]
    sizes = (GLA_KDIM, GLA_KDIM, GLA_WIDTH, GLA_WIDTH, GLA_LOWRANK, SSD_WIDTH, SSD_CONV_CH, SSD_HEADS)
    offs = np.cumsum((0,) + sizes)
    wb16 = w_in[l].astype(BF16)
    npad = SMALL_W - GLA_LOWRANK - SSD_HEADS
    w_in_b = jnp.concatenate([wb16[:, offs[5]:offs[7]], wb16[:, offs[4]:offs[5]], wb16[:, offs[7]:offs[8]],
                              jnp.zeros((D, npad), BF16)], axis=1)
    w2p = jnp.zeros((SMALL_W, GLA_KDIM), F32).at[0:GLA_LOWRANK].set(gla_w_gate2[l])
    w2_hi = w2p.astype(BF16)
    w2_lo = (w2p - w2_hi.astype(F32)).astype(BF16)
    w2cat = jnp.concatenate([w2_hi, w2_hi, w2_lo], axis=0)
    pad_small = lambda v: jnp.zeros((1, SMALL_W), F32).at[0, DTR_OFF:DTR_OFF + SSD_HEADS].set(v)
    row = lambda v: v.reshape(1, -1)
    return dict(
        n_mix_pre=row(norm_mix_pre[l]), n_mix_post=row(norm_mix_post[l]),
        n_ffn_pre=row(norm_ffn_pre[l]), n_ffn_post=row(norm_ffn_post[l]),
        w_in_a=wb16, w_in_b=w_in_b, w2cat=w2cat, bg=row(gla_b_gate[l]), gn=row(gla_norm[l]),
        ssd_cw=ssd_conv_w[l], ssd_cb=row(ssd_conv_b[l]),
        dtb=pad_small(ssd_dt_bias[l]), alog=pad_small(ssd_A_log[l]),
        de=row(jnp.repeat(ssd_D[l], SSD_HEADDIM)), ssd_nw=row(ssd_norm[l]),
        w_out=w_out[l].astype(BF16), ffn_w_in=ffn_w_in[l].astype(BF16),
        ffn_cw=ffn_conv_w[l], ffn_cb=row(ffn_conv_b[l]), ffn_w_out=ffn_w_out[l].astype(BF16))


def kernel(x_prompt, x_sample, state_gla, state_ssd, state_ssd_conv, state_ffn_conv, norm_mix_pre,
           norm_mix_post, norm_ffn_pre, norm_ffn_post, w_in, gla_w_gate2, gla_b_gate, gla_norm,
           ssd_conv_w, ssd_conv_b, ssd_dt_bias, ssd_A_log, ssd_D, ssd_norm, w_out, ffn_w_in,
           ffn_conv_w, ffn_conv_b, ffn_w_out):
    depth = w_in.shape[0]
    xp, xs = x_prompt, x_sample
    Bp, Lp, D = xp.shape
    Bs, Ls, _ = xs.shape
    F2 = ffn_w_in.shape[2]
    outs = [[] for _ in range(8)]
    for l in range(depth):
        p = _prep_params(l, norm_mix_pre, norm_mix_post, norm_ffn_pre, norm_ffn_post, w_in,
                         gla_w_gate2, gla_b_gate, gla_norm, ssd_conv_w, ssd_conv_b, ssd_dt_bias,
                         ssd_A_log, ssd_D, ssd_norm, w_out, ffn_w_in, ffn_conv_w, ffn_conv_b, ffn_w_out)
        cp = CHUNK
        xp, g1, h1, c1, f1 = _layer(
            xp,
            jnp.zeros((Bp, GLA_HEADS, GLA_HEAD_K, GLA_HEAD_V), F32),
            jnp.zeros((Bp, SSD_HEADS, SSD_HEADDIM, SSD_STATE), F32),
            jnp.zeros((Bp, SSD_CONV - 1, SSD_CONV_CH), F32),
            jnp.zeros((Bp, FFN_CONV - 1, F2), F32),
            p, nb=1, c=cp, ns=Bp, ffn_nseq=1, ffn_L=512, ffn_tps=Lp // 512)
        nbs = CHUNK // Ls
        xs, g2, h2, c2, f2 = _layer(
            xs, state_gla[l], state_ssd[l], state_ssd_conv[l], state_ffn_conv[l],
            p, nb=nbs, c=Ls, ns=2, ffn_nseq=512 // Ls, ffn_L=Ls, ffn_tps=1)
        for lst, val in zip(outs, (g1, h1, c1, f1, g2, h2, c2, f2)):
            lst.append(val)
    return (xp, xs) + tuple(jnp.stack(o) for o in outs)
```

```python
import functools

import numpy as np
import jax
import jax.numpy as jnp
from jax import lax
from jax.experimental import pallas as pl
from jax.experimental.pallas import tpu as pltpu

F32 = jnp.float32
BF16 = jnp.bfloat16
EPS = 1e-6

GLA_HEADS = 4
GLA_HEAD_K = 128
GLA_HEAD_V = 256
GLA_KDIM = GLA_HEADS * GLA_HEAD_K
GLA_WIDTH = GLA_HEADS * GLA_HEAD_V
GLA_LOWRANK = 16
GLA_GATE_NORMALIZER = 16.0
SSD_HEADS = 16
SSD_HEADDIM = 64
SSD_STATE = 128
SSD_GROUPS = 2
SSD_WIDTH = SSD_HEADS * SSD_HEADDIM
SSD_GROUP_WIDTH = SSD_WIDTH // SSD_GROUPS
SSD_CONV = 4
SSD_BC = 2 * SSD_GROUPS * SSD_STATE
SSD_CONV_CH = SSD_WIDTH + SSD_BC
FFN_CONV = 3
CHUNK = 64

A_Q = 0
A_K = GLA_KDIM
A_V = 2 * GLA_KDIM
A_G = A_V + GLA_WIDTH
A_W = A_G + GLA_WIDTH
B_Z = 0
B_XS = SSD_WIDTH
B_BC = B_XS + SSD_WIDTH
B_SMALL = B_BC + SSD_BC
SMALL_W = 128
DTR_OFF = GLA_LOWRANK
B_W = B_SMALL + SMALL_W

VMEM_LIMIT = 56 * 1024 * 1024


def _cparams(sem):
    return pltpu.CompilerParams(dimension_semantics=sem, vmem_limit_bytes=VMEM_LIMIT)


def _split3(x):
    hi = x.astype(BF16)
    r = x - hi.astype(F32)
    mid = r.astype(BF16)
    lo = (r - mid.astype(F32)).astype(BF16)
    return hi, mid, lo


def _dot(a, b):
    return jnp.dot(a, b, preferred_element_type=F32)


def _dot_nt(a, b):
    return lax.dot_general(a, b, (((1,), (1,)), ((), ())), preferred_element_type=F32)


def _dot_tn(a, b):
    return lax.dot_general(a, b, (((0,), (0,)), ((), ())), preferred_element_type=F32)


def _dot_sel(sel3_bf16, x_f32):
    return _dot(sel3_bf16, jnp.concatenate(_split3(x_f32), axis=0))


NEG_LOG2E = -1.4426950408889634


def _silu(x):
    return x / (1.0 + jnp.exp2(x * NEG_LOG2E))


def _softplus(x):
    return jnp.maximum(x, 0.0) + jnp.log1p(jnp.exp(-jnp.abs(x)))


def _rms(x, w):
    return x * lax.rsqrt(jnp.mean(x * x, axis=-1, keepdims=True) + EPS) * w


def _chunk_consts(nb, c):
    R = nb * c
    idx = np.arange(R)
    seq, pos = idx // c, idx % c
    same = seq[:, None] == seq[None, :]
    t, u = pos[:, None], pos[None, :]
    blocks = [same & (u <= t), same & (u > t)]
    masks = [np.eye(R, dtype=bool)]
    m = c // 2
    while m >= 1:
        blk = pos // (2 * m)
        rho = blk * 2 * m + m - 1
        upper = pos > rho
        a_up = upper[:, None] & (u > rho[:, None]) & (u <= t)
        a_lo = (~upper)[:, None] & (u > t) & (u <= rho[:, None])
        blocks.append(same & (a_up | a_lo))
        masks.append(same & upper[:, None] & (~upper)[None, :] & (blk[:, None] == blk[None, :]))
        m //= 2
    sel = np.concatenate(blocks, 0).astype(np.float32)
    lvl = np.stack(masks).astype(np.float32)
    causal = (same & (u <= t)).astype(np.float32)
    return sel, lvl, causal


def _inproj_kernel(x_ref, nw_ref, w_ref, o_ref):
    xn = _rms(x_ref[...], nw_ref[...]).astype(BF16)
    o_ref[...] = _dot(xn, w_ref[...])


def _inproj(x2d, nw, w_bf16, n_out, tm):
    M, D = x2d.shape
    return pl.pallas_call(
        _inproj_kernel,
        grid=(M // tm,),
        in_specs=[pl.BlockSpec((tm, D), lambda i: (i, 0)),
                  pl.BlockSpec((1, D), lambda i: (0, 0)),
                  pl.BlockSpec((D, n_out), lambda i: (0, 0))],
        out_specs=pl.BlockSpec((tm, n_out), lambda i: (i, 0)),
        out_shape=jax.ShapeDtypeStruct((M, n_out), F32),
        compiler_params=_cparams(("parallel",)),
        name="inproj",
    )(x2d, nw, w_bf16)


def _gla_group(gi, q_ref, k_ref, v_ref, g_ref, sm_ref, s_in, s_out, sel_ref, lvl_ref, w2_ref, bg_ref,
               gn_ref, o_ref, *, nb, c, nlev):
    R = nb * c
    sm = sm_ref[gi]
    sm_hi = sm.astype(BF16)
    sm_lo = (sm - sm_hi.astype(F32)).astype(BF16)
    zg = _dot(jnp.concatenate([sm_hi, sm_lo, sm_hi], axis=1), w2_ref[...]) + bg_ref[...]
    lg = -_softplus(-zg) * (1.0 / GLA_GATE_NORMALIZER)
    E = _dot_sel(sel_ref[...], lg)
    b = E[0:R]
    eb = jnp.exp(b)
    q = q_ref[gi] * (GLA_HEAD_K ** -0.5)
    k = k_ref[gi]
    qe = q * eb
    ke = k * jnp.exp(E[R:2 * R])
    ql, kl = [q.astype(BF16)], [k.astype(BF16)]
    for l in range(nlev):
        p = jnp.exp(E[(2 + l) * R:(3 + l) * R])
        ql.append((q * p).astype(BF16))
        kl.append((k * p).astype(BF16))
    vb = v_ref[gi].astype(BF16)
    g = g_ref[gi]
    gn = gn_ref[...]
    rows = lax.broadcasted_iota(jnp.int32, (R, 1), 0)

    for h in range(GLA_HEADS):
        ks = slice(h * GLA_HEAD_K, (h + 1) * GLA_HEAD_K)
        vs = slice(h * GLA_HEAD_V, (h + 1) * GLA_HEAD_V)
        att = jnp.zeros((R, R), F32)
        for l in range(nlev + 1):
            att = att + _dot_nt(ql[l][:, ks], kl[l][:, ks]) * lvl_ref[l]
        o = _dot(att.astype(BF16), vb[:, vs])
        for s in range(nb):
            if nb > 1:
                rm = jnp.logical_and(rows >= s * c, rows < (s + 1) * c).astype(F32)
                qs = (qe[:, ks] * rm).astype(BF16)
                kd = (ke[:, ks] * rm).astype(BF16)
            else:
                qs = qe[:, ks].astype(BF16)
                kd = ke[:, ks].astype(BF16)
            S = s_in[gi * nb + s, h]
            o = o + _dot(qs, S.astype(BF16))
            upd = _dot_tn(kd, vb[:, vs])
            d = eb[s * c + c - 1:s * c + c, ks]
            dcol = jnp.transpose(jnp.broadcast_to(d, (GLA_HEAD_K, GLA_HEAD_K)))
            s_out[gi * nb + s, h] = jnp.concatenate([dcol, dcol], axis=1) * S + upd
        gh = g[:, vs]
        o_ref[gi, :, vs] = (_rms(o, gn) * _silu(gh)).astype(BF16)


def _gla_kernel(q_ref, k_ref, v_ref, g_ref, sm_ref, s0_ref, sel_ref, lvl_ref, w2_ref, bg_ref, gn_ref,
                o_ref, sn_ref, *scratch, ns, nb, c, nlev):
    args = (q_ref, k_ref, v_ref, g_ref, sm_ref)
    consts = (sel_ref, lvl_ref, w2_ref, bg_ref, gn_ref, o_ref)
    kw = dict(nb=nb, c=c, nlev=nlev)
    if not scratch:
        for gi in range(ns):
            _gla_group(gi, *args, s0_ref, sn_ref, *consts, **kw)
        return
    s_scr, = scratch
    ci = pl.program_id(1)

    @pl.when(ci == 0)
    def _():
        s_scr[...] = s0_ref[...]

    for gi in range(ns):
        _gla_group(gi, *args, s_scr, s_scr, *consts, **kw)

    @pl.when(ci == pl.num_programs(1) - 1)
    def _():
        sn_ref[...] = s_scr[...]


def _group_view(a, G):
    return a.reshape(G, a.shape[0] // G, a.shape[1])


def _gla(proj_a, proj_b, s0, w2cat, bg, gn, *, B, L, nb, c, ns):
    R = nb * c
    G = B // nb
    Lg = L * nb
    ncl = Lg // R
    nlev = int(np.log2(c))
    sel, lvl, _ = _chunk_consts(nb, c)
    sel = jnp.asarray(np.tile(sel, (1, 3)), BF16)
    lvl = jnp.asarray(lvl, F32)
    M = proj_a.shape[0]
    pa, pb = _group_view(proj_a, G), _group_view(proj_b, G)

    def rowblk(w, col):
        return pl.BlockSpec((ns, R, w), lambda bi, ci: (bi, ci, col // w))

    const2 = lambda a: pl.BlockSpec(a.shape, lambda bi, ci: (0,) * a.ndim)
    sshape = (ns * nb, GLA_HEADS, GLA_HEAD_K, GLA_HEAD_V)
    sspec = pl.BlockSpec(sshape, lambda bi, ci: (bi, 0, 0, 0))
    kern = functools.partial(_gla_kernel, ns=ns, nb=nb, c=c, nlev=nlev)
    og, s_new = pl.pallas_call(
        kern,
        grid=(G // ns, ncl),
        in_specs=[rowblk(GLA_KDIM, A_Q), rowblk(GLA_KDIM, A_K), rowblk(GLA_WIDTH, A_V),
                  rowblk(GLA_WIDTH, A_G), rowblk(SMALL_W, B_SMALL), sspec,
                  const2(sel), const2(lvl), const2(w2cat), const2(bg), const2(gn)],
        out_specs=[pl.BlockSpec((ns, R, GLA_WIDTH), lambda bi, ci: (bi, ci, 0)), sspec],
        out_shape=[jax.ShapeDtypeStruct((G, Lg, GLA_WIDTH), BF16),
                   jax.ShapeDtypeStruct((B, GLA_HEADS, GLA_HEAD_K, GLA_HEAD_V), F32)],
        scratch_shapes=[pltpu.VMEM(sshape, F32)] if ncl > 1 else [],
        compiler_params=_cparams(("parallel", "arbitrary")),
        name="gla",
    )(pa, pa, pa, pa, pb, s0, sel, lvl, w2cat, bg, gn)
    return og.reshape(M, GLA_WIDTH), s_new


def _conv_taps(u3, prev, w, bias, width):
    n, L, C = u3.shape
    pos = lax.broadcasted_iota(jnp.int32, (1, L, 1), 1)
    acc = bias + w[width - 1:width] * u3
    for j in range(width - 1):
        d = width - 1 - j
        r = pltpu.roll(u3, d, axis=1)
        for tt in range(d):
            r = jnp.where(pos == tt, prev[:, j + tt:j + tt + 1, :], r)
        acc = acc + w[j:j + 1] * r
    return acc


def _ssd_group(gi, z_ref, xs_ref, bc_ref, sm_ref, sel_ref, cm_ref, ex_ref, cw_ref, cb_ref, dtb_ref,
               alog_ref, de_ref, nw_ref, y_ref, ht_scr, cx_scr, cbc_scr, *, nb, c):
    R = nb * c
    sq = slice(gi * nb, (gi + 1) * nb)
    cw = cw_ref[...]
    cbias = cb_ref[...]
    xraw = xs_ref[gi].reshape(nb, c, SSD_WIDTH)
    bcraw = bc_ref[gi].reshape(nb, c, SSD_BC)
    xs = _silu(_conv_taps(xraw, cx_scr[sq], cw[:, 0:SSD_WIDTH], cbias[:, 0:SSD_WIDTH], SSD_CONV))
    bca = _silu(_conv_taps(bcraw, cbc_scr[sq], cw[:, SSD_WIDTH:SSD_CONV_CH],
                           cbias[:, SSD_WIDTH:SSD_CONV_CH], SSD_CONV))
    cx_scr[sq] = xraw[:, c - (SSD_CONV - 1):c, :]
    cbc_scr[sq] = bcraw[:, c - (SSD_CONV - 1):c, :]
    xs = xs.reshape(R, SSD_WIDTH)
    bca = bca.reshape(R, SSD_BC)
    Bm = bca[:, 0:SSD_GROUPS * SSD_STATE]
    Cm = bca[:, SSD_GROUPS * SSD_STATE:SSD_BC]

    dt = _softplus(sm_ref[gi] + dtb_ref[...])
    la = dt * (-jnp.exp(alog_ref[...]))
    cs = _dot_sel(sel_ref[...], la)
    cum = cs[0:R]
    stack = jnp.concatenate([cs, dt], axis=0)
    st_e = _dot(jnp.concatenate(_split3(stack), axis=1), ex_ref[...])
    cum_e = st_e[0:R]
    lmc_e = st_e[R:2 * R]
    dt_e = st_e[2 * R:3 * R]
    ecum_e = jnp.exp(cum_e)
    xdt = xs * dt_e
    xw = (xdt * jnp.exp(lmc_e)).astype(BF16)
    xdtb = xdt.astype(BF16)
    Bb = Bm.astype(BF16)
    Cb = Cm.astype(BF16)

    cumT2 = jnp.transpose(jnp.concatenate([cum, cum], axis=0))
    cmask2 = cm_ref[...] > 0.5
    rows = lax.broadcasted_iota(jnp.int32, (R, 1), 0)
    lane = lax.broadcasted_iota(jnp.int32, (1, 2 * R), 1)
    first = lane < R
    lo_half = lax.broadcasted_iota(jnp.int32, (1, 128), 1) < SSD_HEADDIM

    hpg = SSD_HEADS // SSD_GROUPS
    for hg in range(SSD_GROUPS):
        ns = slice(hg * SSD_STATE, (hg + 1) * SSD_STATE)
        gs = slice(hg * SSD_GROUP_WIDTH, (hg + 1) * SSD_GROUP_WIDTH)
        bgrp = Bb[:, ns]
        cb2 = _dot_nt(Cb[:, ns], jnp.concatenate([bgrp, bgrp], axis=0))
        yg = []
        for j in range(hpg // 2):
            ls = slice(hg * SSD_GROUP_WIDTH + j * 128, hg * SSD_GROUP_WIDTH + (j + 1) * 128)
            h0 = DTR_OFF + hg * hpg + 2 * j
            col = jnp.where(first, cum[:, h0:h0 + 1], cum[:, h0 + 1:h0 + 2])
            row = jnp.where(first, cumT2[h0:h0 + 1, :], cumT2[h0 + 1:h0 + 2, :])
            m2 = cb2 * jnp.exp(jnp.where(cmask2, col - row, -jnp.inf))
            xpair = xdt[:, ls]
            x2 = jnp.concatenate([jnp.where(lo_half, xpair, 0.0).astype(BF16),
                                  jnp.where(lo_half, 0.0, xpair).astype(BF16)], axis=0)
            yg.append(_dot(m2.astype(BF16), x2))
        y_intra = jnp.concatenate(yg, axis=1)
        y_inter = jnp.zeros((R, SSD_GROUP_WIDTH), F32)
        for s in range(nb):
            if nb > 1:
                rm = jnp.logical_and(rows >= s * c, rows < (s + 1) * c).astype(F32)
                cg = (Cm[:, ns] * rm).astype(BF16)
                bg_ = (Bm[:, ns] * rm).astype(BF16)
            else:
                cg = Cb[:, ns]
                bg_ = Bb[:, ns]
            hT = ht_scr[gi * nb + s, :, gs]
            y_inter = y_inter + _dot(cg, hT.astype(BF16))
            upd = _dot_tn(bg_, xw[:, gs])
            dl = ecum_e[s * c + c - 1:s * c + c, gs]
            ht_scr[gi * nb + s, :, gs] = dl * hT + upd
        y = y_intra + y_inter * ecum_e[:, gs] + de_ref[:, gs] * xs[:, gs]
        y = y * _silu(z_ref[gi, :, gs])
        y_ref[gi, :, gs] = _rms(y, nw_ref[:, gs]).astype(BF16)


def _ssd_kernel(z_ref, xs_ref, bc_ref, sm_ref, cst_ref, h0_ref, sel_ref, cm_ref, ex_ref,
                cw_ref, cb_ref, dtb_ref, alog_ref, de_ref, nw_ref,
                y_ref, hn_ref, ht_scr, cx_scr, cbc_scr, *, ns, nb, c):
    ci = pl.program_id(1)

    @pl.when(ci == 0)
    def _():
        cx_scr[...] = cst_ref[:, :, 0:SSD_WIDTH]
        cbc_scr[...] = cst_ref[:, :, SSD_WIDTH:SSD_CONV_CH]
        for s in range(ns * nb):
            ht_scr[s] = jnp.transpose(h0_ref[s].reshape(SSD_WIDTH, SSD_STATE))

    for gi in range(ns):
        _ssd_group(gi, z_ref, xs_ref, bc_ref, sm_ref, sel_ref, cm_ref, ex_ref, cw_ref, cb_ref, dtb_ref,
                   alog_ref, de_ref, nw_ref, y_ref, ht_scr, cx_scr, cbc_scr, nb=nb, c=c)

    @pl.when(ci == pl.num_programs(1) - 1)
    def _():
        for s in range(ns * nb):
            hn_ref[s] = jnp.transpose(ht_scr[s]).reshape(SSD_HEADS, SSD_HEADDIM, SSD_STATE)


def _ssd(proj_b, cst, h0, cw, cb, dtb, alog, de, nw, *, B, L, nb, c, ns):
    R = nb * c
    G = B // nb
    Lg = L * nb
    ncl = Lg // R
    sel, _, causal = _chunk_consts(nb, c)
    assert 2 * R == 128, "two heads share one 128-lane tile"
    sel = jnp.asarray(np.tile(sel[:2 * R], (1, 3)), BF16)
    causal = jnp.asarray(np.tile(causal, (1, 2)), F32)
    exn = np.zeros((SMALL_W, SSD_WIDTH), np.float32)
    for h in range(SSD_HEADS):
        exn[DTR_OFF + h, h * SSD_HEADDIM:(h + 1) * SSD_HEADDIM] = 1.0
    ex = jnp.asarray(np.tile(exn, (3, 1)), BF16)
    M = proj_b.shape[0]
    pb = _group_view(proj_b, G)

    def rowblk(w, col):
        return pl.BlockSpec((ns, R, w), lambda bi, ci: (bi, ci, col // w))

    const2 = lambda a: pl.BlockSpec(a.shape, lambda bi, ci: (0,) * a.ndim)
    nsq = ns * nb
    hspec = pl.BlockSpec((nsq, SSD_HEADS, SSD_HEADDIM, SSD_STATE), lambda bi, ci: (bi, 0, 0, 0))
    kern = functools.partial(_ssd_kernel, ns=ns, nb=nb, c=c)
    ys, h_new = pl.pallas_call(
        kern,
        grid=(G // ns, ncl),
        in_specs=[rowblk(SSD_WIDTH, B_Z), rowblk(SSD_WIDTH, B_XS), rowblk(SSD_BC, B_BC),
                  rowblk(SMALL_W, B_SMALL),
                  pl.BlockSpec((nsq, SSD_CONV - 1, SSD_CONV_CH), lambda bi, ci: (bi, 0, 0)), hspec,
                  const2(sel), const2(causal), const2(ex),
                  const2(cw), const2(cb), const2(dtb), const2(alog), const2(de), const2(nw)],
        out_specs=[pl.BlockSpec((ns, R, SSD_WIDTH), lambda bi, ci: (bi, ci, 0)), hspec],
        out_shape=[jax.ShapeDtypeStruct((G, Lg, SSD_WIDTH), BF16),
                   jax.ShapeDtypeStruct((B, SSD_HEADS, SSD_HEADDIM, SSD_STATE), F32)],
        scratch_shapes=[pltpu.VMEM((nsq, SSD_STATE, SSD_WIDTH), F32),
                        pltpu.VMEM((nsq, SSD_CONV - 1, SSD_WIDTH), F32),
                        pltpu.VMEM((nsq, SSD_CONV - 1, SSD_BC), F32)],
        compiler_params=_cparams(("parallel", "arbitrary")),
        name="ssd",
    )(pb, pb, pb, pb, cst, h0, sel, causal, ex, cw, cb, dtb, alog, de, nw)
    return ys.reshape(M, SSD_WIDTH), h_new


def _outproj_kernel(o_ref, y_ref, x_ref, wt_ref, wb_ref, npost_ref, npre_ref, h_ref, xn_ref):
    mix = _dot(o_ref[...], wt_ref[...]) + _dot(y_ref[...], wb_ref[...])
    h = x_ref[...] + _rms(mix, npost_ref[...])
    h_ref[...] = h
    xn_ref[...] = _rms(h, npre_ref[...]).astype(BF16)


def _outproj(og, ys, x2d, w_out_bf16, npost, npre, tm):
    M, D = x2d.shape
    half = w_out_bf16.shape[0] // 2
    return pl.pallas_call(
        _outproj_kernel,
        grid=(M // tm,),
        in_specs=[pl.BlockSpec((tm, half), lambda i: (i, 0)),
                  pl.BlockSpec((tm, half), lambda i: (i, 0)),
                  pl.BlockSpec((tm, D), lambda i: (i, 0)),
                  pl.BlockSpec((half, D), lambda i: (0, 0)),
                  pl.BlockSpec((half, D), lambda i: (1, 0)),
                  pl.BlockSpec((1, D), lambda i: (0, 0)),
                  pl.BlockSpec((1, D), lambda i: (0, 0))],
        out_specs=[pl.BlockSpec((tm, D), lambda i: (i, 0)),
                   pl.BlockSpec((tm, D), lambda i: (i, 0))],
        out_shape=[jax.ShapeDtypeStruct((M, D), F32), jax.ShapeDtypeStruct((M, D), BF16)],
        compiler_params=_cparams(("parallel",)),
        name="outproj",
    )(og, ys, x2d, w_out_bf16, w_out_bf16, npost, npre)


SUBLANES = 8
FFN_HDR = SUBLANES
FFN_RBLK = 64
FFN_SPLIT = 2


def _ffn_kernel(xn_ref, h_ref, sa_ref, sb_ref, wa_ref, wb_ref, cwa_ref, cwb_ref, cba_ref, cbb_ref,
                wo_ref, npost_ref, y_ref, na_ref, nb_ref, acc_scr, ua_scr, ub_scr, act_scr, *carry,
                nseq, L, tps):
    i = pl.program_id(0)
    j = pl.program_id(1)
    tf = wa_ref.shape[1]
    W1 = FFN_CONV - 1
    H = FFN_HDR
    first = i % tps == 0

    if tps > 1:
        ca_scr, cb_scr = carry

        @pl.when(jnp.logical_not(first))
        def _():
            ua_scr[:, 0:H, :] = ca_scr[j]
            ub_scr[:, 0:H, :] = cb_scr[j]

    @pl.when(first)
    def _():
        ua_scr[:, H - W1:H, :] = sa_ref[...]
        ub_scr[:, H - W1:H, :] = sb_ref[...]

    @pl.when(j == 0)
    def _():
        acc_scr[...] = jnp.zeros_like(acc_scr)

    if nseq == 1:
        parts = [(0, 1, k * L // FFN_SPLIT, (k + 1) * L // FFN_SPLIT) for k in range(FFN_SPLIT)]
    else:
        parts = [(k * nseq // FFN_SPLIT, (k + 1) * nseq // FFN_SPLIT, 0, L) for k in range(FFN_SPLIT)]
    tile_rows = lambda q0, q1, r0, r1: slice(q0 * L + r0, (q1 - 1) * L + r1)

    for (q0, q1, r0, r1) in parts:
        rs = tile_rows(q0, q1, r0, r1)
        ua_scr[q0:q1, H + r0:H + r1, :] = _dot(xn_ref[rs, :], wa_ref[...]).reshape(q1 - q0, r1 - r0, tf)
        ub_scr[q0:q1, H + r0:H + r1, :] = _dot(xn_ref[rs, :], wb_ref[...]).reshape(q1 - q0, r1 - r0, tf)

    spread = lambda row: jnp.broadcast_to(row, (SUBLANES, tf))
    wa8 = [spread(cwa_ref[t:t + 1, :]) for t in range(FFN_CONV)]
    wb8 = [spread(cwb_ref[t:t + 1, :]) for t in range(FFN_CONV)]
    ba8 = spread(cba_ref[...])
    bb8 = spread(cbb_ref[...])
    for (q0, q1, r0, r1) in parts:
        if nseq == 1:
            blocks = [(0, 1, r, r + FFN_RBLK) for r in range(r0, r1, FFN_RBLK)]
        else:
            qb = FFN_RBLK // L
            blocks = [(q, q + qb, 0, L) for q in range(q0, q1, qb)]
        for (bq0, bq1, br0, br1) in blocks:
            def conv(u_scr, w8, b8):
                tap = lambda d: u_scr[bq0:bq1, H - d + br0:H - d + br1, :].reshape(-1, SUBLANES, tf)
                out = b8 + w8[W1] * tap(0)
                for t in range(W1):
                    out = out + w8[t] * tap(W1 - t)
                return out
            a = conv(ua_scr, wa8, ba8)
            b = conv(ub_scr, wb8, bb8)
            row0 = bq0 * L + br0
            act_scr[row0:row0 + FFN_RBLK, :] = (_silu(a) * b).reshape(FFN_RBLK, tf).astype(BF16)
        rs = tile_rows(q0, q1, r0, r1)
        acc_scr[rs, :] += _dot(act_scr[rs, :], wo_ref[...])

    na_ref[...] = ua_scr[:, H + L - W1:H + L, :]
    nb_ref[...] = ub_scr[:, H + L - W1:H + L, :]
    if tps > 1:
        ca_scr[j] = ua_scr[:, L:L + H, :]
        cb_scr[j] = ub_scr[:, L:L + H, :]

    @pl.when(j == pl.num_programs(1) - 1)
    def _():
        y_ref[...] = h_ref[...] + _rms(acc_scr[...], npost_ref[...])


def _ffn(xn2, h2d, st, w_in_bf16, cw, cb, w_out_bf16, npost, *, nseq, L, tps, tf):
    M, D = h2d.shape
    F = w_out_bf16.shape[0]
    nj = F // tf
    tm = nseq * L
    W1 = FFN_CONV - 1
    assert (L if nseq == 1 else nseq * L) % (FFN_SPLIT * FFN_RBLK) == 0
    kern = functools.partial(_ffn_kernel, nseq=nseq, L=L, tps=tps)
    stspec_a = pl.BlockSpec((nseq, W1, tf), lambda i, j: (i // tps, 0, j))
    stspec_b = pl.BlockSpec((nseq, W1, tf), lambda i, j: (i // tps, 0, j + nj))
    return pl.pallas_call(
        kern,
        grid=(M // tm, nj),
        in_specs=[pl.BlockSpec((tm, D), lambda i, j: (i, 0)),
                  pl.BlockSpec((tm, D), lambda i, j: (i, 0)),
                  stspec_a, stspec_b,
                  pl.BlockSpec((D, tf), lambda i, j: (0, j)),
                  pl.BlockSpec((D, tf), lambda i, j: (0, j + nj)),
                  pl.BlockSpec((FFN_CONV, tf), lambda i, j: (0, j)),
                  pl.BlockSpec((FFN_CONV, tf), lambda i, j: (0, j + nj)),
                  pl.BlockSpec((1, tf), lambda i, j: (0, j)),
                  pl.BlockSpec((1, tf), lambda i, j: (0, j + nj)),
                  pl.BlockSpec((tf, D), lambda i, j: (j, 0)),
                  pl.BlockSpec((1, D), lambda i, j: (0, 0))],
        out_specs=[pl.BlockSpec((tm, D), lambda i, j: (i, 0)),
                   pl.BlockSpec((nseq, W1, tf), lambda i, j: (i, 0, j)),
                   pl.BlockSpec((nseq, W1, tf), lambda i, j: (i, 0, j))],
        out_shape=[jax.ShapeDtypeStruct((M, D), F32),
                   jax.ShapeDtypeStruct((M // L, W1, F), F32),
                   jax.ShapeDtypeStruct((M // L, W1, F), F32)],
        scratch_shapes=[pltpu.VMEM((tm, D), F32),
                        pltpu.VMEM((nseq, FFN_HDR + L, tf), F32),
                        pltpu.VMEM((nseq, FFN_HDR + L, tf), F32),
                        pltpu.VMEM((tm, tf), BF16)] + (
                            [pltpu.VMEM((nj, nseq, FFN_HDR, tf), F32)] * 2 if tps > 1 else []),
        compiler_params=_cparams(("arbitrary", "arbitrary")),
        name="ffn",
    )(xn2, h2d, st, st, w_in_bf16, w_in_bf16, cw, cw, cb, cb, w_out_bf16, npost)


def _layer(x, s_gla, s_ssd, s_conv, s_ffn, p, *, nb, c, ns, ffn_nseq, ffn_L, ffn_tps):
    B, L, D = x.shape
    M = B * L
    x2d = x.reshape(M, D)
    proj_a = _inproj(x2d, p['n_mix_pre'], p['w_in_a'], A_W, 512)
    proj_b = _inproj(x2d, p['n_mix_pre'], p['w_in_b'], B_W, 512)
    og, g_new = _gla(proj_a, proj_b, s_gla, p['w2cat'], p['bg'], p['gn'], B=B, L=L, nb=nb, c=c, ns=ns)
    ys, h_new = _ssd(proj_b, s_conv, s_ssd, p['ssd_cw'], p['ssd_cb'], p['dtb'], p['alog'], p['de'],
                     p['ssd_nw'], B=B, L=L, nb=nb, c=c, ns=ns)
    hres, xn2 = _outproj(og, ys, x2d, p['w_out'], p['n_mix_post'], p['n_ffn_pre'], 512)
    y, fa, fb = _ffn(xn2, hres, s_ffn, p['ffn_w_in'], p['ffn_cw'], p['ffn_cb'], p['ffn_w_out'],
                     p['n_ffn_post'], nseq=ffn_nseq, L=ffn_L, tps=ffn_tps, tf=512)
    c_new = proj_b.reshape(B, L, -1)[:, L - (SSD_CONV - 1):, B_XS:B_XS + SSD_CONV_CH]
    f_new = jnp.concatenate([fa, fb], axis=-1)[ffn_tps - 1::ffn_tps]
    return y.reshape(B, L, D), g_new, h_new, c_new, f_new


def _prep_params(l, norm_mix_pre, norm_mix_post, norm_ffn_pre, norm_ffn_post, w_in, gla_w_gate2,
                 gla_b_gate, gla_norm, ssd_conv_w, ssd_conv_b, ssd_dt_bias, ssd_A_log, ssd_D, ssd_norm,
                 w_out, ffn_w_in, ffn_conv_w, ffn_conv_b, ffn_w_out):
    D = w_in.shape[1]
    sizes = (GLA_KDIM, GLA_KDIM, GLA_WIDTH, GLA_WIDTH, GLA_LOWRANK, SSD_WIDTH, SSD_CONV_CH, SSD_HEADS)
    offs = np.cumsum((0,) + sizes)
    wb16 = w_in[l].astype(BF16)
    npad = SMALL_W - GLA_LOWRANK - SSD_HEADS
    w_in_b = jnp.concatenate([wb16[:, offs[5]:offs[7]], wb16[:, offs[4]:offs[5]], wb16[:, offs[7]:offs[8]],
                              jnp.zeros((D, npad), BF16)], axis=1)
    w2p = jnp.zeros((SMALL_W, GLA_KDIM), F32).at[0:GLA_LOWRANK].set(gla_w_gate2[l])
    w2_hi = w2p.astype(BF16)
    w2_lo = (w2p - w2_hi.astype(F32)).astype(BF16)
    w2cat = jnp.concatenate([w2_hi, w2_hi, w2_lo], axis=0)
    pad_small = lambda v: jnp.zeros((1, SMALL_W), F32).at[0, DTR_OFF:DTR_OFF + SSD_HEADS].set(v)
    row = lambda v: v.reshape(1, -1)
    return dict(
        n_mix_pre=row(norm_mix_pre[l]), n_mix_post=row(norm_mix_post[l]),
        n_ffn_pre=row(norm_ffn_pre[l]), n_ffn_post=row(norm_ffn_post[l]),
        w_in_a=wb16, w_in_b=w_in_b, w2cat=w2cat, bg=row(gla_b_gate[l]), gn=row(gla_norm[l]),
        ssd_cw=ssd_conv_w[l], ssd_cb=row(ssd_conv_b[l]),
        dtb=pad_small(ssd_dt_bias[l]), alog=pad_small(ssd_A_log[l]),
        de=row(jnp.repeat(ssd_D[l], SSD_HEADDIM)), ssd_nw=row(ssd_norm[l]),
        w_out=w_out[l].astype(BF16), ffn_w_in=ffn_w_in[l].astype(BF16),
        ffn_cw=ffn_conv_w[l], ffn_cb=row(ffn_conv_b[l]), ffn_w_out=ffn_w_out[l].astype(BF16))


def kernel(x_prompt, x_sample, state_gla, state_ssd, state_ssd_conv, state_ffn_conv, norm_mix_pre,
           norm_mix_post, norm_ffn_pre, norm_ffn_post, w_in, gla_w_gate2, gla_b_gate, gla_norm,
           ssd_conv_w, ssd_conv_b, ssd_dt_bias, ssd_A_log, ssd_D, ssd_norm, w_out, ffn_w_in,
           ffn_conv_w, ffn_conv_b, ffn_w_out):
    depth = w_in.shape[0]
    xp, xs = x_prompt, x_sample
    Bp, Lp, D = xp.shape
    Bs, Ls, _ = xs.shape
    F2 = ffn_w_in.shape[2]
    outs = [[] for _ in range(8)]
    for l in range(depth):
        p = _prep_params(l, norm_mix_pre, norm_mix_post, norm_ffn_pre, norm_ffn_post, w_in,
                         gla_w_gate2, gla_b_gate, gla_norm, ssd_conv_w, ssd_conv_b, ssd_dt_bias,
                         ssd_A_log, ssd_D, ssd_norm, w_out, ffn_w_in, ffn_conv_w, ffn_conv_b, ffn_w_out)
        cp = CHUNK
        xp, g1, h1, c1, f1 = _layer(
            xp,
            jnp.zeros((Bp, GLA_HEADS, GLA_HEAD_K, GLA_HEAD_V), F32),
            jnp.zeros((Bp, SSD_HEADS, SSD_HEADDIM, SSD_STATE), F32),
            jnp.zeros((Bp, SSD_CONV - 1, SSD_CONV_CH), F32),
            jnp.zeros((Bp, FFN_CONV - 1, F2), F32),
            p, nb=1, c=cp, ns=Bp, ffn_nseq=1, ffn_L=512, ffn_tps=Lp // 512)
        nbs = CHUNK // Ls
        xs, g2, h2, c2, f2 = _layer(
            xs, state_gla[l], state_ssd[l], state_ssd_conv[l], state_ffn_conv[l],
            p, nb=nbs, c=Ls, ns=2, ffn_nseq=512 // Ls, ffn_L=Ls, ffn_tps=1)
        for lst, val in zip(outs, (g1, h1, c1, f1, g2, h2, c2, f2)):
            lst.append(val)
    return (xp, xs) + tuple(jnp.stack(o) for o in outs)
```

```python
import functools

import numpy as np
import jax
import jax.numpy as jnp
from jax import lax
from jax.experimental import pallas as pl
from jax.experimental.pallas import tpu as pltpu

F32 = jnp.float32
BF16 = jnp.bfloat16
EPS = 1e-6

GLA_HEADS = 4
GLA_HEAD_K = 128
GLA_HEAD_V = 256
GLA_KDIM = GLA_HEADS * GLA_HEAD_K
GLA_WIDTH = GLA_HEADS * GLA_HEAD_V
GLA_LOWRANK = 16
GLA_GATE_NORMALIZER = 16.0
SSD_HEADS = 16
SSD_HEADDIM = 64
SSD_STATE = 128
SSD_GROUPS = 2
SSD_WIDTH = SSD_HEADS * SSD_HEADDIM
SSD_GROUP_WIDTH = SSD_WIDTH // SSD_GROUPS
SSD_CONV = 4
SSD_BC = 2 * SSD_GROUPS * SSD_STATE
SSD_CONV_CH = SSD_WIDTH + SSD_BC
FFN_CONV = 3
CHUNK = 64

A_Q = 0
A_K = GLA_KDIM
A_V = 2 * GLA_KDIM
A_G = A_V + GLA_WIDTH
A_W = A_G + GLA_WIDTH
B_Z = 0
B_XS = SSD_WIDTH
B_BC = B_XS + SSD_WIDTH
B_SMALL = B_BC + SSD_BC
SMALL_W = 128
DTR_OFF = GLA_LOWRANK
B_W = B_SMALL + SMALL_W

VMEM_LIMIT = 56 * 1024 * 1024


def _cparams(sem):
    return pltpu.CompilerParams(dimension_semantics=sem, vmem_limit_bytes=VMEM_LIMIT)


def _split3(x):
    hi = x.astype(BF16)
    r = x - hi.astype(F32)
    mid = r.astype(BF16)
    lo = (r - mid.astype(F32)).astype(BF16)
    return hi, mid, lo


def _dot(a, b):
    return jnp.dot(a, b, preferred_element_type=F32)


def _dot_nt(a, b):
    return lax.dot_general(a, b, (((1,), (1,)), ((), ())), preferred_element_type=F32)


def _dot_tn(a, b):
    return lax.dot_general(a, b, (((0,), (0,)), ((), ())), preferred_element_type=F32)


def _dot_sel(sel3_bf16, x_f32):
    return _dot(sel3_bf16, jnp.concatenate(_split3(x_f32), axis=0))


NEG_LOG2E = -1.4426950408889634


def _silu(x):
    return x / (1.0 + jnp.exp2(x * NEG_LOG2E))


def _softplus(x):
    return jnp.maximum(x, 0.0) + jnp.log1p(jnp.exp(-jnp.abs(x)))


def _rms(x, w):
    return x * lax.rsqrt(jnp.mean(x * x, axis=-1, keepdims=True) + EPS) * w


def _chunk_consts(nb, c):
    R = nb * c
    idx = np.arange(R)
    seq, pos = idx // c, idx % c
    same = seq[:, None] == seq[None, :]
    t, u = pos[:, None], pos[None, :]
    blocks = [same & (u <= t), same & (u > t)]
    masks = [np.eye(R, dtype=bool)]
    m = c // 2
    while m >= 1:
        blk = pos // (2 * m)
        rho = blk * 2 * m + m - 1
        upper = pos > rho
        a_up = upper[:, None] & (u > rho[:, None]) & (u <= t)
        a_lo = (~upper)[:, None] & (u > t) & (u <= rho[:, None])
        blocks.append(same & (a_up | a_lo))
        masks.append(same & upper[:, None] & (~upper)[None, :] & (blk[:, None] == blk[None, :]))
        m //= 2
    sel = np.concatenate(blocks, 0).astype(np.float32)
    lvl = np.stack(masks).astype(np.float32)
    causal = (same & (u <= t)).astype(np.float32)
    return sel, lvl, causal


def _inproj_kernel(x_ref, nw_ref, w_ref, o_ref):
    xn = _rms(x_ref[...], nw_ref[...]).astype(BF16)
    o_ref[...] = _dot(xn, w_ref[...])


def _inproj(x2d, nw, w_bf16, n_out, tm):
    M, D = x2d.shape
    return pl.pallas_call(
        _inproj_kernel,
        grid=(M // tm,),
        in_specs=[pl.BlockSpec((tm, D), lambda i: (i, 0)),
                  pl.BlockSpec((1, D), lambda i: (0, 0)),
                  pl.BlockSpec((D, n_out), lambda i: (0, 0))],
        out_specs=pl.BlockSpec((tm, n_out), lambda i: (i, 0)),
        out_shape=jax.ShapeDtypeStruct((M, n_out), F32),
        compiler_params=_cparams(("parallel",)),
        name="inproj",
    )(x2d, nw, w_bf16)


def _gla_group(gi, q_ref, k_ref, v_ref, g_ref, sm_ref, s_in, s_out, sel_ref, lvl_ref, w2_ref, bg_ref,
               gn_ref, o_ref, *, nb, c, nlev):
    R = nb * c
    sm = sm_ref[gi]
    sm_hi = sm.astype(BF16)
    sm_lo = (sm - sm_hi.astype(F32)).astype(BF16)
    zg = _dot(jnp.concatenate([sm_hi, sm_lo, sm_hi], axis=1), w2_ref[...]) + bg_ref[...]
    lg = -_softplus(-zg) * (1.0 / GLA_GATE_NORMALIZER)
    E = _dot_sel(sel_ref[...], lg)
    b = E[0:R]
    eb = jnp.exp(b)
    q = q_ref[gi] * (GLA_HEAD_K ** -0.5)
    k = k_ref[gi]
    qe = q * eb
    ke = k * jnp.exp(E[R:2 * R])
    ql, kl = [q.astype(BF16)], [k.astype(BF16)]
    for l in range(nlev):
        p = jnp.exp(E[(2 + l) * R:(3 + l) * R])
        ql.append((q * p).astype(BF16))
        kl.append((k * p).astype(BF16))
    vb = v_ref[gi].astype(BF16)
    g = g_ref[gi]
    gn = gn_ref[...]
    rows = lax.broadcasted_iota(jnp.int32, (R, 1), 0)

    for h in range(GLA_HEADS):
        ks = slice(h * GLA_HEAD_K, (h + 1) * GLA_HEAD_K)
        vs = slice(h * GLA_HEAD_V, (h + 1) * GLA_HEAD_V)
        att = jnp.zeros((R, R), F32)
        for l in range(nlev + 1):
            att = att + _dot_nt(ql[l][:, ks], kl[l][:, ks]) * lvl_ref[l]
        o = _dot(att.astype(BF16), vb[:, vs])
        for s in range(nb):
            if nb > 1:
                rm = jnp.logical_and(rows >= s * c, rows < (s + 1) * c).astype(F32)
                qs = (qe[:, ks] * rm).astype(BF16)
                kd = (ke[:, ks] * rm).astype(BF16)
            else:
                qs = qe[:, ks].astype(BF16)
                kd = ke[:, ks].astype(BF16)
            S = s_in[gi * nb + s, h]
            o = o + _dot(qs, S.astype(BF16))
            upd = _dot_tn(kd, vb[:, vs])
            d = eb[s * c + c - 1:s * c + c, ks]
            dcol = jnp.transpose(jnp.broadcast_to(d, (GLA_HEAD_K, GLA_HEAD_K)))
            s_out[gi * nb + s, h] = jnp.concatenate([dcol, dcol], axis=1) * S + upd
        gh = g[:, vs]
        o_ref[gi, :, vs] = (_rms(o, gn) * _silu(gh)).astype(BF16)


def _gla_kernel(q_ref, k_ref, v_ref, g_ref, sm_ref, s0_ref, sel_ref, lvl_ref, w2_ref, bg_ref, gn_ref,
                o_ref, sn_ref, *scratch, ns, nb, c, nlev):
    args = (q_ref, k_ref, v_ref, g_ref, sm_ref)
    consts = (sel_ref, lvl_ref, w2_ref, bg_ref, gn_ref, o_ref)
    kw = dict(nb=nb, c=c, nlev=nlev)
    if not scratch:
        for gi in range(ns):
            _gla_group(gi, *args, s0_ref, sn_ref, *consts, **kw)
        return
    s_scr, = scratch
    ci = pl.program_id(1)

    @pl.when(ci == 0)
    def _():
        s_scr[...] = s0_ref[...]

    for gi in range(ns):
        _gla_group(gi, *args, s_scr, s_scr, *consts, **kw)

    @pl.when(ci == pl.num_programs(1) - 1)
    def _():
        sn_ref[...] = s_scr[...]


def _group_view(a, G):
    return a.reshape(G, a.shape[0] // G, a.shape[1])


def _gla(proj_a, proj_b, s0, w2cat, bg, gn, *, B, L, nb, c, ns):
    R = nb * c
    G = B // nb
    Lg = L * nb
    ncl = Lg // R
    nlev = int(np.log2(c))
    sel, lvl, _ = _chunk_consts(nb, c)
    sel = jnp.asarray(np.tile(sel, (1, 3)), BF16)
    lvl = jnp.asarray(lvl, F32)
    M = proj_a.shape[0]
    pa, pb = _group_view(proj_a, G), _group_view(proj_b, G)

    def rowblk(w, col):
        return pl.BlockSpec((ns, R, w), lambda bi, ci: (bi, ci, col // w))

    const2 = lambda a: pl.BlockSpec(a.shape, lambda bi, ci: (0,) * a.ndim)
    sshape = (ns * nb, GLA_HEADS, GLA_HEAD_K, GLA_HEAD_V)
    sspec = pl.BlockSpec(sshape, lambda bi, ci: (bi, 0, 0, 0))
    kern = functools.partial(_gla_kernel, ns=ns, nb=nb, c=c, nlev=nlev)
    og, s_new = pl.pallas_call(
        kern,
        grid=(G // ns, ncl),
        in_specs=[rowblk(GLA_KDIM, A_Q), rowblk(GLA_KDIM, A_K), rowblk(GLA_WIDTH, A_V),
                  rowblk(GLA_WIDTH, A_G), rowblk(SMALL_W, B_SMALL), sspec,
                  const2(sel), const2(lvl), const2(w2cat), const2(bg), const2(gn)],
        out_specs=[pl.BlockSpec((ns, R, GLA_WIDTH), lambda bi, ci: (bi, ci, 0)), sspec],
        out_shape=[jax.ShapeDtypeStruct((G, Lg, GLA_WIDTH), BF16),
                   jax.ShapeDtypeStruct((B, GLA_HEADS, GLA_HEAD_K, GLA_HEAD_V), F32)],
        scratch_shapes=[pltpu.VMEM(sshape, F32)] if ncl > 1 else [],
        compiler_params=_cparams(("parallel", "arbitrary")),
        name="gla",
    )(pa, pa, pa, pa, pb, s0, sel, lvl, w2cat, bg, gn)
    return og.reshape(M, GLA_WIDTH), s_new


def _conv_taps(u3, prev, w, bias, width):
    n, L, C = u3.shape
    pos = lax.broadcasted_iota(jnp.int32, (1, L, 1), 1)
    acc = bias + w[width - 1:width] * u3
    for j in range(width - 1):
        d = width - 1 - j
        r = pltpu.roll(u3, d, axis=1)
        for tt in range(d):
            r = jnp.where(pos == tt, prev[:, j + tt:j + tt + 1, :], r)
        acc = acc + w[j:j + 1] * r
    return acc


def _ssd_group(gi, z_ref, xs_ref, bc_ref, sm_ref, sel_ref, cm_ref, ex_ref, cw_ref, cb_ref, dtb_ref,
               alog_ref, de_ref, nw_ref, y_ref, ht_scr, cx_scr, cbc_scr, *, nb, c):
    R = nb * c
    sq = slice(gi * nb, (gi + 1) * nb)
    cw = cw_ref[...]
    cbias = cb_ref[...]
    xraw = xs_ref[gi].reshape(nb, c, SSD_WIDTH)
    bcraw = bc_ref[gi].reshape(nb, c, SSD_BC)
    xs = _silu(_conv_taps(xraw, cx_scr[sq], cw[:, 0:SSD_WIDTH], cbias[:, 0:SSD_WIDTH], SSD_CONV))
    bca = _silu(_conv_taps(bcraw, cbc_scr[sq], cw[:, SSD_WIDTH:SSD_CONV_CH],
                           cbias[:, SSD_WIDTH:SSD_CONV_CH], SSD_CONV))
    cx_scr[sq] = xraw[:, c - (SSD_CONV - 1):c, :]
    cbc_scr[sq] = bcraw[:, c - (SSD_CONV - 1):c, :]
    xs = xs.reshape(R, SSD_WIDTH)
    bca = bca.reshape(R, SSD_BC)
    Bm = bca[:, 0:SSD_GROUPS * SSD_STATE]
    Cm = bca[:, SSD_GROUPS * SSD_STATE:SSD_BC]

    dt = _softplus(sm_ref[gi] + dtb_ref[...])
    la = dt * (-jnp.exp(alog_ref[...]))
    cs = _dot_sel(sel_ref[...], la)
    cum = cs[0:R]
    stack = jnp.concatenate([cs, dt], axis=0)
    st_e = _dot(jnp.concatenate(_split3(stack), axis=1), ex_ref[...])
    cum_e = st_e[0:R]
    lmc_e = st_e[R:2 * R]
    dt_e = st_e[2 * R:3 * R]
    ecum_e = jnp.exp(cum_e)
    xdt = xs * dt_e
    xw = (xdt * jnp.exp(lmc_e)).astype(BF16)
    xdtb = xdt.astype(BF16)
    Bb = Bm.astype(BF16)
    Cb = Cm.astype(BF16)

    cumT2 = jnp.transpose(jnp.concatenate([cum, cum], axis=0))
    cmask2 = cm_ref[...] > 0.5
    rows = lax.broadcasted_iota(jnp.int32, (R, 1), 0)
    lane = lax.broadcasted_iota(jnp.int32, (1, 2 * R), 1)
    first = lane < R
    lo_half = lax.broadcasted_iota(jnp.int32, (1, 128), 1) < SSD_HEADDIM

    hpg = SSD_HEADS // SSD_GROUPS
    for hg in range(SSD_GROUPS):
        ns = slice(hg * SSD_STATE, (hg + 1) * SSD_STATE)
        gs = slice(hg * SSD_GROUP_WIDTH, (hg + 1) * SSD_GROUP_WIDTH)
        bgrp = Bb[:, ns]
        cb2 = _dot_nt(Cb[:, ns], jnp.concatenate([bgrp, bgrp], axis=0))
        yg = []
        for j in range(hpg // 2):
            ls = slice(hg * SSD_GROUP_WIDTH + j * 128, hg * SSD_GROUP_WIDTH + (j + 1) * 128)
            h0 = DTR_OFF + hg * hpg + 2 * j
            col = jnp.where(first, cum[:, h0:h0 + 1], cum[:, h0 + 1:h0 + 2])
            row = jnp.where(first, cumT2[h0:h0 + 1, :], cumT2[h0 + 1:h0 + 2, :])
            m2 = cb2 * jnp.exp(jnp.where(cmask2, col - row, -jnp.inf))
            xpair = xdt[:, ls]
            x2 = jnp.concatenate([jnp.where(lo_half, xpair, 0.0).astype(BF16),
                                  jnp.where(lo_half, 0.0, xpair).astype(BF16)], axis=0)
            yg.append(_dot(m2.astype(BF16), x2))
        y_intra = jnp.concatenate(yg, axis=1)
        y_inter = jnp.zeros((R, SSD_GROUP_WIDTH), F32)
        for s in range(nb):
            if nb > 1:
                rm = jnp.logical_and(rows >= s * c, rows < (s + 1) * c).astype(F32)
                cg = (Cm[:, ns] * rm).astype(BF16)
                bg_ = (Bm[:, ns] * rm).astype(BF16)
            else:
                cg = Cb[:, ns]
                bg_ = Bb[:, ns]
            hT = ht_scr[gi * nb + s, :, gs]
            y_inter = y_inter + _dot(cg, hT.astype(BF16))
            upd = _dot_tn(bg_, xw[:, gs])
            dl = ecum_e[s * c + c - 1:s * c + c, gs]
            ht_scr[gi * nb + s, :, gs] = dl * hT + upd
        y = y_intra + y_inter * ecum_e[:, gs] + de_ref[:, gs] * xs[:, gs]
        y = y * _silu(z_ref[gi, :, gs])
        y_ref[gi, :, gs] = _rms(y, nw_ref[:, gs]).astype(BF16)


def _ssd_kernel(z_ref, xs_ref, bc_ref, sm_ref, cst_ref, h0_ref, sel_ref, cm_ref, ex_ref,
                cw_ref, cb_ref, dtb_ref, alog_ref, de_ref, nw_ref,
                y_ref, hn_ref, ht_scr, cx_scr, cbc_scr, *, ns, nb, c):
    ci = pl.program_id(1)

    @pl.when(ci == 0)
    def _():
        cx_scr[...] = cst_ref[:, :, 0:SSD_WIDTH]
        cbc_scr[...] = cst_ref[:, :, SSD_WIDTH:SSD_CONV_CH]
        for s in range(ns * nb):
            ht_scr[s] = jnp.transpose(h0_ref[s].reshape(SSD_WIDTH, SSD_STATE))

    for gi in range(ns):
        _ssd_group(gi, z_ref, xs_ref, bc_ref, sm_ref, sel_ref, cm_ref, ex_ref, cw_ref, cb_ref, dtb_ref,
                   alog_ref, de_ref, nw_ref, y_ref, ht_scr, cx_scr, cbc_scr, nb=nb, c=c)

    @pl.when(ci == pl.num_programs(1) - 1)
    def _():
        for s in range(ns * nb):
            hn_ref[s] = jnp.transpose(ht_scr[s]).reshape(SSD_HEADS, SSD_HEADDIM, SSD_STATE)


def _ssd(proj_b, cst, h0, cw, cb, dtb, alog, de, nw, *, B, L, nb, c, ns):
    R = nb * c
    G = B // nb
    Lg = L * nb
    ncl = Lg // R
    sel, _, causal = _chunk_consts(nb, c)
    assert 2 * R == 128, "two heads share one 128-lane tile"
    sel = jnp.asarray(np.tile(sel[:2 * R], (1, 3)), BF16)
    causal = jnp.asarray(np.tile(causal, (1, 2)), F32)
    exn = np.zeros((SMALL_W, SSD_WIDTH), np.float32)
    for h in range(SSD_HEADS):
        exn[DTR_OFF + h, h * SSD_HEADDIM:(h + 1) * SSD_HEADDIM] = 1.0
    ex = jnp.asarray(np.tile(exn, (3, 1)), BF16)
    M = proj_b.shape[0]
    pb = _group_view(proj_b, G)

    def rowblk(w, col):
        return pl.BlockSpec((ns, R, w), lambda bi, ci: (bi, ci, col // w))

    const2 = lambda a: pl.BlockSpec(a.shape, lambda bi, ci: (0,) * a.ndim)
    nsq = ns * nb
    hspec = pl.BlockSpec((nsq, SSD_HEADS, SSD_HEADDIM, SSD_STATE), lambda bi, ci: (bi, 0, 0, 0))
    kern = functools.partial(_ssd_kernel, ns=ns, nb=nb, c=c)
    ys, h_new = pl.pallas_call(
        kern,
        grid=(G // ns, ncl),
        in_specs=[rowblk(SSD_WIDTH, B_Z), rowblk(SSD_WIDTH, B_XS), rowblk(SSD_BC, B_BC),
                  rowblk(SMALL_W, B_SMALL),
                  pl.BlockSpec((nsq, SSD_CONV - 1, SSD_CONV_CH), lambda bi, ci: (bi, 0, 0)), hspec,
                  const2(sel), const2(causal), const2(ex),
                  const2(cw), const2(cb), const2(dtb), const2(alog), const2(de), const2(nw)],
        out_specs=[pl.BlockSpec((ns, R, SSD_WIDTH), lambda bi, ci: (bi, ci, 0)), hspec],
        out_shape=[jax.ShapeDtypeStruct((G, Lg, SSD_WIDTH), BF16),
                   jax.ShapeDtypeStruct((B, SSD_HEADS, SSD_HEADDIM, SSD_STATE), F32)],
        scratch_shapes=[pltpu.VMEM((nsq, SSD_STATE, SSD_WIDTH), F32),
                        pltpu.VMEM((nsq, SSD_CONV - 1, SSD_WIDTH), F32),
                        pltpu.VMEM((nsq, SSD_CONV - 1, SSD_BC), F32)],
        compiler_params=_cparams(("parallel", "arbitrary")),
        name="ssd",
    )(pb, pb, pb, pb, cst, h0, sel, causal, ex, cw, cb, dtb, alog, de, nw)
    return ys.reshape(M, SSD_WIDTH), h_new


def _outproj_kernel(o_ref, y_ref, x_ref, wt_ref, wb_ref, npost_ref, npre_ref, h_ref, xn_ref):
    mix = _dot(o_ref[...], wt_ref[...]) + _dot(y_ref[...], wb_ref[...])
    h = x_ref[...] + _rms(mix, npost_ref[...])
    h_ref[...] = h
    xn_ref[...] = _rms(h, npre_ref[...]).astype(BF16)


def _outproj(og, ys, x2d, w_out_bf16, npost, npre, tm):
    M, D = x2d.shape
    half = w_out_bf16.shape[0] // 2
    return pl.pallas_call(
        _outproj_kernel,
        grid=(M // tm,),
        in_specs=[pl.BlockSpec((tm, half), lambda i: (i, 0)),
                  pl.BlockSpec((tm, half), lambda i: (i, 0)),
                  pl.BlockSpec((tm, D), lambda i: (i, 0)),
                  pl.BlockSpec((half, D), lambda i: (0, 0)),
                  pl.BlockSpec((half, D), lambda i: (1, 0)),
                  pl.BlockSpec((1, D), lambda i: (0, 0)),
                  pl.BlockSpec((1, D), lambda i: (0, 0))],
        out_specs=[pl.BlockSpec((tm, D), lambda i: (i, 0)),
                   pl.BlockSpec((tm, D), lambda i: (i, 0))],
        out_shape=[jax.ShapeDtypeStruct((M, D), F32), jax.ShapeDtypeStruct((M, D), BF16)],
        compiler_params=_cparams(("parallel",)),
        name="outproj",
    )(og, ys, x2d, w_out_bf16, w_out_bf16, npost, npre)


SUBLANES = 8
FFN_HDR = SUBLANES
FFN_RBLK = 64
FFN_SPLIT = 2


def _ffn_kernel(xn_ref, h_ref, sa_ref, sb_ref, wa_ref, wb_ref, cwa_ref, cwb_ref, cba_ref, cbb_ref,
                wo_ref, npost_ref, y_ref, na_ref, nb_ref, ua_scr, ub_scr, act_scr, *carry,
                nseq, L, tps):
    i = pl.program_id(0)
    j = pl.program_id(1)
    tf = wa_ref.shape[1]
    W1 = FFN_CONV - 1
    H = FFN_HDR
    first = i % tps == 0

    if tps > 1:
        ca_scr, cb_scr = carry

        @pl.when(jnp.logical_not(first))
        def _():
            ua_scr[:, 0:H, :] = ca_scr[j]
            ub_scr[:, 0:H, :] = cb_scr[j]

    @pl.when(first)
    def _():
        ua_scr[:, H - W1:H, :] = sa_ref[...]
        ub_scr[:, H - W1:H, :] = sb_ref[...]

    @pl.when(j == 0)
    def _():
        y_ref[...] = jnp.zeros_like(y_ref)

    if nseq == 1:
        parts = [(0, 1, k * L // FFN_SPLIT, (k + 1) * L // FFN_SPLIT) for k in range(FFN_SPLIT)]
    else:
        parts = [(k * nseq // FFN_SPLIT, (k + 1) * nseq // FFN_SPLIT, 0, L) for k in range(FFN_SPLIT)]
    tile_rows = lambda q0, q1, r0, r1: slice(q0 * L + r0, (q1 - 1) * L + r1)

    for (q0, q1, r0, r1) in parts:
        rs = tile_rows(q0, q1, r0, r1)
        ua_scr[q0:q1, H + r0:H + r1, :] = _dot(xn_ref[rs, :], wa_ref[...]).reshape(q1 - q0, r1 - r0, tf)
        ub_scr[q0:q1, H + r0:H + r1, :] = _dot(xn_ref[rs, :], wb_ref[...]).reshape(q1 - q0, r1 - r0, tf)

    spread = lambda row: jnp.broadcast_to(row, (SUBLANES, tf))
    wa8 = [spread(cwa_ref[t:t + 1, :]) for t in range(FFN_CONV)]
    wb8 = [spread(cwb_ref[t:t + 1, :]) for t in range(FFN_CONV)]
    ba8 = spread(cba_ref[...])
    bb8 = spread(cbb_ref[...])
    for (q0, q1, r0, r1) in parts:
        if nseq == 1:
            blocks = [(0, 1, r, r + FFN_RBLK) for r in range(r0, r1, FFN_RBLK)]
        else:
            qb = FFN_RBLK // L
            blocks = [(q, q + qb, 0, L) for q in range(q0, q1, qb)]
        for (bq0, bq1, br0, br1) in blocks:
            def conv(u_scr, w8, b8):
                tap = lambda d: u_scr[bq0:bq1, H - d + br0:H - d + br1, :].reshape(-1, SUBLANES, tf)
                out = b8 + w8[W1] * tap(0)
                for t in range(W1):
                    out = out + w8[t] * tap(W1 - t)
                return out
            a = conv(ua_scr, wa8, ba8)
            b = conv(ub_scr, wb8, bb8)
            row0 = bq0 * L + br0
            act_scr[row0:row0 + FFN_RBLK, :] = (_silu(a) * b).reshape(FFN_RBLK, tf).astype(BF16)
        rs = tile_rows(q0, q1, r0, r1)
        y_ref[rs, :] += _dot(act_scr[rs, :], wo_ref[...])

    na_ref[...] = ua_scr[:, H + L - W1:H + L, :]
    nb_ref[...] = ub_scr[:, H + L - W1:H + L, :]
    if tps > 1:
        ca_scr[j] = ua_scr[:, L:L + H, :]
        cb_scr[j] = ub_scr[:, L:L + H, :]

    @pl.when(j == pl.num_programs(1) - 1)
    def _():
        y_ref[...] = h_ref[...] + _rms(y_ref[...], npost_ref[...])


def _ffn(xn2, h2d, st, w_in_bf16, cw, cb, w_out_bf16, npost, *, nseq, L, tps, tf):
    M, D = h2d.shape
    F = w_out_bf16.shape[0]
    nj = F // tf
    tm = nseq * L
    W1 = FFN_CONV - 1
    assert (L if nseq == 1 else nseq * L) % (FFN_SPLIT * FFN_RBLK) == 0
    kern = functools.partial(_ffn_kernel, nseq=nseq, L=L, tps=tps)
    stspec_a = pl.BlockSpec((nseq, W1, tf), lambda i, j: (i // tps, 0, j))
    stspec_b = pl.BlockSpec((nseq, W1, tf), lambda i, j: (i // tps, 0, j + nj))
    return pl.pallas_call(
        kern,
        grid=(M // tm, nj),
        in_specs=[pl.BlockSpec((tm, D), lambda i, j: (i, 0), pipeline_mode=pl.Buffered(1)),
                  pl.BlockSpec((tm, D), lambda i, j: (i, 0), pipeline_mode=pl.Buffered(1)),
                  stspec_a, stspec_b,
                  pl.BlockSpec((D, tf), lambda i, j: (0, j)),
                  pl.BlockSpec((D, tf), lambda i, j: (0, j + nj)),
                  pl.BlockSpec((FFN_CONV, tf), lambda i, j: (0, j)),
                  pl.BlockSpec((FFN_CONV, tf), lambda i, j: (0, j + nj)),
                  pl.BlockSpec((1, tf), lambda i, j: (0, j)),
                  pl.BlockSpec((1, tf), lambda i, j: (0, j + nj)),
                  pl.BlockSpec((tf, D), lambda i, j: (j, 0)),
                  pl.BlockSpec((1, D), lambda i, j: (0, 0))],
        out_specs=[pl.BlockSpec((tm, D), lambda i, j: (i, 0)),
                   pl.BlockSpec((nseq, W1, tf), lambda i, j: (i, 0, j)),
                   pl.BlockSpec((nseq, W1, tf), lambda i, j: (i, 0, j))],
        out_shape=[jax.ShapeDtypeStruct((M, D), F32),
                   jax.ShapeDtypeStruct((M // L, W1, F), F32),
                   jax.ShapeDtypeStruct((M // L, W1, F), F32)],
        scratch_shapes=[pltpu.VMEM((nseq, FFN_HDR + L, tf), F32),
                        pltpu.VMEM((nseq, FFN_HDR + L, tf), F32),
                        pltpu.VMEM((tm, tf), BF16)] + (
                            [pltpu.VMEM((nj, nseq, FFN_HDR, tf), F32)] * 2 if tps > 1 else []),
        compiler_params=_cparams(("arbitrary", "arbitrary")),
        name="ffn",
    )(xn2, h2d, st, st, w_in_bf16, w_in_bf16, cw, cw, cb, cb, w_out_bf16, npost)


def _layer(x, s_gla, s_ssd, s_conv, s_ffn, p, *, nb, c, ns, ffn_nseq, ffn_L, ffn_tps):
    B, L, D = x.shape
    M = B * L
    x2d = x.reshape(M, D)
    proj_a = _inproj(x2d, p['n_mix_pre'], p['w_in_a'], A_W, 512)
    proj_b = _inproj(x2d, p['n_mix_pre'], p['w_in_b'], B_W, 512)
    og, g_new = _gla(proj_a, proj_b, s_gla, p['w2cat'], p['bg'], p['gn'], B=B, L=L, nb=nb, c=c, ns=ns)
    ys, h_new = _ssd(proj_b, s_conv, s_ssd, p['ssd_cw'], p['ssd_cb'], p['dtb'], p['alog'], p['de'],
                     p['ssd_nw'], B=B, L=L, nb=nb, c=c, ns=ns)
    hres, xn2 = _outproj(og, ys, x2d, p['w_out'], p['n_mix_post'], p['n_ffn_pre'], 512)
    y, fa, fb = _ffn(xn2, hres, s_ffn, p['ffn_w_in'], p['ffn_cw'], p['ffn_cb'], p['ffn_w_out'],
                     p['n_ffn_post'], nseq=ffn_nseq, L=ffn_L, tps=ffn_tps, tf=512)
    c_new = proj_b.reshape(B, L, -1)[:, L - (SSD_CONV - 1):, B_XS:B_XS + SSD_CONV_CH]
    f_new = jnp.concatenate([fa, fb], axis=-1)[ffn_tps - 1::ffn_tps]
    return y.reshape(B, L, D), g_new, h_new, c_new, f_new


def _prep_params(l, norm_mix_pre, norm_mix_post, norm_ffn_pre, norm_ffn_post, w_in, gla_w_gate2,
                 gla_b_gate, gla_norm, ssd_conv_w, ssd_conv_b, ssd_dt_bias, ssd_A_log, ssd_D, ssd_norm,
                 w_out, ffn_w_in, ffn_conv_w, ffn_conv_b, ffn_w_out):
    D = w_in.shape[1]
    sizes = (GLA_KDIM, GLA_KDIM, GLA_WIDTH, GLA_WIDTH, GLA_LOWRANK, SSD_WIDTH, SSD_CONV_CH, SSD_HEADS)
    offs = np.cumsum((0,) + sizes)
    wb16 = w_in[l].astype(BF16)
    npad = SMALL_W - GLA_LOWRANK - SSD_HEADS
    w_in_b = jnp.concatenate([wb16[:, offs[5]:offs[7]], wb16[:, offs[4]:offs[5]], wb16[:, offs[7]:offs[8]],
                              jnp.zeros((D, npad), BF16)], axis=1)
    w2p = jnp.zeros((SMALL_W, GLA_KDIM), F32).at[0:GLA_LOWRANK].set(gla_w_gate2[l])
    w2_hi = w2p.astype(BF16)
    w2_lo = (w2p - w2_hi.astype(F32)).astype(BF16)
    w2cat = jnp.concatenate([w2_hi, w2_hi, w2_lo], axis=0)
    pad_small = lambda v: jnp.zeros((1, SMALL_W), F32).at[0, DTR_OFF:DTR_OFF + SSD_HEADS].set(v)
    row = lambda v: v.reshape(1, -1)
    return dict(
        n_mix_pre=row(norm_mix_pre[l]), n_mix_post=row(norm_mix_post[l]),
        n_ffn_pre=row(norm_ffn_pre[l]), n_ffn_post=row(norm_ffn_post[l]),
        w_in_a=wb16, w_in_b=w_in_b, w2cat=w2cat, bg=row(gla_b_gate[l]), gn=row(gla_norm[l]),
        ssd_cw=ssd_conv_w[l], ssd_cb=row(ssd_conv_b[l]),
        dtb=pad_small(ssd_dt_bias[l]), alog=pad_small(ssd_A_log[l]),
        de=row(jnp.repeat(ssd_D[l], SSD_HEADDIM)), ssd_nw=row(ssd_norm[l]),
        w_out=w_out[l].astype(BF16), ffn_w_in=ffn_w_in[l].astype(BF16),
        ffn_cw=ffn_conv_w[l], ffn_cb=row(ffn_conv_b[l]), ffn_w_out=ffn_w_out[l].astype(BF16))


def kernel(x_prompt, x_sample, state_gla, state_ssd, state_ssd_conv, state_ffn_conv, norm_mix_pre,
           norm_mix_post, norm_ffn_pre, norm_ffn_post, w_in, gla_w_gate2, gla_b_gate, gla_norm,
           ssd_conv_w, ssd_conv_b, ssd_dt_bias, ssd_A_log, ssd_D, ssd_norm, w_out, ffn_w_in,
           ffn_conv_w, ffn_conv_b, ffn_w_out):
    depth = w_in.shape[0]
    xp, xs = x_prompt, x_sample
    Bp, Lp, D = xp.shape
    Bs, Ls, _ = xs.shape
    F2 = ffn_w_in.shape[2]
    outs = [[] for _ in range(8)]
    for l in range(depth):
        p = _prep_params(l, norm_mix_pre, norm_mix_post, norm_ffn_pre, norm_ffn_post, w_in,
                         gla_w_gate2, gla_b_gate, gla_norm, ssd_conv_w, ssd_conv_b, ssd_dt_bias,
                         ssd_A_log, ssd_D, ssd_norm, w_out, ffn_w_in, ffn_conv_w, ffn_conv_b, ffn_w_out)
        cp = CHUNK
        xp, g1, h1, c1, f1 = _layer(
            xp,
            jnp.zeros((Bp, GLA_HEADS, GLA_HEAD_K, GLA_HEAD_V), F32),
            jnp.zeros((Bp, SSD_HEADS, SSD_HEADDIM, SSD_STATE), F32),
            jnp.zeros((Bp, SSD_CONV - 1, SSD_CONV_CH), F32),
            jnp.zeros((Bp, FFN_CONV - 1, F2), F32),
            p, nb=1, c=cp, ns=Bp, ffn_nseq=1, ffn_L=1024, ffn_tps=Lp // 1024)
        nbs = CHUNK // Ls
        xs, g2, h2, c2, f2 = _layer(
            xs, state_gla[l], state_ssd[l], state_ssd_conv[l], state_ffn_conv[l],
            p, nb=nbs, c=Ls, ns=2, ffn_nseq=512 // Ls, ffn_L=Ls, ffn_tps=1)
        for lst, val in zip(outs, (g1, h1, c1, f1, g2, h2, c2, f2)):
            lst.append(val)
    return (xp, xs) + tuple(jnp.stack(o) for o in outs)
```

```python
import functools

import numpy as np
import jax
import jax.numpy as jnp
from jax import lax
from jax.experimental import pallas as pl
from jax.experimental.pallas import tpu as pltpu

F32 = jnp.float32
BF16 = jnp.bfloat16
EPS = 1e-6

GLA_HEADS = 4
GLA_HEAD_K = 128
GLA_HEAD_V = 256
GLA_KDIM = GLA_HEADS * GLA_HEAD_K
GLA_WIDTH = GLA_HEADS * GLA_HEAD_V
GLA_LOWRANK = 16
GLA_GATE_NORMALIZER = 16.0
SSD_HEADS = 16
SSD_HEADDIM = 64
SSD_STATE = 128
SSD_GROUPS = 2
SSD_WIDTH = SSD_HEADS * SSD_HEADDIM
SSD_GROUP_WIDTH = SSD_WIDTH // SSD_GROUPS
SSD_CONV = 4
SSD_BC = 2 * SSD_GROUPS * SSD_STATE
SSD_CONV_CH = SSD_WIDTH + SSD_BC
FFN_CONV = 3
CHUNK = 64

A_Q = 0
A_K = GLA_KDIM
A_V = 2 * GLA_KDIM
A_G = A_V + GLA_WIDTH
A_W = A_G + GLA_WIDTH
B_Z = 0
B_XS = SSD_WIDTH
B_BC = B_XS + SSD_WIDTH
B_SMALL = B_BC + SSD_BC
SMALL_W = 128
DTR_OFF = GLA_LOWRANK
B_W = B_SMALL + SMALL_W

VMEM_LIMIT = 56 * 1024 * 1024


def _cparams(sem):
    return pltpu.CompilerParams(dimension_semantics=sem, vmem_limit_bytes=VMEM_LIMIT)


def _split3(x):
    hi = x.astype(BF16)
    r = x - hi.astype(F32)
    mid = r.astype(BF16)
    lo = (r - mid.astype(F32)).astype(BF16)
    return hi, mid, lo


def _dot(a, b):
    return jnp.dot(a, b, preferred_element_type=F32)


def _dot_nt(a, b):
    return lax.dot_general(a, b, (((1,), (1,)), ((), ())), preferred_element_type=F32)


def _dot_tn(a, b):
    return lax.dot_general(a, b, (((0,), (0,)), ((), ())), preferred_element_type=F32)


def _dot_sel(sel3_bf16, x_f32):
    return _dot(sel3_bf16, jnp.concatenate(_split3(x_f32), axis=0))


NEG_LOG2E = -1.4426950408889634


def _silu(x):
    return x / (1.0 + jnp.exp2(x * NEG_LOG2E))


def _softplus(x):
    return jnp.maximum(x, 0.0) + jnp.log1p(jnp.exp(-jnp.abs(x)))


def _rms(x, w):
    return x * lax.rsqrt(jnp.mean(x * x, axis=-1, keepdims=True) + EPS) * w


def _chunk_consts(nb, c):
    R = nb * c
    idx = np.arange(R)
    seq, pos = idx // c, idx % c
    same = seq[:, None] == seq[None, :]
    t, u = pos[:, None], pos[None, :]
    blocks = [same & (u <= t), same & (u > t)]
    masks = [np.eye(R, dtype=bool)]
    m = c // 2
    while m >= 1:
        blk = pos // (2 * m)
        rho = blk * 2 * m + m - 1
        upper = pos > rho
        a_up = upper[:, None] & (u > rho[:, None]) & (u <= t)
        a_lo = (~upper)[:, None] & (u > t) & (u <= rho[:, None])
        blocks.append(same & (a_up | a_lo))
        masks.append(same & upper[:, None] & (~upper)[None, :] & (blk[:, None] == blk[None, :]))
        m //= 2
    sel = np.concatenate(blocks, 0).astype(np.float32)
    lvl = np.stack(masks).astype(np.float32)
    causal = (same & (u <= t)).astype(np.float32)
    return sel, lvl, causal


def _inproj_kernel(x_ref, nw_ref, w_ref, o_ref):
    xn = _rms(x_ref[...], nw_ref[...]).astype(BF16)
    o_ref[...] = _dot(xn, w_ref[...])


def _inproj(x2d, nw, w_bf16, n_out, tm):
    M, D = x2d.shape
    return pl.pallas_call(
        _inproj_kernel,
        grid=(M // tm,),
        in_specs=[pl.BlockSpec((tm, D), lambda i: (i, 0)),
                  pl.BlockSpec((1, D), lambda i: (0, 0)),
                  pl.BlockSpec((D, n_out), lambda i: (0, 0))],
        out_specs=pl.BlockSpec((tm, n_out), lambda i: (i, 0)),
        out_shape=jax.ShapeDtypeStruct((M, n_out), F32),
        compiler_params=_cparams(("parallel",)),
        name="inproj",
    )(x2d, nw, w_bf16)


def _gla_group(gi, q_ref, k_ref, v_ref, g_ref, sm_ref, s_in, s_out, sel_ref, lvl_ref, w2_ref, bg_ref,
               gn_ref, o_ref, *, nb, c, nlev):
    R = nb * c
    sm = sm_ref[gi]
    sm_hi = sm.astype(BF16)
    sm_lo = (sm - sm_hi.astype(F32)).astype(BF16)
    zg = _dot(jnp.concatenate([sm_hi, sm_lo, sm_hi], axis=1), w2_ref[...]) + bg_ref[...]
    lg = -_softplus(-zg) * (1.0 / GLA_GATE_NORMALIZER)
    E = _dot_sel(sel_ref[...], lg)
    b = E[0:R]
    eb = jnp.exp(b)
    q = q_ref[gi] * (GLA_HEAD_K ** -0.5)
    k = k_ref[gi]
    qe = q * eb
    ke = k * jnp.exp(E[R:2 * R])
    ql, kl = [q.astype(BF16)], [k.astype(BF16)]
    for l in range(nlev):
        p = jnp.exp(E[(2 + l) * R:(3 + l) * R])
        ql.append((q * p).astype(BF16))
        kl.append((k * p).astype(BF16))
    vb = v_ref[gi].astype(BF16)
    g = g_ref[gi]
    gn = gn_ref[...]
    rows = lax.broadcasted_iota(jnp.int32, (R, 1), 0)

    zk = jnp.zeros((R, GLA_HEAD_K), BF16)
    zv = jnp.zeros((R, GLA_HEAD_V), BF16)
    o_intra = []
    for hp in range(GLA_HEADS // 2):
        k0 = slice(2 * hp * GLA_HEAD_K, (2 * hp + 1) * GLA_HEAD_K)
        k1 = slice((2 * hp + 1) * GLA_HEAD_K, (2 * hp + 2) * GLA_HEAD_K)
        k01 = slice(2 * hp * GLA_HEAD_K, (2 * hp + 2) * GLA_HEAD_K)
        att2 = jnp.zeros((R, 2 * R), F32)
        for l in range(nlev + 1):
            kbd = jnp.concatenate([jnp.concatenate([kl[l][:, k0], zk], axis=1),
                                   jnp.concatenate([zk, kl[l][:, k1]], axis=1)], axis=0)
            att2 = att2 + _dot_nt(ql[l][:, k01], kbd) * lvl_ref[l]
        v0 = vb[:, 2 * hp * GLA_HEAD_V:(2 * hp + 1) * GLA_HEAD_V]
        v1 = vb[:, (2 * hp + 1) * GLA_HEAD_V:(2 * hp + 2) * GLA_HEAD_V]
        vbd = jnp.concatenate([jnp.concatenate([v0, zv], axis=1),
                               jnp.concatenate([zv, v1], axis=1)], axis=0)
        o2 = _dot(att2.astype(BF16), vbd)
        o_intra += [o2[:, 0:GLA_HEAD_V], o2[:, GLA_HEAD_V:2 * GLA_HEAD_V]]

    for h in range(GLA_HEADS):
        ks = slice(h * GLA_HEAD_K, (h + 1) * GLA_HEAD_K)
        vs = slice(h * GLA_HEAD_V, (h + 1) * GLA_HEAD_V)
        o = o_intra[h]
        for s in range(nb):
            if nb > 1:
                rm = jnp.logical_and(rows >= s * c, rows < (s + 1) * c).astype(F32)
                qs = (qe[:, ks] * rm).astype(BF16)
                kd = (ke[:, ks] * rm).astype(BF16)
            else:
                qs = qe[:, ks].astype(BF16)
                kd = ke[:, ks].astype(BF16)
            S = s_in[gi * nb + s, h]
            o = o + _dot(qs, S.astype(BF16))
            upd = _dot_tn(kd, vb[:, vs])
            d = eb[s * c + c - 1:s * c + c, ks]
            dcol = jnp.transpose(jnp.broadcast_to(d, (GLA_HEAD_K, GLA_HEAD_K)))
            s_out[gi * nb + s, h] = jnp.concatenate([dcol, dcol], axis=1) * S + upd
        gh = g[:, vs]
        o_ref[gi, :, vs] = (_rms(o, gn) * _silu(gh)).astype(BF16)


def _gla_kernel(q_ref, k_ref, v_ref, g_ref, sm_ref, s0_ref, sel_ref, lvl_ref, w2_ref, bg_ref, gn_ref,
                o_ref, sn_ref, *scratch, ns, nb, c, nlev):
    args = (q_ref, k_ref, v_ref, g_ref, sm_ref)
    consts = (sel_ref, lvl_ref, w2_ref, bg_ref, gn_ref, o_ref)
    kw = dict(nb=nb, c=c, nlev=nlev)
    if not scratch:
        for gi in range(ns):
            _gla_group(gi, *args, s0_ref, sn_ref, *consts, **kw)
        return
    s_scr, = scratch
    ci = pl.program_id(1)

    @pl.when(ci == 0)
    def _():
        s_scr[...] = s0_ref[...]

    for gi in range(ns):
        _gla_group(gi, *args, s_scr, s_scr, *consts, **kw)

    @pl.when(ci == pl.num_programs(1) - 1)
    def _():
        sn_ref[...] = s_scr[...]


def _group_view(a, G):
    return a.reshape(G, a.shape[0] // G, a.shape[1])


def _gla(proj_a, proj_b, s0, w2cat, bg, gn, *, B, L, nb, c, ns):
    R = nb * c
    G = B // nb
    Lg = L * nb
    ncl = Lg // R
    nlev = int(np.log2(c))
    sel, lvl, _ = _chunk_consts(nb, c)
    assert 2 * R == 128, "two heads share one 128-lane score tile"
    sel = jnp.asarray(np.tile(sel, (1, 3)), BF16)
    lvl = jnp.asarray(np.tile(lvl, (1, 1, 2)), F32)
    M = proj_a.shape[0]
    pa, pb = _group_view(proj_a, G), _group_view(proj_b, G)

    def rowblk(w, col):
        return pl.BlockSpec((ns, R, w), lambda bi, ci: (bi, ci, col // w))

    const2 = lambda a: pl.BlockSpec(a.shape, lambda bi, ci: (0,) * a.ndim)
    sshape = (ns * nb, GLA_HEADS, GLA_HEAD_K, GLA_HEAD_V)
    sspec = pl.BlockSpec(sshape, lambda bi, ci: (bi, 0, 0, 0))
    kern = functools.partial(_gla_kernel, ns=ns, nb=nb, c=c, nlev=nlev)
    og, s_new = pl.pallas_call(
        kern,
        grid=(G // ns, ncl),
        in_specs=[rowblk(GLA_KDIM, A_Q), rowblk(GLA_KDIM, A_K), rowblk(GLA_WIDTH, A_V),
                  rowblk(GLA_WIDTH, A_G), rowblk(SMALL_W, B_SMALL), sspec,
                  const2(sel), const2(lvl), const2(w2cat), const2(bg), const2(gn)],
        out_specs=[pl.BlockSpec((ns, R, GLA_WIDTH), lambda bi, ci: (bi, ci, 0)), sspec],
        out_shape=[jax.ShapeDtypeStruct((G, Lg, GLA_WIDTH), BF16),
                   jax.ShapeDtypeStruct((B, GLA_HEADS, GLA_HEAD_K, GLA_HEAD_V), F32)],
        scratch_shapes=[pltpu.VMEM(sshape, F32)] if ncl > 1 else [],
        compiler_params=_cparams(("parallel", "arbitrary")),
        name="gla",
    )(pa, pa, pa, pa, pb, s0, sel, lvl, w2cat, bg, gn)
    return og.reshape(M, GLA_WIDTH), s_new


def _conv_taps(u3, prev, w, bias, width):
    n, L, C = u3.shape
    pos = lax.broadcasted_iota(jnp.int32, (1, L, 1), 1)
    acc = bias + w[width - 1:width] * u3
    for j in range(width - 1):
        d = width - 1 - j
        r = pltpu.roll(u3, d, axis=1)
        for tt in range(d):
            r = jnp.where(pos == tt, prev[:, j + tt:j + tt + 1, :], r)
        acc = acc + w[j:j + 1] * r
    return acc


def _ssd_group(gi, z_ref, xs_ref, bc_ref, sm_ref, sel_ref, cm_ref, ex_ref, cw_ref, cb_ref, dtb_ref,
               alog_ref, de_ref, nw_ref, y_ref, ht_scr, cx_scr, cbc_scr, *, nb, c):
    R = nb * c
    sq = slice(gi * nb, (gi + 1) * nb)
    cw = cw_ref[...]
    cbias = cb_ref[...]
    xraw = xs_ref[gi].reshape(nb, c, SSD_WIDTH)
    bcraw = bc_ref[gi].reshape(nb, c, SSD_BC)
    xs = _silu(_conv_taps(xraw, cx_scr[sq], cw[:, 0:SSD_WIDTH], cbias[:, 0:SSD_WIDTH], SSD_CONV))
    bca = _silu(_conv_taps(bcraw, cbc_scr[sq], cw[:, SSD_WIDTH:SSD_CONV_CH],
                           cbias[:, SSD_WIDTH:SSD_CONV_CH], SSD_CONV))
    cx_scr[sq] = xraw[:, c - (SSD_CONV - 1):c, :]
    cbc_scr[sq] = bcraw[:, c - (SSD_CONV - 1):c, :]
    xs = xs.reshape(R, SSD_WIDTH)
    bca = bca.reshape(R, SSD_BC)
    Bm = bca[:, 0:SSD_GROUPS * SSD_STATE]
    Cm = bca[:, SSD_GROUPS * SSD_STATE:SSD_BC]

    dt = _softplus(sm_ref[gi] + dtb_ref[...])
    la = dt * (-jnp.exp(alog_ref[...]))
    cs = _dot_sel(sel_ref[...], la)
    cum = cs[0:R]
    stack = jnp.concatenate([cs, dt], axis=0)
    st_e = _dot(jnp.concatenate(_split3(stack), axis=1), ex_ref[...])
    cum_e = st_e[0:R]
    lmc_e = st_e[R:2 * R]
    dt_e = st_e[2 * R:3 * R]
    ecum_e = jnp.exp(cum_e)
    xdt = xs * dt_e
    xw = (xdt * jnp.exp(lmc_e)).astype(BF16)
    xdtb = xdt.astype(BF16)
    Bb = Bm.astype(BF16)
    Cb = Cm.astype(BF16)

    cumT2 = jnp.transpose(jnp.concatenate([cum, cum], axis=0))
    cmask2 = cm_ref[...] > 0.5
    rows = lax.broadcasted_iota(jnp.int32, (R, 1), 0)
    lane = lax.broadcasted_iota(jnp.int32, (1, 2 * R), 1)
    first = lane < R
    lo_half = lax.broadcasted_iota(jnp.int32, (1, 128), 1) < SSD_HEADDIM

    hpg = SSD_HEADS // SSD_GROUPS
    for hg in range(SSD_GROUPS):
        ns = slice(hg * SSD_STATE, (hg + 1) * SSD_STATE)
        gs = slice(hg * SSD_GROUP_WIDTH, (hg + 1) * SSD_GROUP_WIDTH)
        bgrp = Bb[:, ns]
        cb2 = _dot_nt(Cb[:, ns], jnp.concatenate([bgrp, bgrp], axis=0))
        yg = []
        for j in range(hpg // 2):
            ls = slice(hg * SSD_GROUP_WIDTH + j * 128, hg * SSD_GROUP_WIDTH + (j + 1) * 128)
            h0 = DTR_OFF + hg * hpg + 2 * j
            col = jnp.where(first, cum[:, h0:h0 + 1], cum[:, h0 + 1:h0 + 2])
            row = jnp.where(first, cumT2[h0:h0 + 1, :], cumT2[h0 + 1:h0 + 2, :])
            m2 = cb2 * jnp.exp(jnp.where(cmask2, col - row, -jnp.inf))
            xpair = xdt[:, ls]
            x2 = jnp.concatenate([jnp.where(lo_half, xpair, 0.0).astype(BF16),
                                  jnp.where(lo_half, 0.0, xpair).astype(BF16)], axis=0)
            yg.append(_dot(m2.astype(BF16), x2))
        y_intra = jnp.concatenate(yg, axis=1)
        y_inter = jnp.zeros((R, SSD_GROUP_WIDTH), F32)
        for s in range(nb):
            if nb > 1:
                rm = jnp.logical_and(rows >= s * c, rows < (s + 1) * c).astype(F32)
                cg = (Cm[:, ns] * rm).astype(BF16)
                bg_ = (Bm[:, ns] * rm).astype(BF16)
            else:
                cg = Cb[:, ns]
                bg_ = Bb[:, ns]
            hT = ht_scr[gi * nb + s, :, gs]
            y_inter = y_inter + _dot(cg, hT.astype(BF16))
            upd = _dot_tn(bg_, xw[:, gs])
            dl = ecum_e[s * c + c - 1:s * c + c, gs]
            ht_scr[gi * nb + s, :, gs] = dl * hT + upd
        y = y_intra + y_inter * ecum_e[:, gs] + de_ref[:, gs] * xs[:, gs]
        y = y * _silu(z_ref[gi, :, gs])
        y_ref[gi, :, gs] = _rms(y, nw_ref[:, gs]).astype(BF16)


def _ssd_kernel(z_ref, xs_ref, bc_ref, sm_ref, cst_ref, h0_ref, sel_ref, cm_ref, ex_ref,
                cw_ref, cb_ref, dtb_ref, alog_ref, de_ref, nw_ref,
                y_ref, hn_ref, ht_scr, cx_scr, cbc_scr, *, ns, nb, c):
    ci = pl.program_id(1)

    @pl.when(ci == 0)
    def _():
        cx_scr[...] = cst_ref[:, :, 0:SSD_WIDTH]
        cbc_scr[...] = cst_ref[:, :, SSD_WIDTH:SSD_CONV_CH]
        for s in range(ns * nb):
            ht_scr[s] = jnp.transpose(h0_ref[s].reshape(SSD_WIDTH, SSD_STATE))

    for gi in range(ns):
        _ssd_group(gi, z_ref, xs_ref, bc_ref, sm_ref, sel_ref, cm_ref, ex_ref, cw_ref, cb_ref, dtb_ref,
                   alog_ref, de_ref, nw_ref, y_ref, ht_scr, cx_scr, cbc_scr, nb=nb, c=c)

    @pl.when(ci == pl.num_programs(1) - 1)
    def _():
        for s in range(ns * nb):
            hn_ref[s] = jnp.transpose(ht_scr[s]).reshape(SSD_HEADS, SSD_HEADDIM, SSD_STATE)


def _ssd(proj_b, cst, h0, cw, cb, dtb, alog, de, nw, *, B, L, nb, c, ns):
    R = nb * c
    G = B // nb
    Lg = L * nb
    ncl = Lg // R
    sel, _, causal = _chunk_consts(nb, c)
    assert 2 * R == 128, "two heads share one 128-lane tile"
    sel = jnp.asarray(np.tile(sel[:2 * R], (1, 3)), BF16)
    causal = jnp.asarray(np.tile(causal, (1, 2)), F32)
    exn = np.zeros((SMALL_W, SSD_WIDTH), np.float32)
    for h in range(SSD_HEADS):
        exn[DTR_OFF + h, h * SSD_HEADDIM:(h + 1) * SSD_HEADDIM] = 1.0
    ex = jnp.asarray(np.tile(exn, (3, 1)), BF16)
    M = proj_b.shape[0]
    pb = _group_view(proj_b, G)

    def rowblk(w, col):
        return pl.BlockSpec((ns, R, w), lambda bi, ci: (bi, ci, col // w))

    const2 = lambda a: pl.BlockSpec(a.shape, lambda bi, ci: (0,) * a.ndim)
    nsq = ns * nb
    hspec = pl.BlockSpec((nsq, SSD_HEADS, SSD_HEADDIM, SSD_STATE), lambda bi, ci: (bi, 0, 0, 0))
    kern = functools.partial(_ssd_kernel, ns=ns, nb=nb, c=c)
    ys, h_new = pl.pallas_call(
        kern,
        grid=(G // ns, ncl),
        in_specs=[rowblk(SSD_WIDTH, B_Z), rowblk(SSD_WIDTH, B_XS), rowblk(SSD_BC, B_BC),
                  rowblk(SMALL_W, B_SMALL),
                  pl.BlockSpec((nsq, SSD_CONV - 1, SSD_CONV_CH), lambda bi, ci: (bi, 0, 0)), hspec,
                  const2(sel), const2(causal), const2(ex),
                  const2(cw), const2(cb), const2(dtb), const2(alog), const2(de), const2(nw)],
        out_specs=[pl.BlockSpec((ns, R, SSD_WIDTH), lambda bi, ci: (bi, ci, 0)), hspec],
        out_shape=[jax.ShapeDtypeStruct((G, Lg, SSD_WIDTH), BF16),
                   jax.ShapeDtypeStruct((B, SSD_HEADS, SSD_HEADDIM, SSD_STATE), F32)],
        scratch_shapes=[pltpu.VMEM((nsq, SSD_STATE, SSD_WIDTH), F32),
                        pltpu.VMEM((nsq, SSD_CONV - 1, SSD_WIDTH), F32),
                        pltpu.VMEM((nsq, SSD_CONV - 1, SSD_BC), F32)],
        compiler_params=_cparams(("parallel", "arbitrary")),
        name="ssd",
    )(pb, pb, pb, pb, cst, h0, sel, causal, ex, cw, cb, dtb, alog, de, nw)
    return ys.reshape(M, SSD_WIDTH), h_new


def _outproj_kernel(o_ref, y_ref, x_ref, wt_ref, wb_ref, npost_ref, npre_ref, h_ref, xn_ref):
    mix = _dot(o_ref[...], wt_ref[...]) + _dot(y_ref[...], wb_ref[...])
    h = x_ref[...] + _rms(mix, npost_ref[...])
    h_ref[...] = h
    xn_ref[...] = _rms(h, npre_ref[...]).astype(BF16)


def _outproj(og, ys, x2d, w_out_bf16, npost, npre, tm):
    M, D = x2d.shape
    half = w_out_bf16.shape[0] // 2
    return pl.pallas_call(
        _outproj_kernel,
        grid=(M // tm,),
        in_specs=[pl.BlockSpec((tm, half), lambda i: (i, 0)),
                  pl.BlockSpec((tm, half), lambda i: (i, 0)),
                  pl.BlockSpec((tm, D), lambda i: (i, 0)),
                  pl.BlockSpec((half, D), lambda i: (0, 0)),
                  pl.BlockSpec((half, D), lambda i: (1, 0)),
                  pl.BlockSpec((1, D), lambda i: (0, 0)),
                  pl.BlockSpec((1, D), lambda i: (0, 0))],
        out_specs=[pl.BlockSpec((tm, D), lambda i: (i, 0)),
                   pl.BlockSpec((tm, D), lambda i: (i, 0))],
        out_shape=[jax.ShapeDtypeStruct((M, D), F32), jax.ShapeDtypeStruct((M, D), BF16)],
        compiler_params=_cparams(("parallel",)),
        name="outproj",
    )(og, ys, x2d, w_out_bf16, w_out_bf16, npost, npre)


SUBLANES = 8
FFN_HDR = SUBLANES
FFN_RBLK = 64
FFN_SPLIT = 2


def _ffn_kernel(xn_ref, h_ref, sa_ref, sb_ref, wa_ref, wb_ref, cwa_ref, cwb_ref, cba_ref, cbb_ref,
                wo_ref, npost_ref, y_ref, na_ref, nb_ref, ua_scr, ub_scr, act_scr, *carry,
                nseq, L, tps):
    i = pl.program_id(0)
    j = pl.program_id(1)
    tf = wa_ref.shape[1]
    W1 = FFN_CONV - 1
    H = FFN_HDR
    first = i % tps == 0

    if tps > 1:
        ca_scr, cb_scr = carry

        @pl.when(jnp.logical_not(first))
        def _():
            ua_scr[:, 0:H, :] = ca_scr[j]
            ub_scr[:, 0:H, :] = cb_scr[j]

    @pl.when(first)
    def _():
        ua_scr[:, H - W1:H, :] = sa_ref[...]
        ub_scr[:, H - W1:H, :] = sb_ref[...]

    @pl.when(j == 0)
    def _():
        y_ref[...] = jnp.zeros_like(y_ref)

    if nseq == 1:
        parts = [(0, 1, k * L // FFN_SPLIT, (k + 1) * L // FFN_SPLIT) for k in range(FFN_SPLIT)]
    else:
        parts = [(k * nseq // FFN_SPLIT, (k + 1) * nseq // FFN_SPLIT, 0, L) for k in range(FFN_SPLIT)]
    tile_rows = lambda q0, q1, r0, r1: slice(q0 * L + r0, (q1 - 1) * L + r1)

    for (q0, q1, r0, r1) in parts:
        rs = tile_rows(q0, q1, r0, r1)
        ua_scr[q0:q1, H + r0:H + r1, :] = _dot(xn_ref[rs, :], wa_ref[...]).reshape(q1 - q0, r1 - r0, tf)
        ub_scr[q0:q1, H + r0:H + r1, :] = _dot(xn_ref[rs, :], wb_ref[...]).reshape(q1 - q0, r1 - r0, tf)

    spread = lambda row: jnp.broadcast_to(row, (SUBLANES, tf))
    wa8 = [spread(cwa_ref[t:t + 1, :]) for t in range(FFN_CONV)]
    wb8 = [spread(cwb_ref[t:t + 1, :]) for t in range(FFN_CONV)]
    ba8 = spread(cba_ref[...])
    bb8 = spread(cbb_ref[...])
    for (q0, q1, r0, r1) in parts:
        if nseq == 1:
            blocks = [(0, 1, r, r + FFN_RBLK) for r in range(r0, r1, FFN_RBLK)]
        else:
            qb = FFN_RBLK // L
            blocks = [(q, q + qb, 0, L) for q in range(q0, q1, qb)]
        for (bq0, bq1, br0, br1) in blocks:
            def conv(u_scr, w8, b8):
                tap = lambda d: u_scr[bq0:bq1, H - d + br0:H - d + br1, :].reshape(-1, SUBLANES, tf)
                out = b8 + w8[W1] * tap(0)
                for t in range(W1):
                    out = out + w8[t] * tap(W1 - t)
                return out
            a = conv(ua_scr, wa8, ba8)
            b = conv(ub_scr, wb8, bb8)
            row0 = bq0 * L + br0
            act_scr[row0:row0 + FFN_RBLK, :] = (_silu(a) * b).reshape(FFN_RBLK, tf).astype(BF16)
        rs = tile_rows(q0, q1, r0, r1)
        y_ref[rs, :] += _dot(act_scr[rs, :], wo_ref[...])

    na_ref[...] = ua_scr[:, H + L - W1:H + L, :]
    nb_ref[...] = ub_scr[:, H + L - W1:H + L, :]
    if tps > 1:
        ca_scr[j] = ua_scr[:, L:L + H, :]
        cb_scr[j] = ub_scr[:, L:L + H, :]

    @pl.when(j == pl.num_programs(1) - 1)
    def _():
        y_ref[...] = h_ref[...] + _rms(y_ref[...], npost_ref[...])


def _ffn(xn2, h2d, st, w_in_bf16, cw, cb, w_out_bf16, npost, *, nseq, L, tps, tf):
    M, D = h2d.shape
    F = w_out_bf16.shape[0]
    nj = F // tf
    tm = nseq * L
    W1 = FFN_CONV - 1
    assert (L if nseq == 1 else nseq * L) % (FFN_SPLIT * FFN_RBLK) == 0
    kern = functools.partial(_ffn_kernel, nseq=nseq, L=L, tps=tps)
    stspec_a = pl.BlockSpec((nseq, W1, tf), lambda i, j: (i // tps, 0, j))
    stspec_b = pl.BlockSpec((nseq, W1, tf), lambda i, j: (i // tps, 0, j + nj))
    return pl.pallas_call(
        kern,
        grid=(M // tm, nj),
        in_specs=[pl.BlockSpec((tm, D), lambda i, j: (i, 0)),
                  pl.BlockSpec((tm, D), lambda i, j: (i, 0)),
                  stspec_a, stspec_b,
                  pl.BlockSpec((D, tf), lambda i, j: (0, j)),
                  pl.BlockSpec((D, tf), lambda i, j: (0, j + nj)),
                  pl.BlockSpec((FFN_CONV, tf), lambda i, j: (0, j)),
                  pl.BlockSpec((FFN_CONV, tf), lambda i, j: (0, j + nj)),
                  pl.BlockSpec((1, tf), lambda i, j: (0, j)),
                  pl.BlockSpec((1, tf), lambda i, j: (0, j + nj)),
                  pl.BlockSpec((tf, D), lambda i, j: (j, 0)),
                  pl.BlockSpec((1, D), lambda i, j: (0, 0))],
        out_specs=[pl.BlockSpec((tm, D), lambda i, j: (i, 0)),
                   pl.BlockSpec((nseq, W1, tf), lambda i, j: (i, 0, j)),
                   pl.BlockSpec((nseq, W1, tf), lambda i, j: (i, 0, j))],
        out_shape=[jax.ShapeDtypeStruct((M, D), F32),
                   jax.ShapeDtypeStruct((M // L, W1, F), F32),
                   jax.ShapeDtypeStruct((M // L, W1, F), F32)],
        scratch_shapes=[pltpu.VMEM((nseq, FFN_HDR + L, tf), F32),
                        pltpu.VMEM((nseq, FFN_HDR + L, tf), F32),
                        pltpu.VMEM((tm, tf), BF16)] + (
                            [pltpu.VMEM((nj, nseq, FFN_HDR, tf), F32)] * 2 if tps > 1 else []),
        compiler_params=_cparams(("arbitrary", "arbitrary")),
        name="ffn",
    )(xn2, h2d, st, st, w_in_bf16, w_in_bf16, cw, cw, cb, cb, w_out_bf16, npost)


def _layer(x, s_gla, s_ssd, s_conv, s_ffn, p, *, nb, c, ns, ffn_nseq, ffn_L, ffn_tps):
    B, L, D = x.shape
    M = B * L
    x2d = x.reshape(M, D)
    proj_a = _inproj(x2d, p['n_mix_pre'], p['w_in_a'], A_W, 512)
    proj_b = _inproj(x2d, p['n_mix_pre'], p['w_in_b'], B_W, 512)
    og, g_new = _gla(proj_a, proj_b, s_gla, p['w2cat'], p['bg'], p['gn'], B=B, L=L, nb=nb, c=c, ns=ns)
    ys, h_new = _ssd(proj_b, s_conv, s_ssd, p['ssd_cw'], p['ssd_cb'], p['dtb'], p['alog'], p['de'],
                     p['ssd_nw'], B=B, L=L, nb=nb, c=c, ns=ns)
    hres, xn2 = _outproj(og, ys, x2d, p['w_out'], p['n_mix_post'], p['n_ffn_pre'], 512)
    y, fa, fb = _ffn(xn2, hres, s_ffn, p['ffn_w_in'], p['ffn_cw'], p['ffn_cb'], p['ffn_w_out'],
                     p['n_ffn_post'], nseq=ffn_nseq, L=ffn_L, tps=ffn_tps, tf=512)
    c_new = proj_b.reshape(B, L, -1)[:, L - (SSD_CONV - 1):, B_XS:B_XS + SSD_CONV_CH]
    f_new = jnp.concatenate([fa, fb], axis=-1)[ffn_tps - 1::ffn_tps]
    return y.reshape(B, L, D), g_new, h_new, c_new, f_new


def _prep_params(l, norm_mix_pre, norm_mix_post, norm_ffn_pre, norm_ffn_post, w_in, gla_w_gate2,
                 gla_b_gate, gla_norm, ssd_conv_w, ssd_conv_b, ssd_dt_bias, ssd_A_log, ssd_D, ssd_norm,
                 w_out, ffn_w_in, ffn_conv_w, ffn_conv_b, ffn_w_out):
    D = w_in.shape[1]
    sizes = (GLA_KDIM, GLA_KDIM, GLA_WIDTH, GLA_WIDTH, GLA_LOWRANK, SSD_WIDTH, SSD_CONV_CH, SSD_HEADS)
    offs = np.cumsum((0,) + sizes)
    wb16 = w_in[l].astype(BF16)
    npad = SMALL_W - GLA_LOWRANK - SSD_HEADS
    w_in_b = jnp.concatenate([wb16[:, offs[5]:offs[7]], wb16[:, offs[4]:offs[5]], wb16[:, offs[7]:offs[8]],
                              jnp.zeros((D, npad), BF16)], axis=1)
    w2p = jnp.zeros((SMALL_W, GLA_KDIM), F32).at[0:GLA_LOWRANK].set(gla_w_gate2[l])
    w2_hi = w2p.astype(BF16)
    w2_lo = (w2p - w2_hi.astype(F32)).astype(BF16)
    w2cat = jnp.concatenate([w2_hi, w2_hi, w2_lo], axis=0)
    pad_small = lambda v: jnp.zeros((1, SMALL_W), F32).at[0, DTR_OFF:DTR_OFF + SSD_HEADS].set(v)
    row = lambda v: v.reshape(1, -1)
    return dict(
        n_mix_pre=row(norm_mix_pre[l]), n_mix_post=row(norm_mix_post[l]),
        n_ffn_pre=row(norm_ffn_pre[l]), n_ffn_post=row(norm_ffn_post[l]),
        w_in_a=wb16, w_in_b=w_in_b, w2cat=w2cat, bg=row(gla_b_gate[l]), gn=row(gla_norm[l]),
        ssd_cw=ssd_conv_w[l], ssd_cb=row(ssd_conv_b[l]),
        dtb=pad_small(ssd_dt_bias[l]), alog=pad_small(ssd_A_log[l]),
        de=row(jnp.repeat(ssd_D[l], SSD_HEADDIM)), ssd_nw=row(ssd_norm[l]),
        w_out=w_out[l].astype(BF16), ffn_w_in=ffn_w_in[l].astype(BF16),
        ffn_cw=ffn_conv_w[l], ffn_cb=row(ffn_conv_b[l]), ffn_w_out=ffn_w_out[l].astype(BF16))


def kernel(x_prompt, x_sample, state_gla, state_ssd, state_ssd_conv, state_ffn_conv, norm_mix_pre,
           norm_mix_post, norm_ffn_pre, norm_ffn_post, w_in, gla_w_gate2, gla_b_gate, gla_norm,
           ssd_conv_w, ssd_conv_b, ssd_dt_bias, ssd_A_log, ssd_D, ssd_norm, w_out, ffn_w_in,
           ffn_conv_w, ffn_conv_b, ffn_w_out):
    depth = w_in.shape[0]
    xp, xs = x_prompt, x_sample
    Bp, Lp, D = xp.shape
    Bs, Ls, _ = xs.shape
    F2 = ffn_w_in.shape[2]
    outs = [[] for _ in range(8)]
    for l in range(depth):
        p = _prep_params(l, norm_mix_pre, norm_mix_post, norm_ffn_pre, norm_ffn_post, w_in,
                         gla_w_gate2, gla_b_gate, gla_norm, ssd_conv_w, ssd_conv_b, ssd_dt_bias,
                         ssd_A_log, ssd_D, ssd_norm, w_out, ffn_w_in, ffn_conv_w, ffn_conv_b, ffn_w_out)
        cp = CHUNK
        xp, g1, h1, c1, f1 = _layer(
            xp,
            jnp.zeros((Bp, GLA_HEADS, GLA_HEAD_K, GLA_HEAD_V), F32),
            jnp.zeros((Bp, SSD_HEADS, SSD_HEADDIM, SSD_STATE), F32),
            jnp.zeros((Bp, SSD_CONV - 1, SSD_CONV_CH), F32),
            jnp.zeros((Bp, FFN_CONV - 1, F2), F32),
            p, nb=1, c=cp, ns=Bp, ffn_nseq=1, ffn_L=512, ffn_tps=Lp // 512)
        nbs = CHUNK // Ls
        xs, g2, h2, c2, f2 = _layer(
            xs, state_gla[l], state_ssd[l], state_ssd_conv[l], state_ffn_conv[l],
            p, nb=nbs, c=Ls, ns=2, ffn_nseq=512 // Ls, ffn_L=Ls, ffn_tps=1)
        for lst, val in zip(outs, (g1, h1, c1, f1, g2, h2, c2, f2)):
            lst.append(val)
    return (xp, xs) + tuple(jnp.stack(o) for o in outs)
```

```python
import functools

import numpy as np
import jax
import jax.numpy as jnp
from jax import lax
from jax.experimental import pallas as pl
from jax.experimental.pallas import tpu as pltpu

F32 = jnp.float32
BF16 = jnp.bfloat16
EPS = 1e-6

GLA_HEADS = 4
GLA_HEAD_K = 128
GLA_HEAD_V = 256
GLA_KDIM = GLA_HEADS * GLA_HEAD_K
GLA_WIDTH = GLA_HEADS * GLA_HEAD_V
GLA_LOWRANK = 16
GLA_GATE_NORMALIZER = 16.0
SSD_HEADS = 16
SSD_HEADDIM = 64
SSD_STATE = 128
SSD_GROUPS = 2
SSD_WIDTH = SSD_HEADS * SSD_HEADDIM
SSD_GROUP_WIDTH = SSD_WIDTH // SSD_GROUPS
SSD_CONV = 4
SSD_BC = 2 * SSD_GROUPS * SSD_STATE
SSD_CONV_CH = SSD_WIDTH + SSD_BC
FFN_CONV = 3
CHUNK = 64

A_Q = 0
A_K = GLA_KDIM
A_V = 2 * GLA_KDIM
A_G = A_V + GLA_WIDTH
A_W = A_G + GLA_WIDTH
B_Z = 0
B_XS = SSD_WIDTH
B_BC = B_XS + SSD_WIDTH
B_SMALL = B_BC + SSD_BC
SMALL_W = 128
DTR_OFF = GLA_LOWRANK
B_W = B_SMALL + SMALL_W

SUBLANES = 8
VMEM_LIMIT = 56 * 1024 * 1024


def _cparams(sem):
    return pltpu.CompilerParams(dimension_semantics=sem, vmem_limit_bytes=VMEM_LIMIT)


def _split3(x):
    hi = x.astype(BF16)
    r = x - hi.astype(F32)
    mid = r.astype(BF16)
    lo = (r - mid.astype(F32)).astype(BF16)
    return hi, mid, lo


def _dot(a, b):
    return jnp.dot(a, b, preferred_element_type=F32)


def _dot_nt(a, b):
    return lax.dot_general(a, b, (((1,), (1,)), ((), ())), preferred_element_type=F32)


def _dot_tn(a, b):
    return lax.dot_general(a, b, (((0,), (0,)), ((), ())), preferred_element_type=F32)


def _dot_sel(sel3_bf16, x_f32):
    return _dot(sel3_bf16, jnp.concatenate(_split3(x_f32), axis=0))


NEG_LOG2E = -1.4426950408889634


def _silu(x):
    return x / (1.0 + jnp.exp2(x * NEG_LOG2E))


def _softplus(x):
    return jnp.maximum(x, 0.0) + jnp.log1p(jnp.exp(-jnp.abs(x)))


def _rms(x, w):
    return x * lax.rsqrt(jnp.mean(x * x, axis=-1, keepdims=True) + EPS) * w


def _chunk_consts(nb, c):
    R = nb * c
    idx = np.arange(R)
    seq, pos = idx // c, idx % c
    same = seq[:, None] == seq[None, :]
    t, u = pos[:, None], pos[None, :]
    blocks = [same & (u <= t), same & (u > t)]
    masks = [np.eye(R, dtype=bool)]
    m = c // 2
    while m >= 1:
        blk = pos // (2 * m)
        rho = blk * 2 * m + m - 1
        upper = pos > rho
        a_up = upper[:, None] & (u > rho[:, None]) & (u <= t)
        a_lo = (~upper)[:, None] & (u > t) & (u <= rho[:, None])
        blocks.append(same & (a_up | a_lo))
        masks.append(same & upper[:, None] & (~upper)[None, :] & (blk[:, None] == blk[None, :]))
        m //= 2
    sel = np.concatenate(blocks, 0).astype(np.float32)
    lvl = np.stack(masks).astype(np.float32)
    causal = (same & (u <= t)).astype(np.float32)
    return sel, lvl, causal


def _inproj_kernel(x_ref, nw_ref, w_ref, o_ref):
    xn = _rms(x_ref[...], nw_ref[...]).astype(BF16)
    o_ref[...] = _dot(xn, w_ref[...])


def _inproj(x2d, nw, w_bf16, n_out, tm):
    M, D = x2d.shape
    return pl.pallas_call(
        _inproj_kernel,
        grid=(M // tm,),
        in_specs=[pl.BlockSpec((tm, D), lambda i: (i, 0)),
                  pl.BlockSpec((1, D), lambda i: (0, 0)),
                  pl.BlockSpec((D, n_out), lambda i: (0, 0))],
        out_specs=pl.BlockSpec((tm, n_out), lambda i: (i, 0)),
        out_shape=jax.ShapeDtypeStruct((M, n_out), F32),
        compiler_params=_cparams(("parallel",)),
        name="inproj",
    )(x2d, nw, w_bf16)


def _gla_group(gi, q_ref, k_ref, v_ref, g_ref, sm_ref, s_in, s_out, sel_ref, lvl_ref, w2_ref, bg_ref,
               gn_ref, o_ref, *, nb, c, nlev):
    R = nb * c
    sm = sm_ref[gi]
    sm_hi = sm.astype(BF16)
    sm_lo = (sm - sm_hi.astype(F32)).astype(BF16)
    zg = _dot(jnp.concatenate([sm_hi, sm_lo, sm_hi], axis=1), w2_ref[...]) + bg_ref[...]
    lg2 = -_softplus(-zg) * (-NEG_LOG2E / GLA_GATE_NORMALIZER)
    E = _dot_sel(sel_ref[...], lg2)
    eb = jnp.exp2(E[0:R])
    q = q_ref[gi] * (GLA_HEAD_K ** -0.5)
    k = k_ref[gi]
    qe = q * eb
    ke = k * jnp.exp2(E[R:2 * R])
    ql, kl = [q.astype(BF16)], [k.astype(BF16)]
    for l in range(nlev):
        p = jnp.exp2(E[(2 + l) * R:(3 + l) * R])
        ql.append((q * p).astype(BF16))
        kl.append((k * p).astype(BF16))
    vb = v_ref[gi].astype(BF16)
    g = g_ref[gi]
    gn = gn_ref[...]
    rows = lax.broadcasted_iota(jnp.int32, (R, 1), 0)

    zk = jnp.zeros((R, GLA_HEAD_K), BF16)
    zv = jnp.zeros((R, GLA_HEAD_V), BF16)
    o_intra = []
    for hp in range(GLA_HEADS // 2):
        k0 = slice(2 * hp * GLA_HEAD_K, (2 * hp + 1) * GLA_HEAD_K)
        k1 = slice((2 * hp + 1) * GLA_HEAD_K, (2 * hp + 2) * GLA_HEAD_K)
        k01 = slice(2 * hp * GLA_HEAD_K, (2 * hp + 2) * GLA_HEAD_K)
        att2 = jnp.zeros((R, 2 * R), F32)
        for l in range(nlev + 1):
            kbd = jnp.concatenate([jnp.concatenate([kl[l][:, k0], zk], axis=1),
                                   jnp.concatenate([zk, kl[l][:, k1]], axis=1)], axis=0)
            att2 = att2 + _dot_nt(ql[l][:, k01], kbd) * lvl_ref[l]
        v0 = vb[:, 2 * hp * GLA_HEAD_V:(2 * hp + 1) * GLA_HEAD_V]
        v1 = vb[:, (2 * hp + 1) * GLA_HEAD_V:(2 * hp + 2) * GLA_HEAD_V]
        vbd = jnp.concatenate([jnp.concatenate([v0, zv], axis=1),
                               jnp.concatenate([zv, v1], axis=1)], axis=0)
        o2 = _dot(att2.astype(BF16), vbd)
        o_intra += [o2[:, 0:GLA_HEAD_V], o2[:, GLA_HEAD_V:2 * GLA_HEAD_V]]

    for h in range(GLA_HEADS):
        ks = slice(h * GLA_HEAD_K, (h + 1) * GLA_HEAD_K)
        vs = slice(h * GLA_HEAD_V, (h + 1) * GLA_HEAD_V)
        o = o_intra[h]
        for s in range(nb):
            if nb > 1:
                rm = jnp.logical_and(rows >= s * c, rows < (s + 1) * c).astype(F32)
                qs = (qe[:, ks] * rm).astype(BF16)
                kd = (ke[:, ks] * rm).astype(BF16)
            else:
                qs = qe[:, ks].astype(BF16)
                kd = ke[:, ks].astype(BF16)
            S = s_in[gi * nb + s, h]
            o = o + _dot(qs, S.astype(BF16))
            upd = _dot_tn(kd, vb[:, vs])
            d = eb[s * c + c - 1:s * c + c, ks]
            dcol = jnp.transpose(jnp.broadcast_to(d, (GLA_HEAD_K, GLA_HEAD_K)))
            s_out[gi * nb + s, h] = jnp.concatenate([dcol, dcol], axis=1) * S + upd
        gh = g[:, vs]
        o_ref[gi, :, vs] = (_rms(o, gn) * _silu(gh)).astype(BF16)


def _conv_taps(u8, prev8, w, bias, width, chained):
    sub = lax.broadcasted_iota(jnp.int32, (1, SUBLANES, 1), 1)
    acc = bias + w[width - 1:width] * u8
    for j in range(width - 1):
        d = width - 1 - j
        r = pltpu.roll(u8, d, axis=1)
        pr = pltpu.roll(prev8, d, axis=1)
        if chained:
            pr = jnp.concatenate([pr, r[:-1]], axis=0)
        acc = acc + w[j:j + 1] * jnp.where(sub < d, pr, r)
    return acc


def _ssd_group(gi, z_ref, xs_ref, bc_ref, sm_ref, sel_ref, cm_ref, ex_ref, cw_ref, cb_ref, dtb_ref,
               alog_ref, de_ref, nw_ref, y_ref, ht_scr, cx_scr, cbc_scr, *, nb, c, out_lane0=0):
    R = nb * c
    sq = slice(gi * nb, (gi + 1) * nb)
    cw = cw_ref[...]
    cbias = cb_ref[...]
    nt = R // SUBLANES
    chained = nb == 1
    xraw = xs_ref[gi].reshape(nt, SUBLANES, SSD_WIDTH)
    bcraw = bc_ref[gi].reshape(nt, SUBLANES, SSD_BC)
    xs = _silu(_conv_taps(xraw, cx_scr[sq], cw[:, 0:SSD_WIDTH], cbias[:, 0:SSD_WIDTH], SSD_CONV, chained))
    bca = _silu(_conv_taps(bcraw, cbc_scr[sq], cw[:, SSD_WIDTH:SSD_CONV_CH],
                           cbias[:, SSD_WIDTH:SSD_CONV_CH], SSD_CONV, chained))
    cx_scr[sq] = xraw[nt - nb:nt]
    cbc_scr[sq] = bcraw[nt - nb:nt]
    xs = xs.reshape(R, SSD_WIDTH)
    bca = bca.reshape(R, SSD_BC)
    Bm = bca[:, 0:SSD_GROUPS * SSD_STATE]
    Cm = bca[:, SSD_GROUPS * SSD_STATE:SSD_BC]

    dt = _softplus(sm_ref[gi] + dtb_ref[...])
    la2 = dt * (NEG_LOG2E * jnp.exp(alog_ref[...]))
    cs = _dot_sel(sel_ref[0:2 * R], la2)
    cum = cs[0:R]
    stack = jnp.concatenate([cs, dt], axis=0)
    st_e = _dot(jnp.concatenate(_split3(stack), axis=1), ex_ref[...])
    cum_e = st_e[0:R]
    lmc_e = st_e[R:2 * R]
    dt_e = st_e[2 * R:3 * R]
    ecum_e = jnp.exp2(cum_e)
    xdt = xs * dt_e
    xw = (xdt * jnp.exp2(lmc_e)).astype(BF16)
    Bb = Bm.astype(BF16)
    Cb = Cm.astype(BF16)

    cumT2 = jnp.transpose(jnp.concatenate([cum, cum], axis=0))
    cmask2 = cm_ref[...] > 0.5
    rows = lax.broadcasted_iota(jnp.int32, (R, 1), 0)
    lane = lax.broadcasted_iota(jnp.int32, (1, 2 * R), 1)
    first = lane < R
    lo_half = lax.broadcasted_iota(jnp.int32, (1, 128), 1) < SSD_HEADDIM

    hpg = SSD_HEADS // SSD_GROUPS
    for hg in range(SSD_GROUPS):
        ns = slice(hg * SSD_STATE, (hg + 1) * SSD_STATE)
        gs = slice(hg * SSD_GROUP_WIDTH, (hg + 1) * SSD_GROUP_WIDTH)
        bgrp = Bb[:, ns]
        cb2 = _dot_nt(Cb[:, ns], jnp.concatenate([bgrp, bgrp], axis=0))
        yg = []
        for j in range(hpg // 2):
            ls = slice(hg * SSD_GROUP_WIDTH + j * 128, hg * SSD_GROUP_WIDTH + (j + 1) * 128)
            h0 = DTR_OFF + hg * hpg + 2 * j
            col = jnp.where(first, cum[:, h0:h0 + 1], cum[:, h0 + 1:h0 + 2])
            row = jnp.where(first, cumT2[h0:h0 + 1, :], cumT2[h0 + 1:h0 + 2, :])
            m2 = cb2 * jnp.exp2(jnp.where(cmask2, col - row, -jnp.inf))
            xpair = xdt[:, ls]
            x2 = jnp.concatenate([jnp.where(lo_half, xpair, 0.0).astype(BF16),
                                  jnp.where(lo_half, 0.0, xpair).astype(BF16)], axis=0)
            yg.append(_dot(m2.astype(BF16), x2))
        y_intra = jnp.concatenate(yg, axis=1)
        y_inter = jnp.zeros((R, SSD_GROUP_WIDTH), F32)
        for s in range(nb):
            if nb > 1:
                rm = jnp.logical_and(rows >= s * c, rows < (s + 1) * c).astype(F32)
                cg = (Cm[:, ns] * rm).astype(BF16)
                bg_ = (Bm[:, ns] * rm).astype(BF16)
            else:
                cg = Cb[:, ns]
                bg_ = Bb[:, ns]
            hT = ht_scr[gi * nb + s, :, gs]
            y_inter = y_inter + _dot(cg, hT.astype(BF16))
            upd = _dot_tn(bg_, xw[:, gs])
            dl = ecum_e[s * c + c - 1:s * c + c, gs]
            ht_scr[gi * nb + s, :, gs] = dl * hT + upd
        y = y_intra + y_inter * ecum_e[:, gs] + de_ref[:, gs] * xs[:, gs]
        y = y * _silu(z_ref[gi, :, gs])
        os_ = slice(out_lane0 + hg * SSD_GROUP_WIDTH, out_lane0 + (hg + 1) * SSD_GROUP_WIDTH)
        y_ref[gi, :, os_] = _rms(y, nw_ref[:, gs]).astype(BF16)


def _group_view(a, G):
    return a.reshape(G, a.shape[0] // G, a.shape[1])


def _mixer_kernel(q_ref, k_ref, v_ref, g_ref, z_ref, xs_ref, bc_ref, sm_ref, s0_ref, cst_ref, h0_ref,
                  sel_ref, lvl_ref, w2_ref, bg_ref, gn_ref, cm_ref, ex_ref, cw_ref, cb_ref, dtb_ref,
                  alog_ref, de_ref, nw_ref, mix_ref, sn_ref, hn_ref, *scratch, ns, nb, c, nlev):
    ci = pl.program_id(1)
    if len(scratch) == 4:
        s_scr, ht_scr, cx_scr, cbc_scr = scratch
        s_in = s_out = s_scr
    else:
        ht_scr, cx_scr, cbc_scr = scratch
        s_scr, s_in, s_out = None, s0_ref, sn_ref

    @pl.when(ci == 0)
    def _():
        if s_scr is not None:
            s_scr[...] = s0_ref[...]
        cx_scr[...] = jnp.zeros_like(cx_scr)
        cbc_scr[...] = jnp.zeros_like(cbc_scr)
        cx_scr[:, SUBLANES - (SSD_CONV - 1):SUBLANES, :] = cst_ref[:, :, 0:SSD_WIDTH]
        cbc_scr[:, SUBLANES - (SSD_CONV - 1):SUBLANES, :] = cst_ref[:, :, SSD_WIDTH:SSD_CONV_CH]
        for s in range(ns * nb):
            ht_scr[s] = jnp.transpose(h0_ref[s].reshape(SSD_WIDTH, SSD_STATE))

    for gi in range(ns):
        _gla_group(gi, q_ref, k_ref, v_ref, g_ref, sm_ref, s_in, s_out, sel_ref, lvl_ref, w2_ref, bg_ref,
                   gn_ref, mix_ref, nb=nb, c=c, nlev=nlev)
        _ssd_group(gi, z_ref, xs_ref, bc_ref, sm_ref, sel_ref, cm_ref, ex_ref, cw_ref, cb_ref, dtb_ref,
                   alog_ref, de_ref, nw_ref, mix_ref, ht_scr, cx_scr, cbc_scr, nb=nb, c=c,
                   out_lane0=GLA_WIDTH)

    @pl.when(ci == pl.num_programs(1) - 1)
    def _():
        if s_scr is not None:
            sn_ref[...] = s_scr[...]
        for s in range(ns * nb):
            hn_ref[s] = jnp.transpose(ht_scr[s]).reshape(SSD_HEADS, SSD_HEADDIM, SSD_STATE)


def _mixer(proj_a, proj_b, s_gla, s_conv, s_ssd, p, *, B, L, nb, c, ns):
    R = nb * c
    G = B // nb
    Lg = L * nb
    ncl = Lg // R
    nlev = int(np.log2(c))
    assert 2 * R == 128, "two heads share one 128-lane tile"
    assert c % SUBLANES == 0 and (nb == 1 or c == SUBLANES), "conv carry is one sublane tile per sequence"
    sel, lvl, causal = _chunk_consts(nb, c)
    sel = jnp.asarray(np.tile(sel, (1, 3)), BF16)
    lvl = jnp.asarray(np.tile(lvl, (1, 1, 2)), F32)
    causal = jnp.asarray(np.tile(causal, (1, 2)), F32)
    exn = np.zeros((SMALL_W, SSD_WIDTH), np.float32)
    for h in range(SSD_HEADS):
        exn[DTR_OFF + h, h * SSD_HEADDIM:(h + 1) * SSD_HEADDIM] = 1.0
    ex = jnp.asarray(np.tile(exn, (3, 1)), BF16)
    M = proj_a.shape[0]
    pa, pb = _group_view(proj_a, G), _group_view(proj_b, G)

    def rowblk(w, col):
        return pl.BlockSpec((ns, R, w), lambda bi, ci: (bi, ci, col // w))

    const2 = lambda a: pl.BlockSpec(a.shape, lambda bi, ci: (0,) * a.ndim)
    nsq = ns * nb
    sshape = (nsq, GLA_HEADS, GLA_HEAD_K, GLA_HEAD_V)
    sspec = pl.BlockSpec(sshape, lambda bi, ci: (bi, 0, 0, 0))
    hspec = pl.BlockSpec((nsq, SSD_HEADS, SSD_HEADDIM, SSD_STATE), lambda bi, ci: (bi, 0, 0, 0))
    consts = (sel, lvl, p['w2cat'], p['bg'], p['gn'], causal, ex, p['ssd_cw'], p['ssd_cb'], p['dtb'],
              p['alog'], p['de'], p['ssd_nw'])
    kern = functools.partial(_mixer_kernel, ns=ns, nb=nb, c=c, nlev=nlev)
    mix, s_new, h_new = pl.pallas_call(
        kern,
        grid=(G // ns, ncl),
        in_specs=[rowblk(GLA_KDIM, A_Q), rowblk(GLA_KDIM, A_K), rowblk(GLA_WIDTH, A_V),
                  rowblk(GLA_WIDTH, A_G), rowblk(SSD_WIDTH, B_Z), rowblk(SSD_WIDTH, B_XS),
                  rowblk(SSD_BC, B_BC), rowblk(SMALL_W, B_SMALL), sspec,
                  pl.BlockSpec((nsq, SSD_CONV - 1, SSD_CONV_CH), lambda bi, ci: (bi, 0, 0)), hspec]
                 + [const2(a) for a in consts],
        out_specs=[pl.BlockSpec((ns, R, GLA_WIDTH + SSD_WIDTH), lambda bi, ci: (bi, ci, 0)), sspec, hspec],
        out_shape=[jax.ShapeDtypeStruct((G, Lg, GLA_WIDTH + SSD_WIDTH), BF16),
                   jax.ShapeDtypeStruct((B, GLA_HEADS, GLA_HEAD_K, GLA_HEAD_V), F32),
                   jax.ShapeDtypeStruct((B, SSD_HEADS, SSD_HEADDIM, SSD_STATE), F32)],
        scratch_shapes=([pltpu.VMEM(sshape, F32)] if ncl > 1 else []) + [
            pltpu.VMEM((nsq, SSD_STATE, SSD_WIDTH), F32),
            pltpu.VMEM((nsq, SUBLANES, SSD_WIDTH), F32),
            pltpu.VMEM((nsq, SUBLANES, SSD_BC), F32)],
        compiler_params=_cparams(("parallel", "arbitrary")),
        name="mixer",
    )(pa, pa, pa, pa, pb, pb, pb, pb, s_gla, s_conv, s_ssd, *consts)
    return mix.reshape(M, GLA_WIDTH + SSD_WIDTH), s_new, h_new


def _outproj_kernel(m_ref, x_ref, w_ref, npost_ref, npre_ref, h_ref, xn_ref):
    h = x_ref[...] + _rms(_dot(m_ref[...], w_ref[...]), npost_ref[...])
    h_ref[...] = h
    xn_ref[...] = _rms(h, npre_ref[...]).astype(BF16)


def _outproj(mix, x2d, w_out_bf16, npost, npre, tm):
    M, D = x2d.shape
    W = w_out_bf16.shape[0]
    return pl.pallas_call(
        _outproj_kernel,
        grid=(M // tm,),
        in_specs=[pl.BlockSpec((tm, W), lambda i: (i, 0)),
                  pl.BlockSpec((tm, D), lambda i: (i, 0)),
                  pl.BlockSpec((W, D), lambda i: (0, 0)),
                  pl.BlockSpec((1, D), lambda i: (0, 0)),
                  pl.BlockSpec((1, D), lambda i: (0, 0))],
        out_specs=[pl.BlockSpec((tm, D), lambda i: (i, 0)),
                   pl.BlockSpec((tm, D), lambda i: (i, 0))],
        out_shape=[jax.ShapeDtypeStruct((M, D), F32), jax.ShapeDtypeStruct((M, D), BF16)],
        compiler_params=_cparams(("parallel",)),
        name="outproj",
    )(mix, x2d, w_out_bf16, npost, npre)


FFN_HDR = SUBLANES
FFN_RBLK = 64
FFN_SPLIT = 2


def _ffn_kernel(xn_ref, h_ref, sa_ref, sb_ref, wa_ref, wb_ref, cwa_ref, cwb_ref, cba_ref, cbb_ref,
                wo_ref, npost_ref, y_ref, na_ref, nb_ref, ua_scr, ub_scr, act_scr, *carry,
                nseq, L, tps):
    i = pl.program_id(0)
    j = pl.program_id(1)
    tf = wa_ref.shape[1]
    W1 = FFN_CONV - 1
    H = FFN_HDR
    first = i % tps == 0

    if tps > 1:
        ca_scr, cb_scr = carry

        @pl.when(jnp.logical_not(first))
        def _():
            ua_scr[:, 0:H, :] = ca_scr[j]
            ub_scr[:, 0:H, :] = cb_scr[j]

    @pl.when(first)
    def _():
        ua_scr[:, H - W1:H, :] = sa_ref[...]
        ub_scr[:, H - W1:H, :] = sb_ref[...]

    @pl.when(j == 0)
    def _():
        y_ref[...] = jnp.zeros_like(y_ref)

    if nseq == 1:
        parts = [(0, 1, k * L // FFN_SPLIT, (k + 1) * L // FFN_SPLIT) for k in range(FFN_SPLIT)]
    else:
        parts = [(k * nseq // FFN_SPLIT, (k + 1) * nseq // FFN_SPLIT, 0, L) for k in range(FFN_SPLIT)]
    tile_rows = lambda q0, q1, r0, r1: slice(q0 * L + r0, (q1 - 1) * L + r1)

    for (q0, q1, r0, r1) in parts:
        rs = tile_rows(q0, q1, r0, r1)
        ua_scr[q0:q1, H + r0:H + r1, :] = _dot(xn_ref[rs, :], wa_ref[...]).reshape(q1 - q0, r1 - r0, tf)
        ub_scr[q0:q1, H + r0:H + r1, :] = _dot(xn_ref[rs, :], wb_ref[...]).reshape(q1 - q0, r1 - r0, tf)

    spread = lambda row: jnp.broadcast_to(row, (SUBLANES, tf))
    wa8 = [spread(cwa_ref[t:t + 1, :]) for t in range(FFN_CONV)]
    wb8 = [spread(cwb_ref[t:t + 1, :]) for t in range(FFN_CONV)]
    ba8 = spread(cba_ref[...])
    bb8 = spread(cbb_ref[...])
    for (q0, q1, r0, r1) in parts:
        if nseq == 1:
            blocks = [(0, 1, r, r + FFN_RBLK) for r in range(r0, r1, FFN_RBLK)]
        else:
            qb = FFN_RBLK // L
            blocks = [(q, q + qb, 0, L) for q in range(q0, q1, qb)]
        for (bq0, bq1, br0, br1) in blocks:
            def conv(u_scr, w8, b8):
                tap = lambda d: u_scr[bq0:bq1, H - d + br0:H - d + br1, :].reshape(-1, SUBLANES, tf)
                out = b8 + w8[W1] * tap(0)
                for t in range(W1):
                    out = out + w8[t] * tap(W1 - t)
                return out
            a = conv(ua_scr, wa8, ba8)
            b = conv(ub_scr, wb8, bb8)
            row0 = bq0 * L + br0
            act_scr[row0:row0 + FFN_RBLK, :] = (_silu(a) * b).reshape(FFN_RBLK, tf).astype(BF16)
        rs = tile_rows(q0, q1, r0, r1)
        y_ref[rs, :] += _dot(act_scr[rs, :], wo_ref[...])

    na_ref[...] = ua_scr[:, H + L - W1:H + L, :]
    nb_ref[...] = ub_scr[:, H + L - W1:H + L, :]
    if tps > 1:
        ca_scr[j] = ua_scr[:, L:L + H, :]
        cb_scr[j] = ub_scr[:, L:L + H, :]

    @pl.when(j == pl.num_programs(1) - 1)
    def _():
        y_ref[...] = h_ref[...] + _rms(y_ref[...], npost_ref[...])


def _ffn(xn2, h2d, st, w_in_bf16, cw, cb, w_out_bf16, npost, *, nseq, L, tps, tf):
    M, D = h2d.shape
    F = w_out_bf16.shape[0]
    nj = F // tf
    tm = nseq * L
    W1 = FFN_CONV - 1
    assert (L if nseq == 1 else nseq * L) % (FFN_SPLIT * FFN_RBLK) == 0
    kern = functools.partial(_ffn_kernel, nseq=nseq, L=L, tps=tps)
    stspec_a = pl.BlockSpec((nseq, W1, tf), lambda i, j: (i // tps, 0, j))
    stspec_b = pl.BlockSpec((nseq, W1, tf), lambda i, j: (i // tps, 0, j + nj))
    return pl.pallas_call(
        kern,
        grid=(M // tm, nj),
        in_specs=[pl.BlockSpec((tm, D), lambda i, j: (i, 0)),
                  pl.BlockSpec((tm, D), lambda i, j: (i, 0)),
                  stspec_a, stspec_b,
                  pl.BlockSpec((D, tf), lambda i, j: (0, j)),
                  pl.BlockSpec((D, tf), lambda i, j: (0, j + nj)),
                  pl.BlockSpec((FFN_CONV, tf), lambda i, j: (0, j)),
                  pl.BlockSpec((FFN_CONV, tf), lambda i, j: (0, j + nj)),
                  pl.BlockSpec((1, tf), lambda i, j: (0, j)),
                  pl.BlockSpec((1, tf), lambda i, j: (0, j + nj)),
                  pl.BlockSpec((tf, D), lambda i, j: (j, 0)),
                  pl.BlockSpec((1, D), lambda i, j: (0, 0))],
        out_specs=[pl.BlockSpec((tm, D), lambda i, j: (i, 0)),
                   pl.BlockSpec((nseq, W1, tf), lambda i, j: (i, 0, j)),
                   pl.BlockSpec((nseq, W1, tf), lambda i, j: (i, 0, j))],
        out_shape=[jax.ShapeDtypeStruct((M, D), F32),
                   jax.ShapeDtypeStruct((M // L, W1, F), F32),
                   jax.ShapeDtypeStruct((M // L, W1, F), F32)],
        scratch_shapes=[pltpu.VMEM((nseq, FFN_HDR + L, tf), F32),
                        pltpu.VMEM((nseq, FFN_HDR + L, tf), F32),
                        pltpu.VMEM((tm, tf), BF16)] + (
                            [pltpu.VMEM((nj, nseq, FFN_HDR, tf), F32)] * 2 if tps > 1 else []),
        compiler_params=_cparams(("arbitrary", "arbitrary")),
        name="ffn",
    )(xn2, h2d, st, st, w_in_bf16, w_in_bf16, cw, cw, cb, cb, w_out_bf16, npost)


def _layer(x, s_gla, s_ssd, s_conv, s_ffn, p, *, nb, c, ns, ffn_nseq, ffn_L, ffn_tps):
    B, L, D = x.shape
    M = B * L
    x2d = x.reshape(M, D)
    proj_a = _inproj(x2d, p['n_mix_pre'], p['w_in_a'], A_W, 512)
    proj_b = _inproj(x2d, p['n_mix_pre'], p['w_in_b'], B_W, 512)
    mix, g_new, h_new = _mixer(proj_a, proj_b, s_gla, s_conv, s_ssd, p, B=B, L=L, nb=nb, c=c, ns=ns)
    hres, xn2 = _outproj(mix, x2d, p['w_out'], p['n_mix_post'], p['n_ffn_pre'], 512)
    y, fa, fb = _ffn(xn2, hres, s_ffn, p['ffn_w_in'], p['ffn_cw'], p['ffn_cb'], p['ffn_w_out'],
                     p['n_ffn_post'], nseq=ffn_nseq, L=ffn_L, tps=ffn_tps, tf=512)
    c_new = proj_b.reshape(B, L, -1)[:, L - (SSD_CONV - 1):, B_XS:B_XS + SSD_CONV_CH]
    f_new = jnp.concatenate([fa, fb], axis=-1)[ffn_tps - 1::ffn_tps]
    return y.reshape(B, L, D), g_new, h_new, c_new, f_new


def _prep_params(l, norm_mix_pre, norm_mix_post, norm_ffn_pre, norm_ffn_post, w_in, gla_w_gate2,
                 gla_b_gate, gla_norm, ssd_conv_w, ssd_conv_b, ssd_dt_bias, ssd_A_log, ssd_D, ssd_norm,
                 w_out, ffn_w_in, ffn_conv_w, ffn_conv_b, ffn_w_out):
    D = w_in.shape[1]
    sizes = (GLA_KDIM, GLA_KDIM, GLA_WIDTH, GLA_WIDTH, GLA_LOWRANK, SSD_WIDTH, SSD_CONV_CH, SSD_HEADS)
    offs = np.cumsum((0,) + sizes)
    wb16 = w_in[l].astype(BF16)
    npad = SMALL_W - GLA_LOWRANK - SSD_HEADS
    w_in_b = jnp.concatenate([wb16[:, offs[5]:offs[7]], wb16[:, offs[4]:offs[5]], wb16[:, offs[7]:offs[8]],
                              jnp.zeros((D, npad), BF16)], axis=1)
    w2p = jnp.zeros((SMALL_W, GLA_KDIM), F32).at[0:GLA_LOWRANK].set(gla_w_gate2[l])
    w2_hi = w2p.astype(BF16)
    w2_lo = (w2p - w2_hi.astype(F32)).astype(BF16)
    w2cat = jnp.concatenate([w2_hi, w2_hi, w2_lo], axis=0)
    pad_small = lambda v: jnp.zeros((1, SMALL_W), F32).at[0, DTR_OFF:DTR_OFF + SSD_HEADS].set(v)
    row = lambda v: v.reshape(1, -1)
    return dict(
        n_mix_pre=row(norm_mix_pre[l]), n_mix_post=row(norm_mix_post[l]),
        n_ffn_pre=row(norm_ffn_pre[l]), n_ffn_post=row(norm_ffn_post[l]),
        w_in_a=wb16, w_in_b=w_in_b, w2cat=w2cat, bg=row(gla_b_gate[l]), gn=row(gla_norm[l]),
        ssd_cw=ssd_conv_w[l], ssd_cb=row(ssd_conv_b[l]),
        dtb=pad_small(ssd_dt_bias[l]), alog=pad_small(ssd_A_log[l]),
        de=row(jnp.repeat(ssd_D[l], SSD_HEADDIM)), ssd_nw=row(ssd_norm[l]),
        w_out=w_out[l].astype(BF16), ffn_w_in=ffn_w_in[l].astype(BF16),
        ffn_cw=ffn_conv_w[l], ffn_cb=row(ffn_conv_b[l]), ffn_w_out=ffn_w_out[l].astype(BF16))


def kernel(x_prompt, x_sample, state_gla, state_ssd, state_ssd_conv, state_ffn_conv, norm_mix_pre,
           norm_mix_post, norm_ffn_pre, norm_ffn_post, w_in, gla_w_gate2, gla_b_gate, gla_norm,
           ssd_conv_w, ssd_conv_b, ssd_dt_bias, ssd_A_log, ssd_D, ssd_norm, w_out, ffn_w_in,
           ffn_conv_w, ffn_conv_b, ffn_w_out):
    depth = w_in.shape[0]
    xp, xs = x_prompt, x_sample
    Bp, Lp, D = xp.shape
    Bs, Ls, _ = xs.shape
    F2 = ffn_w_in.shape[2]
    outs = [[] for _ in range(8)]
    for l in range(depth):
        p = _prep_params(l, norm_mix_pre, norm_mix_post, norm_ffn_pre, norm_ffn_post, w_in,
                         gla_w_gate2, gla_b_gate, gla_norm, ssd_conv_w, ssd_conv_b, ssd_dt_bias,
                         ssd_A_log, ssd_D, ssd_norm, w_out, ffn_w_in, ffn_conv_w, ffn_conv_b, ffn_w_out)
        cp = CHUNK
        xp, g1, h1, c1, f1 = _layer(
            xp,
            jnp.zeros((Bp, GLA_HEADS, GLA_HEAD_K, GLA_HEAD_V), F32),
            jnp.zeros((Bp, SSD_HEADS, SSD_HEADDIM, SSD_STATE), F32),
            jnp.zeros((Bp, SSD_CONV - 1, SSD_CONV_CH), F32),
            jnp.zeros((Bp, FFN_CONV - 1, F2), F32),
            p, nb=1, c=cp, ns=Bp, ffn_nseq=1, ffn_L=512, ffn_tps=Lp // 512)
        nbs = CHUNK // Ls
        xs, g2, h2, c2, f2 = _layer(
            xs, state_gla[l], state_ssd[l], state_ssd_conv[l], state_ffn_conv[l],
            p, nb=nbs, c=Ls, ns=1, ffn_nseq=512 // Ls, ffn_L=Ls, ffn_tps=1)
        for lst, val in zip(outs, (g1, h1, c1, f1, g2, h2, c2, f2)):
            lst.append(val)
    return (xp, xs) + tuple(jnp.stack(o) for o in outs)
```

```python
import functools

import numpy as np
import jax
import jax.numpy as jnp
from jax import lax
from jax.experimental import pallas as pl
from jax.experimental.pallas import tpu as pltpu

F32 = jnp.float32
BF16 = jnp.bfloat16
EPS = 1e-6

GLA_HEADS = 4
GLA_HEAD_K = 128
GLA_HEAD_V = 256
GLA_KDIM = GLA_HEADS * GLA_HEAD_K
GLA_WIDTH = GLA_HEADS * GLA_HEAD_V
GLA_LOWRANK = 16
GLA_GATE_NORMALIZER = 16.0
SSD_HEADS = 16
SSD_HEADDIM = 64
SSD_STATE = 128
SSD_GROUPS = 2
SSD_WIDTH = SSD_HEADS * SSD_HEADDIM
SSD_GROUP_WIDTH = SSD_WIDTH // SSD_GROUPS
SSD_CONV = 4
SSD_BC = 2 * SSD_GROUPS * SSD_STATE
SSD_CONV_CH = SSD_WIDTH + SSD_BC
FFN_CONV = 3
CHUNK = 64

A_Q = 0
A_K = GLA_KDIM
A_V = 2 * GLA_KDIM
A_G = A_V + GLA_WIDTH
A_W = A_G + GLA_WIDTH
B_Z = 0
B_XS = SSD_WIDTH
B_BC = B_XS + SSD_WIDTH
B_SMALL = B_BC + SSD_BC
SMALL_W = 128
DTR_OFF = GLA_LOWRANK
B_W = B_SMALL + SMALL_W

SUBLANES = 8
VMEM_LIMIT = 56 * 1024 * 1024


def _cparams(sem):
    return pltpu.CompilerParams(dimension_semantics=sem, vmem_limit_bytes=VMEM_LIMIT)


def _split3(x):
    hi = x.astype(BF16)
    r = x - hi.astype(F32)
    mid = r.astype(BF16)
    lo = (r - mid.astype(F32)).astype(BF16)
    return hi, mid, lo


def _dot(a, b):
    return jnp.dot(a, b, preferred_element_type=F32)


def _dot_nt(a, b):
    return lax.dot_general(a, b, (((1,), (1,)), ((), ())), preferred_element_type=F32)


def _dot_tn(a, b):
    return lax.dot_general(a, b, (((0,), (0,)), ((), ())), preferred_element_type=F32)


def _dot_sel(sel3_bf16, x_f32):
    return _dot(sel3_bf16, jnp.concatenate(_split3(x_f32), axis=0))


NEG_LOG2E = -1.4426950408889634


def _silu(x):
    return x / (1.0 + jnp.exp2(x * NEG_LOG2E))


def _softplus(x):
    return jnp.maximum(x, 0.0) + jnp.log1p(jnp.exp(-jnp.abs(x)))


def _rms(x, w):
    return x * lax.rsqrt(jnp.mean(x * x, axis=-1, keepdims=True) + EPS) * w


def _chunk_consts(nb, c):
    R = nb * c
    idx = np.arange(R)
    seq, pos = idx // c, idx % c
    same = seq[:, None] == seq[None, :]
    t, u = pos[:, None], pos[None, :]
    blocks = [same & (u <= t), same & (u > t)]
    masks = [np.eye(R, dtype=bool)]
    m = c // 2
    while m >= 1:
        blk = pos // (2 * m)
        rho = blk * 2 * m + m - 1
        upper = pos > rho
        a_up = upper[:, None] & (u > rho[:, None]) & (u <= t)
        a_lo = (~upper)[:, None] & (u > t) & (u <= rho[:, None])
        blocks.append(same & (a_up | a_lo))
        masks.append(same & upper[:, None] & (~upper)[None, :] & (blk[:, None] == blk[None, :]))
        m //= 2
    sel = np.concatenate(blocks, 0).astype(np.float32)
    lvl = np.stack(masks).astype(np.float32)
    causal = (same & (u <= t)).astype(np.float32)
    return sel, lvl, causal


def _inproj_kernel(x_ref, nw_ref, w_ref, o_ref):
    xn = _rms(x_ref[...], nw_ref[...]).astype(BF16)
    o_ref[...] = _dot(xn, w_ref[...])


def _inproj(x2d, nw, w_bf16, n_out, tm):
    M, D = x2d.shape
    return pl.pallas_call(
        _inproj_kernel,
        grid=(M // tm,),
        in_specs=[pl.BlockSpec((tm, D), lambda i: (i, 0)),
                  pl.BlockSpec((1, D), lambda i: (0, 0)),
                  pl.BlockSpec((D, n_out), lambda i: (0, 0))],
        out_specs=pl.BlockSpec((tm, n_out), lambda i: (i, 0)),
        out_shape=jax.ShapeDtypeStruct((M, n_out), F32),
        compiler_params=_cparams(("parallel",)),
        name="inproj",
    )(x2d, nw, w_bf16)


def _gla_group(gi, q_ref, k_ref, v_ref, g_ref, sm_ref, s_in, s_out, sel_ref, lvl_ref, w2_ref, bg_ref,
               gn_ref, o_ref, *, nb, c, nlev):
    R = nb * c
    sm = sm_ref[gi]
    sm_hi = sm.astype(BF16)
    sm_lo = (sm - sm_hi.astype(F32)).astype(BF16)
    zg = _dot(jnp.concatenate([sm_hi, sm_lo, sm_hi], axis=1), w2_ref[...]) + bg_ref[...]
    lg2 = -_softplus(-zg) * (-NEG_LOG2E / GLA_GATE_NORMALIZER)
    E = _dot_sel(sel_ref[...], lg2)
    eb = jnp.exp2(E[0:R])
    q = q_ref[gi] * (GLA_HEAD_K ** -0.5)
    k = k_ref[gi]
    qe = q * eb
    ke = k * jnp.exp2(E[R:2 * R])
    ql, kl = [q.astype(BF16)], [k.astype(BF16)]
    for l in range(nlev):
        p = jnp.exp2(E[(2 + l) * R:(3 + l) * R])
        ql.append((q * p).astype(BF16))
        kl.append((k * p).astype(BF16))
    vb = v_ref[gi].astype(BF16)
    g = g_ref[gi]
    gn = gn_ref[...]
    rows = lax.broadcasted_iota(jnp.int32, (R, 1), 0)

    zk = jnp.zeros((R, GLA_HEAD_K), BF16)
    zv = jnp.zeros((R, GLA_HEAD_V), BF16)
    o_intra = []
    for hp in range(GLA_HEADS // 2):
        k0 = slice(2 * hp * GLA_HEAD_K, (2 * hp + 1) * GLA_HEAD_K)
        k1 = slice((2 * hp + 1) * GLA_HEAD_K, (2 * hp + 2) * GLA_HEAD_K)
        k01 = slice(2 * hp * GLA_HEAD_K, (2 * hp + 2) * GLA_HEAD_K)
        att2 = jnp.zeros((R, 2 * R), F32)
        for l in range(nlev + 1):
            kbd = jnp.concatenate([jnp.concatenate([kl[l][:, k0], zk], axis=1),
                                   jnp.concatenate([zk, kl[l][:, k1]], axis=1)], axis=0)
            att2 = att2 + _dot_nt(ql[l][:, k01], kbd) * lvl_ref[l]
        v0 = vb[:, 2 * hp * GLA_HEAD_V:(2 * hp + 1) * GLA_HEAD_V]
        v1 = vb[:, (2 * hp + 1) * GLA_HEAD_V:(2 * hp + 2) * GLA_HEAD_V]
        vbd = jnp.concatenate([jnp.concatenate([v0, zv], axis=1),
                               jnp.concatenate([zv, v1], axis=1)], axis=0)
        o2 = _dot(att2.astype(BF16), vbd)
        o_intra += [o2[:, 0:GLA_HEAD_V], o2[:, GLA_HEAD_V:2 * GLA_HEAD_V]]

    for h in range(GLA_HEADS):
        ks = slice(h * GLA_HEAD_K, (h + 1) * GLA_HEAD_K)
        vs = slice(h * GLA_HEAD_V, (h + 1) * GLA_HEAD_V)
        o = o_intra[h]
        for s in range(nb):
            if nb > 1:
                rm = jnp.logical_and(rows >= s * c, rows < (s + 1) * c).astype(F32)
                qs = (qe[:, ks] * rm).astype(BF16)
                kd = (ke[:, ks] * rm).astype(BF16)
            else:
                qs = qe[:, ks].astype(BF16)
                kd = ke[:, ks].astype(BF16)
            S = s_in[gi * nb + s, h]
            o = o + _dot(qs, S.astype(BF16))
            upd = _dot_tn(kd, vb[:, vs])
            d = eb[s * c + c - 1:s * c + c, ks]
            dcol = jnp.transpose(jnp.broadcast_to(d, (GLA_HEAD_K, GLA_HEAD_K)))
            s_out[gi * nb + s, h] = jnp.concatenate([dcol, dcol], axis=1) * S + upd
        gh = g[:, vs]
        o_ref[gi, :, vs] = (_rms(o, gn) * _silu(gh)).astype(BF16)


def _conv_taps(u8, prev8, w, bias, width, chained):
    sub = lax.broadcasted_iota(jnp.int32, (1, SUBLANES, 1), 1)
    acc = bias + w[width - 1:width] * u8
    for j in range(width - 1):
        d = width - 1 - j
        r = pltpu.roll(u8, d, axis=1)
        pr = pltpu.roll(prev8, d, axis=1)
        if chained:
            pr = jnp.concatenate([pr, r[:-1]], axis=0)
        acc = acc + w[j:j + 1] * jnp.where(sub < d, pr, r)
    return acc


def _ssd_group(gi, z_ref, xs_ref, bc_ref, sm_ref, sel_ref, cm_ref, ex_ref, cw_ref, cb_ref, dtb_ref,
               alog_ref, de_ref, nw_ref, y_ref, ht_scr, cx_scr, cbc_scr, *, nb, c, out_lane0=0):
    R = nb * c
    sq = slice(gi * nb, (gi + 1) * nb)
    cw = cw_ref[...]
    cbias = cb_ref[...]
    nt = R // SUBLANES
    chained = nb == 1
    xraw = xs_ref[gi].reshape(nt, SUBLANES, SSD_WIDTH)
    bcraw = bc_ref[gi].reshape(nt, SUBLANES, SSD_BC)
    xs = _silu(_conv_taps(xraw, cx_scr[sq], cw[:, 0:SSD_WIDTH], cbias[:, 0:SSD_WIDTH], SSD_CONV, chained))
    bca = _silu(_conv_taps(bcraw, cbc_scr[sq], cw[:, SSD_WIDTH:SSD_CONV_CH],
                           cbias[:, SSD_WIDTH:SSD_CONV_CH], SSD_CONV, chained))
    cx_scr[sq] = xraw[nt - nb:nt]
    cbc_scr[sq] = bcraw[nt - nb:nt]
    xs = xs.reshape(R, SSD_WIDTH)
    bca = bca.reshape(R, SSD_BC)
    Bm = bca[:, 0:SSD_GROUPS * SSD_STATE]
    Cm = bca[:, SSD_GROUPS * SSD_STATE:SSD_BC]

    dt = _softplus(sm_ref[gi] + dtb_ref[...])
    la2 = dt * (NEG_LOG2E * jnp.exp(alog_ref[...]))
    cs = _dot_sel(sel_ref[0:2 * R], la2)
    cum = cs[0:R]
    stack = jnp.concatenate([cs, dt], axis=0)
    st_e = _dot(jnp.concatenate(_split3(stack), axis=1), ex_ref[...])
    cum_e = st_e[0:R]
    lmc_e = st_e[R:2 * R]
    dt_e = st_e[2 * R:3 * R]
    ecum_e = jnp.exp2(cum_e)
    xdt = xs * dt_e
    xw = (xdt * jnp.exp2(lmc_e)).astype(BF16)
    Bb = Bm.astype(BF16)
    Cb = Cm.astype(BF16)

    cumT2 = jnp.transpose(jnp.concatenate([cum, cum], axis=0))
    cmask2 = cm_ref[...] > 0.5
    rows = lax.broadcasted_iota(jnp.int32, (R, 1), 0)
    lane = lax.broadcasted_iota(jnp.int32, (1, 2 * R), 1)
    first = lane < R
    lo_half = lax.broadcasted_iota(jnp.int32, (1, 128), 1) < SSD_HEADDIM

    hpg = SSD_HEADS // SSD_GROUPS
    for hg in range(SSD_GROUPS):
        ns = slice(hg * SSD_STATE, (hg + 1) * SSD_STATE)
        gs = slice(hg * SSD_GROUP_WIDTH, (hg + 1) * SSD_GROUP_WIDTH)
        bgrp = Bb[:, ns]
        cb2 = _dot_nt(Cb[:, ns], jnp.concatenate([bgrp, bgrp], axis=0))
        yg = []
        for j in range(hpg // 2):
            ls = slice(hg * SSD_GROUP_WIDTH + j * 128, hg * SSD_GROUP_WIDTH + (j + 1) * 128)
            h0 = DTR_OFF + hg * hpg + 2 * j
            col = jnp.where(first, cum[:, h0:h0 + 1], cum[:, h0 + 1:h0 + 2])
            row = jnp.where(first, cumT2[h0:h0 + 1, :], cumT2[h0 + 1:h0 + 2, :])
            m2 = cb2 * jnp.exp2(jnp.where(cmask2, col - row, -jnp.inf))
            xpair = xdt[:, ls]
            x2 = jnp.concatenate([jnp.where(lo_half, xpair, 0.0).astype(BF16),
                                  jnp.where(lo_half, 0.0, xpair).astype(BF16)], axis=0)
            yg.append(_dot(m2.astype(BF16), x2))
        y_intra = jnp.concatenate(yg, axis=1)
        y_inter = jnp.zeros((R, SSD_GROUP_WIDTH), F32)
        for s in range(nb):
            if nb > 1:
                rm = jnp.logical_and(rows >= s * c, rows < (s + 1) * c).astype(F32)
                cg = (Cm[:, ns] * rm).astype(BF16)
                bg_ = (Bm[:, ns] * rm).astype(BF16)
            else:
                cg = Cb[:, ns]
                bg_ = Bb[:, ns]
            hT = ht_scr[gi * nb + s, :, gs]
            y_inter = y_inter + _dot(cg, hT.astype(BF16))
            upd = _dot_tn(bg_, xw[:, gs])
            dl = ecum_e[s * c + c - 1:s * c + c, gs]
            ht_scr[gi * nb + s, :, gs] = dl * hT + upd
        y = y_intra + y_inter * ecum_e[:, gs] + de_ref[:, gs] * xs[:, gs]
        y = y * _silu(z_ref[gi, :, gs])
        os_ = slice(out_lane0 + hg * SSD_GROUP_WIDTH, out_lane0 + (hg + 1) * SSD_GROUP_WIDTH)
        y_ref[gi, :, os_] = _rms(y, nw_ref[:, gs]).astype(BF16)


def _group_view(a, G):
    return a.reshape(G, a.shape[0] // G, a.shape[1])


def _mixer_kernel(q_ref, k_ref, v_ref, g_ref, z_ref, xs_ref, bc_ref, sm_ref, s0_ref, cst_ref, h0_ref,
                  sel_ref, lvl_ref, w2_ref, bg_ref, gn_ref, cm_ref, ex_ref, cw_ref, cb_ref, dtb_ref,
                  alog_ref, de_ref, nw_ref, mix_ref, sn_ref, hn_ref, *scratch, ns, nb, c, nlev):
    ci = pl.program_id(1)
    if len(scratch) == 4:
        s_scr, ht_scr, cx_scr, cbc_scr = scratch
        s_in = s_out = s_scr
    else:
        ht_scr, cx_scr, cbc_scr = scratch
        s_scr, s_in, s_out = None, s0_ref, sn_ref

    @pl.when(ci == 0)
    def _():
        if s_scr is not None:
            s_scr[...] = s0_ref[...]
        cx_scr[...] = jnp.zeros_like(cx_scr)
        cbc_scr[...] = jnp.zeros_like(cbc_scr)
        cx_scr[:, SUBLANES - (SSD_CONV - 1):SUBLANES, :] = cst_ref[:, :, 0:SSD_WIDTH]
        cbc_scr[:, SUBLANES - (SSD_CONV - 1):SUBLANES, :] = cst_ref[:, :, SSD_WIDTH:SSD_CONV_CH]
        for s in range(ns * nb):
            ht_scr[s] = jnp.transpose(h0_ref[s].reshape(SSD_WIDTH, SSD_STATE))

    for gi in range(ns):
        _gla_group(gi, q_ref, k_ref, v_ref, g_ref, sm_ref, s_in, s_out, sel_ref, lvl_ref, w2_ref, bg_ref,
                   gn_ref, mix_ref, nb=nb, c=c, nlev=nlev)
        _ssd_group(gi, z_ref, xs_ref, bc_ref, sm_ref, sel_ref, cm_ref, ex_ref, cw_ref, cb_ref, dtb_ref,
                   alog_ref, de_ref, nw_ref, mix_ref, ht_scr, cx_scr, cbc_scr, nb=nb, c=c,
                   out_lane0=GLA_WIDTH)

    @pl.when(ci == pl.num_programs(1) - 1)
    def _():
        if s_scr is not None:
            sn_ref[...] = s_scr[...]
        for s in range(ns * nb):
            hn_ref[s] = jnp.transpose(ht_scr[s]).reshape(SSD_HEADS, SSD_HEADDIM, SSD_STATE)


def _mixer(proj_a, proj_b, s_gla, s_conv, s_ssd, p, *, B, L, nb, c, ns):
    R = nb * c
    G = B // nb
    Lg = L * nb
    ncl = Lg // R
    nlev = int(np.log2(c))
    assert 2 * R == 128, "two heads share one 128-lane tile"
    assert c % SUBLANES == 0 and (nb == 1 or c == SUBLANES), "conv carry is one sublane tile per sequence"
    sel, lvl, causal = _chunk_consts(nb, c)
    sel = jnp.asarray(np.tile(sel, (1, 3)), BF16)
    lvl = jnp.asarray(np.tile(lvl, (1, 1, 2)), F32)
    causal = jnp.asarray(np.tile(causal, (1, 2)), F32)
    exn = np.zeros((SMALL_W, SSD_WIDTH), np.float32)
    for h in range(SSD_HEADS):
        exn[DTR_OFF + h, h * SSD_HEADDIM:(h + 1) * SSD_HEADDIM] = 1.0
    ex = jnp.asarray(np.tile(exn, (3, 1)), BF16)
    M = proj_a.shape[0]
    pa, pb = _group_view(proj_a, G), _group_view(proj_b, G)

    def rowblk(w, col):
        return pl.BlockSpec((ns, R, w), lambda bi, ci: (bi, ci, col // w))

    const2 = lambda a: pl.BlockSpec(a.shape, lambda bi, ci: (0,) * a.ndim)
    nsq = ns * nb
    sshape = (nsq, GLA_HEADS, GLA_HEAD_K, GLA_HEAD_V)
    sspec = pl.BlockSpec(sshape, lambda bi, ci: (bi, 0, 0, 0))
    hspec = pl.BlockSpec((nsq, SSD_HEADS, SSD_HEADDIM, SSD_STATE), lambda bi, ci: (bi, 0, 0, 0))
    consts = (sel, lvl, p['w2cat'], p['bg'], p['gn'], causal, ex, p['ssd_cw'], p['ssd_cb'], p['dtb'],
              p['alog'], p['de'], p['ssd_nw'])
    kern = functools.partial(_mixer_kernel, ns=ns, nb=nb, c=c, nlev=nlev)
    mix, s_new, h_new = pl.pallas_call(
        kern,
        grid=(G // ns, ncl),
        in_specs=[rowblk(GLA_KDIM, A_Q), rowblk(GLA_KDIM, A_K), rowblk(GLA_WIDTH, A_V),
                  rowblk(GLA_WIDTH, A_G), rowblk(SSD_WIDTH, B_Z), rowblk(SSD_WIDTH, B_XS),
                  rowblk(SSD_BC, B_BC), rowblk(SMALL_W, B_SMALL), sspec,
                  pl.BlockSpec((nsq, SSD_CONV - 1, SSD_CONV_CH), lambda bi, ci: (bi, 0, 0)), hspec]
                 + [const2(a) for a in consts],
        out_specs=[pl.BlockSpec((ns, R, GLA_WIDTH + SSD_WIDTH), lambda bi, ci: (bi, ci, 0)), sspec, hspec],
        out_shape=[jax.ShapeDtypeStruct((G, Lg, GLA_WIDTH + SSD_WIDTH), BF16),
                   jax.ShapeDtypeStruct((B, GLA_HEADS, GLA_HEAD_K, GLA_HEAD_V), F32),
                   jax.ShapeDtypeStruct((B, SSD_HEADS, SSD_HEADDIM, SSD_STATE), F32)],
        scratch_shapes=([pltpu.VMEM(sshape, F32)] if ncl > 1 else []) + [
            pltpu.VMEM((nsq, SSD_STATE, SSD_WIDTH), F32),
            pltpu.VMEM((nsq, SUBLANES, SSD_WIDTH), F32),
            pltpu.VMEM((nsq, SUBLANES, SSD_BC), F32)],
        compiler_params=_cparams(("parallel", "arbitrary")),
        name="mixer",
    )(pa, pa, pa, pa, pb, pb, pb, pb, s_gla, s_conv, s_ssd, *consts)
    return mix.reshape(M, GLA_WIDTH + SSD_WIDTH), s_new, h_new


def _outproj_kernel(m_ref, x_ref, w_ref, npost_ref, npre_ref, h_ref, xn_ref):
    h = x_ref[...] + _rms(_dot(m_ref[...], w_ref[...]), npost_ref[...])
    h_ref[...] = h
    xn_ref[...] = _rms(h, npre_ref[...]).astype(BF16)


def _outproj(mix, x2d, w_out_bf16, npost, npre, tm):
    M, D = x2d.shape
    W = w_out_bf16.shape[0]
    return pl.pallas_call(
        _outproj_kernel,
        grid=(M // tm,),
        in_specs=[pl.BlockSpec((tm, W), lambda i: (i, 0)),
                  pl.BlockSpec((tm, D), lambda i: (i, 0)),
                  pl.BlockSpec((W, D), lambda i: (0, 0)),
                  pl.BlockSpec((1, D), lambda i: (0, 0)),
                  pl.BlockSpec((1, D), lambda i: (0, 0))],
        out_specs=[pl.BlockSpec((tm, D), lambda i: (i, 0)),
                   pl.BlockSpec((tm, D), lambda i: (i, 0))],
        out_shape=[jax.ShapeDtypeStruct((M, D), F32), jax.ShapeDtypeStruct((M, D), BF16)],
        compiler_params=_cparams(("parallel",)),
        name="outproj",
    )(mix, x2d, w_out_bf16, npost, npre)


FFN_HDR = SUBLANES
FFN_TF = 512
FFN_RBLK = 64
FFN_SPLIT = 2


def _ffn_kernel(xn_ref, h_ref, sa_ref, sb_ref, wa_ref, wb_ref, cwa_ref, cwb_ref, cba_ref, cbb_ref,
                wo_ref, npost_ref, y_ref, na_ref, nb_ref, ua_scr, ub_scr, act_scr, *carry,
                nseq, L, tps):
    i = pl.program_id(0)
    j = pl.program_id(1)
    tf = wa_ref.shape[1]
    W1 = FFN_CONV - 1
    H = FFN_HDR
    first = i % tps == 0

    if tps > 1:
        ca_scr, cb_scr = carry

        @pl.when(jnp.logical_not(first))
        def _():
            ua_scr[:, 0:H, :] = ca_scr[j]
            ub_scr[:, 0:H, :] = cb_scr[j]

    @pl.when(first)
    def _():
        ua_scr[:, H - W1:H, :] = sa_ref[...]
        ub_scr[:, H - W1:H, :] = sb_ref[...]

    @pl.when(j == 0)
    def _():
        y_ref[...] = jnp.zeros_like(y_ref)

    if nseq == 1:
        parts = [(0, 1, k * L // FFN_SPLIT, (k + 1) * L // FFN_SPLIT) for k in range(FFN_SPLIT)]
    else:
        parts = [(k * nseq // FFN_SPLIT, (k + 1) * nseq // FFN_SPLIT, 0, L) for k in range(FFN_SPLIT)]
    tile_rows = lambda q0, q1, r0, r1: slice(q0 * L + r0, (q1 - 1) * L + r1)

    for (q0, q1, r0, r1) in parts:
        rs = tile_rows(q0, q1, r0, r1)
        ua_scr[q0:q1, H + r0:H + r1, :] = _dot(xn_ref[rs, :], wa_ref[...]).reshape(q1 - q0, r1 - r0, tf)
        ub_scr[q0:q1, H + r0:H + r1, :] = _dot(xn_ref[rs, :], wb_ref[...]).reshape(q1 - q0, r1 - r0, tf)

    spread = lambda row: jnp.broadcast_to(row, (SUBLANES, tf))
    wa8 = [spread(cwa_ref[t:t + 1, :]) for t in range(FFN_CONV)]
    wb8 = [spread(cwb_ref[t:t + 1, :]) for t in range(FFN_CONV)]
    ba8 = spread(cba_ref[...])
    bb8 = spread(cbb_ref[...])
    for (q0, q1, r0, r1) in parts:
        if nseq == 1:
            blocks = [(0, 1, r, r + FFN_RBLK) for r in range(r0, r1, FFN_RBLK)]
        else:
            qb = FFN_RBLK // L
            blocks = [(q, q + qb, 0, L) for q in range(q0, q1, qb)]
        for (bq0, bq1, br0, br1) in blocks:
            def conv(u_scr, w8, b8):
                tap = lambda d: u_scr[bq0:bq1, H - d + br0:H - d + br1, :].reshape(-1, SUBLANES, tf)
                out = b8 + w8[W1] * tap(0)
                for t in range(W1):
                    out = out + w8[t] * tap(W1 - t)
                return out
            a = conv(ua_scr, wa8, ba8)
            b = conv(ub_scr, wb8, bb8)
            row0 = bq0 * L + br0
            act_scr[row0:row0 + FFN_RBLK, :] = (_silu(a) * b).reshape(FFN_RBLK, tf).astype(BF16)
        rs = tile_rows(q0, q1, r0, r1)
        y_ref[rs, :] += _dot(act_scr[rs, :], wo_ref[...])

    na_ref[...] = ua_scr[:, H + L - W1:H + L, :]
    nb_ref[...] = ub_scr[:, H + L - W1:H + L, :]
    if tps > 1:
        ca_scr[j] = ua_scr[:, L:L + H, :]
        cb_scr[j] = ub_scr[:, L:L + H, :]

    @pl.when(j == pl.num_programs(1) - 1)
    def _():
        y_ref[...] = h_ref[...] + _rms(y_ref[...], npost_ref[...])


def _ffn(xn2, h2d, st, w_in_chunks, cw, cb, w_out_bf16, npost, *, nseq, L, tps):
    M, D = h2d.shape
    F = w_out_bf16.shape[0]
    tf = w_in_chunks.shape[2]
    nj = F // tf
    tm = nseq * L
    W1 = FFN_CONV - 1
    assert (L if nseq == 1 else nseq * L) % (FFN_SPLIT * FFN_RBLK) == 0
    kern = functools.partial(_ffn_kernel, nseq=nseq, L=L, tps=tps)
    stspec_a = pl.BlockSpec((nseq, W1, tf), lambda i, j: (i // tps, 0, j))
    stspec_b = pl.BlockSpec((nseq, W1, tf), lambda i, j: (i // tps, 0, j + nj))
    return pl.pallas_call(
        kern,
        grid=(M // tm, nj),
        in_specs=[pl.BlockSpec((tm, D), lambda i, j: (i, 0)),
                  pl.BlockSpec((tm, D), lambda i, j: (i, 0)),
                  stspec_a, stspec_b,
                  pl.BlockSpec((None, D, tf), lambda i, j: (j, 0, 0)),
                  pl.BlockSpec((None, D, tf), lambda i, j: (j + nj, 0, 0)),
                  pl.BlockSpec((FFN_CONV, tf), lambda i, j: (0, j)),
                  pl.BlockSpec((FFN_CONV, tf), lambda i, j: (0, j + nj)),
                  pl.BlockSpec((1, tf), lambda i, j: (0, j)),
                  pl.BlockSpec((1, tf), lambda i, j: (0, j + nj)),
                  pl.BlockSpec((tf, D), lambda i, j: (j, 0)),
                  pl.BlockSpec((1, D), lambda i, j: (0, 0))],
        out_specs=[pl.BlockSpec((tm, D), lambda i, j: (i, 0)),
                   pl.BlockSpec((nseq, W1, tf), lambda i, j: (i, 0, j)),
                   pl.BlockSpec((nseq, W1, tf), lambda i, j: (i, 0, j))],
        out_shape=[jax.ShapeDtypeStruct((M, D), F32),
                   jax.ShapeDtypeStruct((M // L, W1, F), F32),
                   jax.ShapeDtypeStruct((M // L, W1, F), F32)],
        scratch_shapes=[pltpu.VMEM((nseq, FFN_HDR + L, tf), F32),
                        pltpu.VMEM((nseq, FFN_HDR + L, tf), F32),
                        pltpu.VMEM((tm, tf), BF16)] + (
                            [pltpu.VMEM((nj, nseq, FFN_HDR, tf), F32)] * 2 if tps > 1 else []),
        compiler_params=_cparams(("arbitrary", "arbitrary")),
        name="ffn",
    )(xn2, h2d, st, st, w_in_chunks, w_in_chunks, cw, cw, cb, cb, w_out_bf16, npost)


def _layer(x, s_gla, s_ssd, s_conv, s_ffn, p, *, nb, c, ns, ffn_nseq, ffn_L, ffn_tps):
    B, L, D = x.shape
    M = B * L
    x2d = x.reshape(M, D)
    proj_a = _inproj(x2d, p['n_mix_pre'], p['w_in_a'], A_W, 512)
    proj_b = _inproj(x2d, p['n_mix_pre'], p['w_in_b'], B_W, 512)
    mix, g_new, h_new = _mixer(proj_a, proj_b, s_gla, s_conv, s_ssd, p, B=B, L=L, nb=nb, c=c, ns=ns)
    hres, xn2 = _outproj(mix, x2d, p['w_out'], p['n_mix_post'], p['n_ffn_pre'], 512)
    y, fa, fb = _ffn(xn2, hres, s_ffn, p['ffn_w_in'], p['ffn_cw'], p['ffn_cb'], p['ffn_w_out'],
                     p['n_ffn_post'], nseq=ffn_nseq, L=ffn_L, tps=ffn_tps)
    c_new = proj_b.reshape(B, L, -1)[:, L - (SSD_CONV - 1):, B_XS:B_XS + SSD_CONV_CH]
    f_new = jnp.concatenate([fa, fb], axis=-1)[ffn_tps - 1::ffn_tps]
    return y.reshape(B, L, D), g_new, h_new, c_new, f_new


def _prep_params(l, norm_mix_pre, norm_mix_post, norm_ffn_pre, norm_ffn_post, w_in, gla_w_gate2,
                 gla_b_gate, gla_norm, ssd_conv_w, ssd_conv_b, ssd_dt_bias, ssd_A_log, ssd_D, ssd_norm,
                 w_out, ffn_w_in, ffn_conv_w, ffn_conv_b, ffn_w_out):
    D = w_in.shape[1]
    sizes = (GLA_KDIM, GLA_KDIM, GLA_WIDTH, GLA_WIDTH, GLA_LOWRANK, SSD_WIDTH, SSD_CONV_CH, SSD_HEADS)
    offs = np.cumsum((0,) + sizes)
    wb16 = w_in[l].astype(BF16)
    npad = SMALL_W - GLA_LOWRANK - SSD_HEADS
    w_in_b = jnp.concatenate([wb16[:, offs[5]:offs[7]], wb16[:, offs[4]:offs[5]], wb16[:, offs[7]:offs[8]],
                              jnp.zeros((D, npad), BF16)], axis=1)
    w2p = jnp.zeros((SMALL_W, GLA_KDIM), F32).at[0:GLA_LOWRANK].set(gla_w_gate2[l])
    w2_hi = w2p.astype(BF16)
    w2_lo = (w2p - w2_hi.astype(F32)).astype(BF16)
    w2cat = jnp.concatenate([w2_hi, w2_hi, w2_lo], axis=0)
    pad_small = lambda v: jnp.zeros((1, SMALL_W), F32).at[0, DTR_OFF:DTR_OFF + SSD_HEADS].set(v)
    row = lambda v: v.reshape(1, -1)
    return dict(
        n_mix_pre=row(norm_mix_pre[l]), n_mix_post=row(norm_mix_post[l]),
        n_ffn_pre=row(norm_ffn_pre[l]), n_ffn_post=row(norm_ffn_post[l]),
        w_in_a=wb16, w_in_b=w_in_b, w2cat=w2cat, bg=row(gla_b_gate[l]), gn=row(gla_norm[l]),
        ssd_cw=ssd_conv_w[l], ssd_cb=row(ssd_conv_b[l]),
        dtb=pad_small(ssd_dt_bias[l]), alog=pad_small(ssd_A_log[l]),
        de=row(jnp.repeat(ssd_D[l], SSD_HEADDIM)), ssd_nw=row(ssd_norm[l]),
        w_out=w_out[l].astype(BF16),
        ffn_w_in=ffn_w_in[l].astype(BF16).reshape(D, -1, FFN_TF).transpose(1, 0, 2),
        ffn_cw=ffn_conv_w[l], ffn_cb=row(ffn_conv_b[l]), ffn_w_out=ffn_w_out[l].astype(BF16))


def kernel(x_prompt, x_sample, state_gla, state_ssd, state_ssd_conv, state_ffn_conv, norm_mix_pre,
           norm_mix_post, norm_ffn_pre, norm_ffn_post, w_in, gla_w_gate2, gla_b_gate, gla_norm,
           ssd_conv_w, ssd_conv_b, ssd_dt_bias, ssd_A_log, ssd_D, ssd_norm, w_out, ffn_w_in,
           ffn_conv_w, ffn_conv_b, ffn_w_out):
    depth = w_in.shape[0]
    xp, xs = x_prompt, x_sample
    Bp, Lp, D = xp.shape
    Bs, Ls, _ = xs.shape
    F2 = ffn_w_in.shape[2]
    outs = [[] for _ in range(8)]
    for l in range(depth):
        p = _prep_params(l, norm_mix_pre, norm_mix_post, norm_ffn_pre, norm_ffn_post, w_in,
                         gla_w_gate2, gla_b_gate, gla_norm, ssd_conv_w, ssd_conv_b, ssd_dt_bias,
                         ssd_A_log, ssd_D, ssd_norm, w_out, ffn_w_in, ffn_conv_w, ffn_conv_b, ffn_w_out)
        cp = CHUNK
        xp, g1, h1, c1, f1 = _layer(
            xp,
            jnp.zeros((Bp, GLA_HEADS, GLA_HEAD_K, GLA_HEAD_V), F32),
            jnp.zeros((Bp, SSD_HEADS, SSD_HEADDIM, SSD_STATE), F32),
            jnp.zeros((Bp, SSD_CONV - 1, SSD_CONV_CH), F32),
            jnp.zeros((Bp, FFN_CONV - 1, F2), F32),
            p, nb=1, c=cp, ns=Bp, ffn_nseq=1, ffn_L=512, ffn_tps=Lp // 512)
        nbs = CHUNK // Ls
        xs, g2, h2, c2, f2 = _layer(
            xs, state_gla[l], state_ssd[l], state_ssd_conv[l], state_ffn_conv[l],
            p, nb=nbs, c=Ls, ns=1, ffn_nseq=512 // Ls, ffn_L=Ls, ffn_tps=1)
        for lst, val in zip(outs, (g1, h1, c1, f1, g2, h2, c2, f2)):
            lst.append(val)
    return (xp, xs) + tuple(jnp.stack(o) for o in outs)
```

```python
import functools

import numpy as np
import jax
import jax.numpy as jnp
from jax import lax
from jax.experimental import pallas as pl
from jax.experimental.pallas import tpu as pltpu

F32 = jnp.float32
BF16 = jnp.bfloat16
EPS = 1e-6

GLA_HEADS = 4
GLA_HEAD_K = 128
GLA_HEAD_V = 256
GLA_KDIM = GLA_HEADS * GLA_HEAD_K
GLA_WIDTH = GLA_HEADS * GLA_HEAD_V
GLA_LOWRANK = 16
GLA_GATE_NORMALIZER = 16.0
SSD_HEADS = 16
SSD_HEADDIM = 64
SSD_STATE = 128
SSD_GROUPS = 2
SSD_WIDTH = SSD_HEADS * SSD_HEADDIM
SSD_GROUP_WIDTH = SSD_WIDTH // SSD_GROUPS
SSD_CONV = 4
SSD_BC = 2 * SSD_GROUPS * SSD_STATE
SSD_CONV_CH = SSD_WIDTH + SSD_BC
FFN_CONV = 3
CHUNK = 64

A_Q = 0
A_K = GLA_KDIM
A_V = 2 * GLA_KDIM
A_G = A_V + GLA_WIDTH
A_W = A_G + GLA_WIDTH
IN_TAIL = GLA_LOWRANK + SSD_WIDTH + SSD_CONV_CH + SSD_HEADS
B_Z = 0
B_XS = SSD_WIDTH
B_BC = B_XS + SSD_WIDTH
B_SMALL = B_BC + SSD_BC
SMALL_W = 128
DTR_OFF = 0
LR_OFF = SSD_HEADS
B_W = B_SMALL + SMALL_W

SUBLANES = 8
VMEM_LIMIT = 56 * 1024 * 1024


def _cparams(sem):
    return pltpu.CompilerParams(dimension_semantics=sem, vmem_limit_bytes=VMEM_LIMIT)


def _split3(x):
    hi = x.astype(BF16)
    r = x - hi.astype(F32)
    mid = r.astype(BF16)
    lo = (r - mid.astype(F32)).astype(BF16)
    return hi, mid, lo


def _dot(a, b):
    return jnp.dot(a, b, preferred_element_type=F32)


def _dot_nt(a, b):
    return lax.dot_general(a, b, (((1,), (1,)), ((), ())), preferred_element_type=F32)


def _dot_tn(a, b):
    return lax.dot_general(a, b, (((0,), (0,)), ((), ())), preferred_element_type=F32)


def _dot_sel(sel3_bf16, x_f32):
    return _dot(sel3_bf16, jnp.concatenate(_split3(x_f32), axis=0))


NEG_LOG2E = -1.4426950408889634


def _silu(x):
    return x / (1.0 + jnp.exp2(x * NEG_LOG2E))


def _softplus(x):
    return jnp.maximum(x, 0.0) + jnp.log1p(jnp.exp(-jnp.abs(x)))


def _rms(x, w):
    return x * lax.rsqrt(jnp.mean(x * x, axis=-1, keepdims=True) + EPS) * w


def _chunk_consts(nb, c):
    R = nb * c
    idx = np.arange(R)
    seq, pos = idx // c, idx % c
    same = seq[:, None] == seq[None, :]
    t, u = pos[:, None], pos[None, :]
    blocks = [same & (u <= t), same & (u > t)]
    masks = [np.eye(R, dtype=bool)]
    m = c // 2
    while m >= 1:
        blk = pos // (2 * m)
        rho = blk * 2 * m + m - 1
        upper = pos > rho
        a_up = upper[:, None] & (u > rho[:, None]) & (u <= t)
        a_lo = (~upper)[:, None] & (u > t) & (u <= rho[:, None])
        blocks.append(same & (a_up | a_lo))
        masks.append(same & upper[:, None] & (~upper)[None, :] & (blk[:, None] == blk[None, :]))
        m //= 2
    sel = np.concatenate(blocks, 0).astype(np.float32)
    lvl = np.stack(masks).astype(np.float32)
    causal = (same & (u <= t)).astype(np.float32)
    return sel, lvl, causal


INPROJ_RCH = 256


def _inproj_kernel(x_ref, nw_ref, w_ref, o_ref, *scratch, tail):
    if tail:
        w_scr, = scratch

        @pl.when(pl.program_id(0) == 0)
        def _():
            zpad = jnp.zeros((INPROJ_RCH, B_W - IN_TAIL), BF16)
            for r in range(0, w_ref.shape[0], INPROJ_RCH):
                blk = w_ref[r:r + INPROJ_RCH, :]
                w_scr[r:r + INPROJ_RCH, :] = jnp.concatenate(
                    [blk[:, GLA_LOWRANK:IN_TAIL], blk[:, 0:GLA_LOWRANK], zpad], axis=1)

        w = w_scr
    else:
        w = w_ref
    xn = _rms(x_ref[...], nw_ref[...]).astype(BF16)
    o_ref[...] = _dot(xn, w[...])


def _inproj(x2d, nw, w_bf16, tm, *, tail):
    M, D = x2d.shape
    n_out = B_W if tail else A_W
    return pl.pallas_call(
        functools.partial(_inproj_kernel, tail=tail),
        grid=(M // tm,),
        in_specs=[pl.BlockSpec((tm, D), lambda i: (i, 0)),
                  pl.BlockSpec((1, D), lambda i: (0, 0)),
                  pl.BlockSpec((D, A_W), lambda i: (0, 1 if tail else 0), pipeline_mode=pl.Buffered(1))],
        out_specs=pl.BlockSpec((tm, n_out), lambda i: (i, 0)),
        out_shape=jax.ShapeDtypeStruct((M, n_out), F32),
        scratch_shapes=[pltpu.VMEM((D, B_W), BF16)] if tail else [],
        compiler_params=_cparams(("arbitrary",)),
        name="inproj",
    )(x2d, nw, w_bf16)


def _gla_group(gi, q_ref, k_ref, v_ref, g_ref, sm_ref, s_in, s_out, sel_ref, lvl_ref, w2_ref, bg_ref,
               gn_ref, o_ref, *, nb, c, nlev):
    R = nb * c
    sm = sm_ref[gi]
    sm_hi = sm.astype(BF16)
    sm_lo = (sm - sm_hi.astype(F32)).astype(BF16)
    zg = _dot(jnp.concatenate([sm_hi, sm_lo, sm_hi], axis=1), w2_ref[...]) + bg_ref[...]
    lg2 = -_softplus(-zg) * (-NEG_LOG2E / GLA_GATE_NORMALIZER)
    E = _dot_sel(sel_ref[...], lg2)
    eb = jnp.exp2(E[0:R])
    q = q_ref[gi] * (GLA_HEAD_K ** -0.5)
    k = k_ref[gi]
    qe = q * eb
    ke = k * jnp.exp2(E[R:2 * R])
    ql, kl = [q.astype(BF16)], [k.astype(BF16)]
    for l in range(nlev):
        p = jnp.exp2(E[(2 + l) * R:(3 + l) * R])
        ql.append((q * p).astype(BF16))
        kl.append((k * p).astype(BF16))
    vb = v_ref[gi].astype(BF16)
    g = g_ref[gi]
    gn = gn_ref[...]
    rows = lax.broadcasted_iota(jnp.int32, (R, 1), 0)

    zk = jnp.zeros((R, GLA_HEAD_K), BF16)
    zv = jnp.zeros((R, GLA_HEAD_V), BF16)
    o_intra = []
    for hp in range(GLA_HEADS // 2):
        k0 = slice(2 * hp * GLA_HEAD_K, (2 * hp + 1) * GLA_HEAD_K)
        k1 = slice((2 * hp + 1) * GLA_HEAD_K, (2 * hp + 2) * GLA_HEAD_K)
        k01 = slice(2 * hp * GLA_HEAD_K, (2 * hp + 2) * GLA_HEAD_K)
        att2 = jnp.zeros((R, 2 * R), F32)
        for l in range(nlev + 1):
            kbd = jnp.concatenate([jnp.concatenate([kl[l][:, k0], zk], axis=1),
                                   jnp.concatenate([zk, kl[l][:, k1]], axis=1)], axis=0)
            att2 = att2 + _dot_nt(ql[l][:, k01], kbd) * lvl_ref[l]
        v0 = vb[:, 2 * hp * GLA_HEAD_V:(2 * hp + 1) * GLA_HEAD_V]
        v1 = vb[:, (2 * hp + 1) * GLA_HEAD_V:(2 * hp + 2) * GLA_HEAD_V]
        vbd = jnp.concatenate([jnp.concatenate([v0, zv], axis=1),
                               jnp.concatenate([zv, v1], axis=1)], axis=0)
        o2 = _dot(att2.astype(BF16), vbd)
        o_intra += [o2[:, 0:GLA_HEAD_V], o2[:, GLA_HEAD_V:2 * GLA_HEAD_V]]

    for h in range(GLA_HEADS):
        ks = slice(h * GLA_HEAD_K, (h + 1) * GLA_HEAD_K)
        vs = slice(h * GLA_HEAD_V, (h + 1) * GLA_HEAD_V)
        o = o_intra[h]
        for s in range(nb):
            if nb > 1:
                rm = jnp.logical_and(rows >= s * c, rows < (s + 1) * c).astype(F32)
                qs = (qe[:, ks] * rm).astype(BF16)
                kd = (ke[:, ks] * rm).astype(BF16)
            else:
                qs = qe[:, ks].astype(BF16)
                kd = ke[:, ks].astype(BF16)
            S = s_in[gi * nb + s, h]
            o = o + _dot(qs, S.astype(BF16))
            upd = _dot_tn(kd, vb[:, vs])
            d = eb[s * c + c - 1:s * c + c, ks]
            dcol = jnp.transpose(jnp.broadcast_to(d, (GLA_HEAD_K, GLA_HEAD_K)))
            s_out[gi * nb + s, h] = jnp.concatenate([dcol, dcol], axis=1) * S + upd
        gh = g[:, vs]
        o_ref[gi, :, vs] = (_rms(o, gn) * _silu(gh)).astype(BF16)


def _conv_taps(u8, prev8, w, bias, width, chained):
    sub = lax.broadcasted_iota(jnp.int32, (1, SUBLANES, 1), 1)
    acc = bias + w[width - 1:width] * u8
    for j in range(width - 1):
        d = width - 1 - j
        r = pltpu.roll(u8, d, axis=1)
        pr = pltpu.roll(prev8, d, axis=1)
        if chained:
            pr = jnp.concatenate([pr, r[:-1]], axis=0)
        acc = acc + w[j:j + 1] * jnp.where(sub < d, pr, r)
    return acc


def _ssd_group(gi, z_ref, xs_ref, bc_ref, sm_ref, sel_ref, cm_ref, ex_ref, cw_ref, cb_ref, dtb_ref,
               alog_ref, de_ref, nw_ref, y_ref, ht_scr, cx_scr, cbc_scr, *, nb, c, out_lane0=0):
    R = nb * c
    sq = slice(gi * nb, (gi + 1) * nb)
    cw = cw_ref[...]
    cbias = cb_ref[...]
    nt = R // SUBLANES
    chained = nb == 1
    xraw = xs_ref[gi].reshape(nt, SUBLANES, SSD_WIDTH)
    bcraw = bc_ref[gi].reshape(nt, SUBLANES, SSD_BC)
    xs = _silu(_conv_taps(xraw, cx_scr[sq], cw[:, 0:SSD_WIDTH], cbias[:, 0:SSD_WIDTH], SSD_CONV, chained))
    bca = _silu(_conv_taps(bcraw, cbc_scr[sq], cw[:, SSD_WIDTH:SSD_CONV_CH],
                           cbias[:, SSD_WIDTH:SSD_CONV_CH], SSD_CONV, chained))
    cx_scr[sq] = xraw[nt - nb:nt]
    cbc_scr[sq] = bcraw[nt - nb:nt]
    xs = xs.reshape(R, SSD_WIDTH)
    bca = bca.reshape(R, SSD_BC)
    Bm = bca[:, 0:SSD_GROUPS * SSD_STATE]
    Cm = bca[:, SSD_GROUPS * SSD_STATE:SSD_BC]

    dt = _softplus(sm_ref[gi] + dtb_ref[...])
    la2 = dt * (NEG_LOG2E * jnp.exp(alog_ref[...]))
    cs = _dot_sel(sel_ref[0:2 * R], la2)
    cum = cs[0:R]
    stack = jnp.concatenate([cs, dt], axis=0)
    st_e = _dot(jnp.concatenate(_split3(stack), axis=1), ex_ref[...])
    cum_e = st_e[0:R]
    lmc_e = st_e[R:2 * R]
    dt_e = st_e[2 * R:3 * R]
    ecum_e = jnp.exp2(cum_e)
    xdt = xs * dt_e
    xw = (xdt * jnp.exp2(lmc_e)).astype(BF16)
    Bb = Bm.astype(BF16)
    Cb = Cm.astype(BF16)

    cumT2 = jnp.transpose(jnp.concatenate([cum, cum], axis=0))
    cmask2 = cm_ref[...] > 0.5
    rows = lax.broadcasted_iota(jnp.int32, (R, 1), 0)
    lane = lax.broadcasted_iota(jnp.int32, (1, 2 * R), 1)
    first = lane < R
    lo_half = lax.broadcasted_iota(jnp.int32, (1, 128), 1) < SSD_HEADDIM

    hpg = SSD_HEADS // SSD_GROUPS
    for hg in range(SSD_GROUPS):
        ns = slice(hg * SSD_STATE, (hg + 1) * SSD_STATE)
        gs = slice(hg * SSD_GROUP_WIDTH, (hg + 1) * SSD_GROUP_WIDTH)
        bgrp = Bb[:, ns]
        cb2 = _dot_nt(Cb[:, ns], jnp.concatenate([bgrp, bgrp], axis=0))
        yg = []
        for j in range(hpg // 2):
            ls = slice(hg * SSD_GROUP_WIDTH + j * 128, hg * SSD_GROUP_WIDTH + (j + 1) * 128)
            h0 = DTR_OFF + hg * hpg + 2 * j
            col = jnp.where(first, cum[:, h0:h0 + 1], cum[:, h0 + 1:h0 + 2])
            row = jnp.where(first, cumT2[h0:h0 + 1, :], cumT2[h0 + 1:h0 + 2, :])
            m2 = cb2 * jnp.exp2(jnp.where(cmask2, col - row, -jnp.inf))
            xpair = xdt[:, ls]
            x2 = jnp.concatenate([jnp.where(lo_half, xpair, 0.0).astype(BF16),
                                  jnp.where(lo_half, 0.0, xpair).astype(BF16)], axis=0)
            yg.append(_dot(m2.astype(BF16), x2))
        y_intra = jnp.concatenate(yg, axis=1)
        y_inter = jnp.zeros((R, SSD_GROUP_WIDTH), F32)
        for s in range(nb):
            if nb > 1:
                rm = jnp.logical_and(rows >= s * c, rows < (s + 1) * c).astype(F32)
                cg = (Cm[:, ns] * rm).astype(BF16)
                bg_ = (Bm[:, ns] * rm).astype(BF16)
            else:
                cg = Cb[:, ns]
                bg_ = Bb[:, ns]
            hT = ht_scr[gi * nb + s, :, gs]
            y_inter = y_inter + _dot(cg, hT.astype(BF16))
            upd = _dot_tn(bg_, xw[:, gs])
            dl = ecum_e[s * c + c - 1:s * c + c, gs]
            ht_scr[gi * nb + s, :, gs] = dl * hT + upd
        y = y_intra + y_inter * ecum_e[:, gs] + de_ref[:, gs] * xs[:, gs]
        y = y * _silu(z_ref[gi, :, gs])
        os_ = slice(out_lane0 + hg * SSD_GROUP_WIDTH, out_lane0 + (hg + 1) * SSD_GROUP_WIDTH)
        y_ref[gi, :, os_] = _rms(y, nw_ref[:, gs]).astype(BF16)


def _group_view(a, G):
    return a.reshape(G, a.shape[0] // G, a.shape[1])


def _mixer_kernel(q_ref, k_ref, v_ref, g_ref, z_ref, xs_ref, bc_ref, sm_ref, s0_ref, cst_ref, h0_ref,
                  sel_ref, lvl_ref, w2_ref, bg_ref, gn_ref, cm_ref, ex_ref, cw_ref, cb_ref, dtb_ref,
                  alog_ref, de_ref, nw_ref, mix_ref, sn_ref, hn_ref, *scratch, ns, nb, c, nlev):
    ci = pl.program_id(1)
    if len(scratch) == 4:
        s_scr, ht_scr, cx_scr, cbc_scr = scratch
        s_in = s_out = s_scr
    else:
        ht_scr, cx_scr, cbc_scr = scratch
        s_scr, s_in, s_out = None, s0_ref, sn_ref

    @pl.when(ci == 0)
    def _():
        if s_scr is not None:
            s_scr[...] = s0_ref[...]
        cx_scr[...] = jnp.zeros_like(cx_scr)
        cbc_scr[...] = jnp.zeros_like(cbc_scr)
        cx_scr[:, SUBLANES - (SSD_CONV - 1):SUBLANES, :] = cst_ref[:, :, 0:SSD_WIDTH]
        cbc_scr[:, SUBLANES - (SSD_CONV - 1):SUBLANES, :] = cst_ref[:, :, SSD_WIDTH:SSD_CONV_CH]
        for s in range(ns * nb):
            ht_scr[s] = jnp.transpose(h0_ref[s].reshape(SSD_WIDTH, SSD_STATE))

    for gi in range(ns):
        _gla_group(gi, q_ref, k_ref, v_ref, g_ref, sm_ref, s_in, s_out, sel_ref, lvl_ref, w2_ref, bg_ref,
                   gn_ref, mix_ref, nb=nb, c=c, nlev=nlev)
        _ssd_group(gi, z_ref, xs_ref, bc_ref, sm_ref, sel_ref, cm_ref, ex_ref, cw_ref, cb_ref, dtb_ref,
                   alog_ref, de_ref, nw_ref, mix_ref, ht_scr, cx_scr, cbc_scr, nb=nb, c=c,
                   out_lane0=GLA_WIDTH)

    @pl.when(ci == pl.num_programs(1) - 1)
    def _():
        if s_scr is not None:
            sn_ref[...] = s_scr[...]
        for s in range(ns * nb):
            hn_ref[s] = jnp.transpose(ht_scr[s]).reshape(SSD_HEADS, SSD_HEADDIM, SSD_STATE)


def _mixer(proj_a, proj_b, s_gla, s_conv, s_ssd, p, *, B, L, nb, c, ns):
    R = nb * c
    G = B // nb
    Lg = L * nb
    ncl = Lg // R
    nlev = int(np.log2(c))
    assert 2 * R == 128, "two heads share one 128-lane tile"
    assert c % SUBLANES == 0 and (nb == 1 or c == SUBLANES), "conv carry is one sublane tile per sequence"
    sel, lvl, causal = _chunk_consts(nb, c)
    sel = jnp.asarray(np.tile(sel, (1, 3)), BF16)
    lvl = jnp.asarray(np.tile(lvl, (1, 1, 2)), F32)
    causal = jnp.asarray(np.tile(causal, (1, 2)), F32)
    exn = np.zeros((SMALL_W, SSD_WIDTH), np.float32)
    for h in range(SSD_HEADS):
        exn[DTR_OFF + h, h * SSD_HEADDIM:(h + 1) * SSD_HEADDIM] = 1.0
    ex = jnp.asarray(np.tile(exn, (3, 1)), BF16)
    M = proj_a.shape[0]
    pa, pb = _group_view(proj_a, G), _group_view(proj_b, G)

    def rowblk(w, col):
        return pl.BlockSpec((ns, R, w), lambda bi, ci: (bi, ci, col // w))

    const2 = lambda a: pl.BlockSpec(a.shape, lambda bi, ci: (0,) * a.ndim)
    nsq = ns * nb
    sshape = (nsq, GLA_HEADS, GLA_HEAD_K, GLA_HEAD_V)
    sspec = pl.BlockSpec(sshape, lambda bi, ci: (bi, 0, 0, 0))
    hspec = pl.BlockSpec((nsq, SSD_HEADS, SSD_HEADDIM, SSD_STATE), lambda bi, ci: (bi, 0, 0, 0))
    consts = (sel, lvl, p['w2cat'], p['bg'], p['gn'], causal, ex, p['ssd_cw'], p['ssd_cb'], p['dtb'],
              p['alog'], p['de'], p['ssd_nw'])
    kern = functools.partial(_mixer_kernel, ns=ns, nb=nb, c=c, nlev=nlev)
    mix, s_new, h_new = pl.pallas_call(
        kern,
        grid=(G // ns, ncl),
        in_specs=[rowblk(GLA_KDIM, A_Q), rowblk(GLA_KDIM, A_K), rowblk(GLA_WIDTH, A_V),
                  rowblk(GLA_WIDTH, A_G), rowblk(SSD_WIDTH, B_Z), rowblk(SSD_WIDTH, B_XS),
                  rowblk(SSD_BC, B_BC), rowblk(SMALL_W, B_SMALL), sspec,
                  pl.BlockSpec((nsq, SSD_CONV - 1, SSD_CONV_CH), lambda bi, ci: (bi, 0, 0)), hspec]
                 + [const2(a) for a in consts],
        out_specs=[pl.BlockSpec((ns, R, GLA_WIDTH + SSD_WIDTH), lambda bi, ci: (bi, ci, 0)), sspec, hspec],
        out_shape=[jax.ShapeDtypeStruct((G, Lg, GLA_WIDTH + SSD_WIDTH), BF16),
                   jax.ShapeDtypeStruct((B, GLA_HEADS, GLA_HEAD_K, GLA_HEAD_V), F32),
                   jax.ShapeDtypeStruct((B, SSD_HEADS, SSD_HEADDIM, SSD_STATE), F32)],
        scratch_shapes=([pltpu.VMEM(sshape, F32)] if ncl > 1 else []) + [
            pltpu.VMEM((nsq, SSD_STATE, SSD_WIDTH), F32),
            pltpu.VMEM((nsq, SUBLANES, SSD_WIDTH), F32),
            pltpu.VMEM((nsq, SUBLANES, SSD_BC), F32)],
        compiler_params=_cparams(("parallel", "arbitrary")),
        name="mixer",
    )(pa, pa, pa, pa, pb, pb, pb, pb, s_gla, s_conv, s_ssd, *consts)
    return mix.reshape(M, GLA_WIDTH + SSD_WIDTH), s_new, h_new


def _outproj_kernel(m_ref, x_ref, w_ref, npost_ref, npre_ref, h_ref, xn_ref):
    h = x_ref[...] + _rms(_dot(m_ref[...], w_ref[...]), npost_ref[...])
    h_ref[...] = h
    xn_ref[...] = _rms(h, npre_ref[...]).astype(BF16)


def _outproj(mix, x2d, w_out_bf16, npost, npre, tm):
    M, D = x2d.shape
    W = w_out_bf16.shape[0]
    return pl.pallas_call(
        _outproj_kernel,
        grid=(M // tm,),
        in_specs=[pl.BlockSpec((tm, W), lambda i: (i, 0)),
                  pl.BlockSpec((tm, D), lambda i: (i, 0)),
                  pl.BlockSpec((W, D), lambda i: (0, 0)),
                  pl.BlockSpec((1, D), lambda i: (0, 0)),
                  pl.BlockSpec((1, D), lambda i: (0, 0))],
        out_specs=[pl.BlockSpec((tm, D), lambda i: (i, 0)),
                   pl.BlockSpec((tm, D), lambda i: (i, 0))],
        out_shape=[jax.ShapeDtypeStruct((M, D), F32), jax.ShapeDtypeStruct((M, D), BF16)],
        compiler_params=_cparams(("parallel",)),
        name="outproj",
    )(mix, x2d, w_out_bf16, npost, npre)


FFN_HDR = SUBLANES
FFN_TF = 512
FFN_RBLK = 64
FFN_SPLIT = 2


def _ffn_kernel(xn_ref, h_ref, sa_ref, sb_ref, wa_ref, wb_ref, cwa_ref, cwb_ref, cba_ref, cbb_ref,
                wo_ref, npost_ref, y_ref, na_ref, nb_ref, ua_scr, ub_scr, act_scr, *carry,
                nseq, L, tps):
    i = pl.program_id(0)
    j = pl.program_id(1)
    tf = wa_ref.shape[1]
    W1 = FFN_CONV - 1
    H = FFN_HDR
    first = i % tps == 0

    if tps > 1:
        ca_scr, cb_scr = carry

        @pl.when(jnp.logical_not(first))
        def _():
            ua_scr[:, 0:H, :] = ca_scr[j]
            ub_scr[:, 0:H, :] = cb_scr[j]

    @pl.when(first)
    def _():
        ua_scr[:, H - W1:H, :] = sa_ref[...]
        ub_scr[:, H - W1:H, :] = sb_ref[...]

    @pl.when(j == 0)
    def _():
        y_ref[...] = jnp.zeros_like(y_ref)

    if nseq == 1:
        parts = [(0, 1, k * L // FFN_SPLIT, (k + 1) * L // FFN_SPLIT) for k in range(FFN_SPLIT)]
    else:
        parts = [(k * nseq // FFN_SPLIT, (k + 1) * nseq // FFN_SPLIT, 0, L) for k in range(FFN_SPLIT)]
    tile_rows = lambda q0, q1, r0, r1: slice(q0 * L + r0, (q1 - 1) * L + r1)

    for (q0, q1, r0, r1) in parts:
        rs = tile_rows(q0, q1, r0, r1)
        ua_scr[q0:q1, H + r0:H + r1, :] = _dot(xn_ref[rs, :], wa_ref[...]).reshape(q1 - q0, r1 - r0, tf)
        ub_scr[q0:q1, H + r0:H + r1, :] = _dot(xn_ref[rs, :], wb_ref[...]).reshape(q1 - q0, r1 - r0, tf)

    spread = lambda row: jnp.broadcast_to(row, (SUBLANES, tf))
    wa8 = [spread(cwa_ref[t:t + 1, :]) for t in range(FFN_CONV)]
    wb8 = [spread(cwb_ref[t:t + 1, :]) for t in range(FFN_CONV)]
    ba8 = spread(cba_ref[...])
    bb8 = spread(cbb_ref[...])
    for (q0, q1, r0, r1) in parts:
        if nseq == 1:
            blocks = [(0, 1, r, r + FFN_RBLK) for r in range(r0, r1, FFN_RBLK)]
        else:
            qb = FFN_RBLK // L
            blocks = [(q, q + qb, 0, L) for q in range(q0, q1, qb)]
        for (bq0, bq1, br0, br1) in blocks:
            def conv(u_scr, w8, b8):
                tap = lambda d: u_scr[bq0:bq1, H - d + br0:H - d + br1, :].reshape(-1, SUBLANES, tf)
                out = b8 + w8[W1] * tap(0)
                for t in range(W1):
                    out = out + w8[t] * tap(W1 - t)
                return out
            a = conv(ua_scr, wa8, ba8)
            b = conv(ub_scr, wb8, bb8)
            row0 = bq0 * L + br0
            act_scr[row0:row0 + FFN_RBLK, :] = (_silu(a) * b).reshape(FFN_RBLK, tf).astype(BF16)
        rs = tile_rows(q0, q1, r0, r1)
        y_ref[rs, :] += _dot(act_scr[rs, :], wo_ref[...])

    na_ref[...] = ua_scr[:, H + L - W1:H + L, :]
    nb_ref[...] = ub_scr[:, H + L - W1:H + L, :]
    if tps > 1:
        ca_scr[j] = ua_scr[:, L:L + H, :]
        cb_scr[j] = ub_scr[:, L:L + H, :]

    @pl.when(j == pl.num_programs(1) - 1)
    def _():
        y_ref[...] = h_ref[...] + _rms(y_ref[...], npost_ref[...])


def _ffn(xn2, h2d, st, w_in_bf16, cw, cb, w_out_bf16, npost, *, nseq, L, tps):
    M, D = h2d.shape
    F = w_out_bf16.shape[0]
    tf = FFN_TF
    nj = F // tf
    tm = nseq * L
    W1 = FFN_CONV - 1
    assert (L if nseq == 1 else nseq * L) % (FFN_SPLIT * FFN_RBLK) == 0
    kern = functools.partial(_ffn_kernel, nseq=nseq, L=L, tps=tps)
    stspec_a = pl.BlockSpec((nseq, W1, tf), lambda i, j: (i // tps, 0, j))
    stspec_b = pl.BlockSpec((nseq, W1, tf), lambda i, j: (i // tps, 0, j + nj))
    return pl.pallas_call(
        kern,
        grid=(M // tm, nj),
        in_specs=[pl.BlockSpec((tm, D), lambda i, j: (i, 0)),
                  pl.BlockSpec((tm, D), lambda i, j: (i, 0)),
                  stspec_a, stspec_b,
                  pl.BlockSpec((D, tf), lambda i, j: (0, j)),
                  pl.BlockSpec((D, tf), lambda i, j: (0, j + nj)),
                  pl.BlockSpec((FFN_CONV, tf), lambda i, j: (0, j)),
                  pl.BlockSpec((FFN_CONV, tf), lambda i, j: (0, j + nj)),
                  pl.BlockSpec((1, tf), lambda i, j: (0, j)),
                  pl.BlockSpec((1, tf), lambda i, j: (0, j + nj)),
                  pl.BlockSpec((tf, D), lambda i, j: (j, 0)),
                  pl.BlockSpec((1, D), lambda i, j: (0, 0))],
        out_specs=[pl.BlockSpec((tm, D), lambda i, j: (i, 0)),
                   pl.BlockSpec((nseq, W1, tf), lambda i, j: (i, 0, j)),
                   pl.BlockSpec((nseq, W1, tf), lambda i, j: (i, 0, j))],
        out_shape=[jax.ShapeDtypeStruct((M, D), F32),
                   jax.ShapeDtypeStruct((M // L, W1, F), F32),
                   jax.ShapeDtypeStruct((M // L, W1, F), F32)],
        scratch_shapes=[pltpu.VMEM((nseq, FFN_HDR + L, tf), F32),
                        pltpu.VMEM((nseq, FFN_HDR + L, tf), F32),
                        pltpu.VMEM((tm, tf), BF16)] + (
                            [pltpu.VMEM((nj, nseq, FFN_HDR, tf), F32)] * 2 if tps > 1 else []),
        compiler_params=_cparams(("arbitrary", "arbitrary")),
        name="ffn",
    )(xn2, h2d, st, st, w_in_bf16, w_in_bf16, cw, cw, cb, cb, w_out_bf16, npost)


def _layer(x, s_gla, s_ssd, s_conv, s_ffn, p, *, nb, c, ns, ffn_nseq, ffn_L, ffn_tps):
    B, L, D = x.shape
    M = B * L
    x2d = x.reshape(M, D)
    proj_a = _inproj(x2d, p['n_mix_pre'], p['w_in'], 512, tail=False)
    proj_b = _inproj(x2d, p['n_mix_pre'], p['w_in'], 512, tail=True)
    mix, g_new, h_new = _mixer(proj_a, proj_b, s_gla, s_conv, s_ssd, p, B=B, L=L, nb=nb, c=c, ns=ns)
    hres, xn2 = _outproj(mix, x2d, p['w_out'], p['n_mix_post'], p['n_ffn_pre'], 512)
    y, fa, fb = _ffn(xn2, hres, s_ffn, p['ffn_w_in'], p['ffn_cw'], p['ffn_cb'], p['ffn_w_out'],
                     p['n_ffn_post'], nseq=ffn_nseq, L=ffn_L, tps=ffn_tps)
    c_new = proj_b.reshape(B, L, -1)[:, L - (SSD_CONV - 1):, B_XS:B_XS + SSD_CONV_CH]
    f_new = jnp.concatenate([fa, fb], axis=-1)[ffn_tps - 1::ffn_tps]
    return y.reshape(B, L, D), g_new, h_new, c_new, f_new


def _prep_params(l, norm_mix_pre, norm_mix_post, norm_ffn_pre, norm_ffn_post, w_in, gla_w_gate2,
                 gla_b_gate, gla_norm, ssd_conv_w, ssd_conv_b, ssd_dt_bias, ssd_A_log, ssd_D, ssd_norm,
                 w_out, ffn_w_in, ffn_conv_w, ffn_conv_b, ffn_w_out):
    assert w_in.shape[2] == A_W + IN_TAIL
    w2p = jnp.zeros((SMALL_W, GLA_KDIM), F32).at[LR_OFF:LR_OFF + GLA_LOWRANK].set(gla_w_gate2[l])
    w2_hi = w2p.astype(BF16)
    w2_lo = (w2p - w2_hi.astype(F32)).astype(BF16)
    w2cat = jnp.concatenate([w2_hi, w2_hi, w2_lo], axis=0)
    pad_small = lambda v: jnp.zeros((1, SMALL_W), F32).at[0, DTR_OFF:DTR_OFF + SSD_HEADS].set(v)
    row = lambda v: v.reshape(1, -1)
    return dict(
        n_mix_pre=row(norm_mix_pre[l]), n_mix_post=row(norm_mix_post[l]),
        n_ffn_pre=row(norm_ffn_pre[l]), n_ffn_post=row(norm_ffn_post[l]),
        w_in=w_in[l].astype(BF16), w2cat=w2cat, bg=row(gla_b_gate[l]), gn=row(gla_norm[l]),
        ssd_cw=ssd_conv_w[l], ssd_cb=row(ssd_conv_b[l]),
        dtb=pad_small(ssd_dt_bias[l]), alog=pad_small(ssd_A_log[l]),
        de=row(jnp.repeat(ssd_D[l], SSD_HEADDIM)), ssd_nw=row(ssd_norm[l]),
        w_out=w_out[l].astype(BF16), ffn_w_in=ffn_w_in[l].astype(BF16),
        ffn_cw=ffn_conv_w[l], ffn_cb=row(ffn_conv_b[l]), ffn_w_out=ffn_w_out[l].astype(BF16))


def kernel(x_prompt, x_sample, state_gla, state_ssd, state_ssd_conv, state_ffn_conv, norm_mix_pre,
           norm_mix_post, norm_ffn_pre, norm_ffn_post, w_in, gla_w_gate2, gla_b_gate, gla_norm,
           ssd_conv_w, ssd_conv_b, ssd_dt_bias, ssd_A_log, ssd_D, ssd_norm, w_out, ffn_w_in,
           ffn_conv_w, ffn_conv_b, ffn_w_out):
    depth = w_in.shape[0]
    xp, xs = x_prompt, x_sample
    Bp, Lp, D = xp.shape
    Bs, Ls, _ = xs.shape
    F2 = ffn_w_in.shape[2]
    outs = [[] for _ in range(8)]
    for l in range(depth):
        p = _prep_params(l, norm_mix_pre, norm_mix_post, norm_ffn_pre, norm_ffn_post, w_in,
                         gla_w_gate2, gla_b_gate, gla_norm, ssd_conv_w, ssd_conv_b, ssd_dt_bias,
                         ssd_A_log, ssd_D, ssd_norm, w_out, ffn_w_in, ffn_conv_w, ffn_conv_b, ffn_w_out)
        cp = CHUNK
        xp, g1, h1, c1, f1 = _layer(
            xp,
            jnp.zeros((Bp, GLA_HEADS, GLA_HEAD_K, GLA_HEAD_V), F32),
            jnp.zeros((Bp, SSD_HEADS, SSD_HEADDIM, SSD_STATE), F32),
            jnp.zeros((Bp, SSD_CONV - 1, SSD_CONV_CH), F32),
            jnp.zeros((Bp, FFN_CONV - 1, F2), F32),
            p, nb=1, c=cp, ns=Bp, ffn_nseq=1, ffn_L=512, ffn_tps=Lp // 512)
        nbs = CHUNK // Ls
        xs, g2, h2, c2, f2 = _layer(
            xs, state_gla[l], state_ssd[l], state_ssd_conv[l], state_ffn_conv[l],
            p, nb=nbs, c=Ls, ns=1, ffn_nseq=512 // Ls, ffn_L=Ls, ffn_tps=1)
        for lst, val in zip(outs, (g1, h1, c1, f1, g2, h2, c2, f2)):
            lst.append(val)
    return (xp, xs) + tuple(jnp.stack(o) for o in outs)
```

```python
import functools

import numpy as np
import jax
import jax.numpy as jnp
from jax import lax
from jax.experimental import pallas as pl
from jax.experimental.pallas import tpu as pltpu

F32 = jnp.float32
BF16 = jnp.bfloat16
EPS = 1e-6

GLA_HEADS = 4
GLA_HEAD_K = 128
GLA_HEAD_V = 256
GLA_KDIM = GLA_HEADS * GLA_HEAD_K
GLA_WIDTH = GLA_HEADS * GLA_HEAD_V
GLA_LOWRANK = 16
GLA_GATE_NORMALIZER = 16.0
SSD_HEADS = 16
SSD_HEADDIM = 64
SSD_STATE = 128
SSD_GROUPS = 2
SSD_WIDTH = SSD_HEADS * SSD_HEADDIM
SSD_GROUP_WIDTH = SSD_WIDTH // SSD_GROUPS
SSD_CONV = 4
SSD_BC = 2 * SSD_GROUPS * SSD_STATE
SSD_CONV_CH = SSD_WIDTH + SSD_BC
FFN_CONV = 3
CHUNK = 64

A_Q = 0
A_K = GLA_KDIM
A_V = 2 * GLA_KDIM
A_G = A_V + GLA_WIDTH
A_W = A_G + GLA_WIDTH
IN_TAIL = GLA_LOWRANK + SSD_WIDTH + SSD_CONV_CH + SSD_HEADS
B_Z = 0
B_XS = SSD_WIDTH
B_BC = B_XS + SSD_WIDTH
B_SMALL = B_BC + SSD_BC
SMALL_W = 128
DTR_OFF = 0
LR_OFF = SSD_HEADS
B_W = B_SMALL + SMALL_W

ROW_PARTS = 4
SUBLANES = 8
VMEM_LIMIT = 56 * 1024 * 1024


def _cparams(sem):
    return pltpu.CompilerParams(dimension_semantics=sem, vmem_limit_bytes=VMEM_LIMIT)


def _split3(x):
    hi = x.astype(BF16)
    r = x - hi.astype(F32)
    mid = r.astype(BF16)
    lo = (r - mid.astype(F32)).astype(BF16)
    return hi, mid, lo


def _dot(a, b):
    return jnp.dot(a, b, preferred_element_type=F32)


def _dot_nt(a, b):
    return lax.dot_general(a, b, (((1,), (1,)), ((), ())), preferred_element_type=F32)


def _dot_tn(a, b):
    return lax.dot_general(a, b, (((0,), (0,)), ((), ())), preferred_element_type=F32)


def _dot_sel(sel3_bf16, x_f32):
    return _dot(sel3_bf16, jnp.concatenate(_split3(x_f32), axis=0))


NEG_LOG2E = -1.4426950408889634


def _silu(x):
    return x / (1.0 + jnp.exp2(x * NEG_LOG2E))


def _softplus(x):
    return jnp.maximum(x, 0.0) + jnp.log1p(jnp.exp(-jnp.abs(x)))


def _rms(x, w):
    return x * lax.rsqrt(jnp.mean(x * x, axis=-1, keepdims=True) + EPS) * w


def _chunk_consts(nb, c):
    R = nb * c
    idx = np.arange(R)
    seq, pos = idx // c, idx % c
    same = seq[:, None] == seq[None, :]
    t, u = pos[:, None], pos[None, :]
    blocks = [same & (u <= t), same & (u > t)]
    masks = [np.eye(R, dtype=bool)]
    m = c // 2
    while m >= 1:
        blk = pos // (2 * m)
        rho = blk * 2 * m + m - 1
        upper = pos > rho
        a_up = upper[:, None] & (u > rho[:, None]) & (u <= t)
        a_lo = (~upper)[:, None] & (u > t) & (u <= rho[:, None])
        blocks.append(same & (a_up | a_lo))
        masks.append(same & upper[:, None] & (~upper)[None, :] & (blk[:, None] == blk[None, :]))
        m //= 2
    sel = np.concatenate(blocks, 0).astype(np.float32)
    lvl = np.stack(masks).astype(np.float32)
    causal = (same & (u <= t)).astype(np.float32)
    return sel, lvl, causal


INPROJ_RCH = 256


def _inproj_kernel(x_ref, nw_ref, w_ref, o_ref, *scratch, tail):
    if tail:
        w_scr, = scratch

        @pl.when(pl.program_id(0) == 0)
        def _():
            zpad = jnp.zeros((INPROJ_RCH, B_W - IN_TAIL), BF16)
            for r in range(0, w_ref.shape[0], INPROJ_RCH):
                blk = w_ref[r:r + INPROJ_RCH, :]
                w_scr[r:r + INPROJ_RCH, :] = jnp.concatenate(
                    [blk[:, GLA_LOWRANK:IN_TAIL], blk[:, 0:GLA_LOWRANK], zpad], axis=1)

        w = w_scr
    else:
        w = w_ref
    rp = x_ref.shape[0] // ROW_PARTS
    for k in range(ROW_PARTS):
        rs = slice(k * rp, (k + 1) * rp)
        xn = _rms(x_ref[rs, :], nw_ref[...]).astype(BF16)
        o_ref[rs, :] = _dot(xn, w[...])


def _inproj(x2d, nw, w_bf16, tm, *, tail):
    M, D = x2d.shape
    n_out = B_W if tail else A_W
    return pl.pallas_call(
        functools.partial(_inproj_kernel, tail=tail),
        grid=(M // tm,),
        in_specs=[pl.BlockSpec((tm, D), lambda i: (i, 0)),
                  pl.BlockSpec((1, D), lambda i: (0, 0)),
                  pl.BlockSpec((D, A_W), lambda i: (0, 1 if tail else 0), pipeline_mode=pl.Buffered(1))],
        out_specs=pl.BlockSpec((tm, n_out), lambda i: (i, 0)),
        out_shape=jax.ShapeDtypeStruct((M, n_out), F32),
        scratch_shapes=[pltpu.VMEM((D, B_W), BF16)] if tail else [],
        compiler_params=_cparams(("arbitrary",)),
        name="inproj",
    )(x2d, nw, w_bf16)


def _gla_group(gi, q_ref, k_ref, v_ref, g_ref, sm_ref, s_in, s_out, sel_ref, lvl_ref, w2_ref, bg_ref,
               gn_ref, o_ref, *, nb, c, nlev):
    R = nb * c
    sm = sm_ref[gi]
    sm_hi = sm.astype(BF16)
    sm_lo = (sm - sm_hi.astype(F32)).astype(BF16)
    zg = _dot(jnp.concatenate([sm_hi, sm_lo, sm_hi], axis=1), w2_ref[...]) + bg_ref[...]
    lg2 = -_softplus(-zg) * (-NEG_LOG2E / GLA_GATE_NORMALIZER)
    E = _dot_sel(sel_ref[...], lg2)
    eb = jnp.exp2(E[0:R])
    q = q_ref[gi] * (GLA_HEAD_K ** -0.5)
    k = k_ref[gi]
    qe = q * eb
    ke = k * jnp.exp2(E[R:2 * R])
    ql, kl = [q.astype(BF16)], [k.astype(BF16)]
    for l in range(nlev):
        p = jnp.exp2(E[(2 + l) * R:(3 + l) * R])
        ql.append((q * p).astype(BF16))
        kl.append((k * p).astype(BF16))
    vb = v_ref[gi].astype(BF16)
    g = g_ref[gi]
    gn = gn_ref[...]
    rows = lax.broadcasted_iota(jnp.int32, (R, 1), 0)

    zk = jnp.zeros((R, GLA_HEAD_K), BF16)
    zv = jnp.zeros((R, GLA_HEAD_V), BF16)
    o_intra = []
    for hp in range(GLA_HEADS // 2):
        k0 = slice(2 * hp * GLA_HEAD_K, (2 * hp + 1) * GLA_HEAD_K)
        k1 = slice((2 * hp + 1) * GLA_HEAD_K, (2 * hp + 2) * GLA_HEAD_K)
        k01 = slice(2 * hp * GLA_HEAD_K, (2 * hp + 2) * GLA_HEAD_K)
        att2 = jnp.zeros((R, 2 * R), F32)
        for l in range(nlev + 1):
            kbd = jnp.concatenate([jnp.concatenate([kl[l][:, k0], zk], axis=1),
                                   jnp.concatenate([zk, kl[l][:, k1]], axis=1)], axis=0)
            att2 = att2 + _dot_nt(ql[l][:, k01], kbd) * lvl_ref[l]
        v0 = vb[:, 2 * hp * GLA_HEAD_V:(2 * hp + 1) * GLA_HEAD_V]
        v1 = vb[:, (2 * hp + 1) * GLA_HEAD_V:(2 * hp + 2) * GLA_HEAD_V]
        vbd = jnp.concatenate([jnp.concatenate([v0, zv], axis=1),
                               jnp.concatenate([zv, v1], axis=1)], axis=0)
        o2 = _dot(att2.astype(BF16), vbd)
        o_intra += [o2[:, 0:GLA_HEAD_V], o2[:, GLA_HEAD_V:2 * GLA_HEAD_V]]

    for h in range(GLA_HEADS):
        ks = slice(h * GLA_HEAD_K, (h + 1) * GLA_HEAD_K)
        vs = slice(h * GLA_HEAD_V, (h + 1) * GLA_HEAD_V)
        o = o_intra[h]
        for s in range(nb):
            if nb > 1:
                rm = jnp.logical_and(rows >= s * c, rows < (s + 1) * c).astype(F32)
                qs = (qe[:, ks] * rm).astype(BF16)
                kd = (ke[:, ks] * rm).astype(BF16)
            else:
                qs = qe[:, ks].astype(BF16)
                kd = ke[:, ks].astype(BF16)
            S = s_in[gi * nb + s, h]
            o = o + _dot(qs, S.astype(BF16))
            upd = _dot_tn(kd, vb[:, vs])
            d = eb[s * c + c - 1:s * c + c, ks]
            dcol = jnp.transpose(jnp.broadcast_to(d, (GLA_HEAD_K, GLA_HEAD_K)))
            s_out[gi * nb + s, h] = jnp.concatenate([dcol, dcol], axis=1) * S + upd
        gh = g[:, vs]
        o_ref[gi, :, vs] = (_rms(o, gn) * _silu(gh)).astype(BF16)


def _conv_taps(u8, prev8, w, bias, width, chained):
    sub = lax.broadcasted_iota(jnp.int32, (1, SUBLANES, 1), 1)
    acc = bias + w[width - 1:width] * u8
    for j in range(width - 1):
        d = width - 1 - j
        r = pltpu.roll(u8, d, axis=1)
        pr = pltpu.roll(prev8, d, axis=1)
        if chained:
            pr = jnp.concatenate([pr, r[:-1]], axis=0)
        acc = acc + w[j:j + 1] * jnp.where(sub < d, pr, r)
    return acc


def _ssd_group(gi, z_ref, xs_ref, bc_ref, sm_ref, sel_ref, cm_ref, ex_ref, cw_ref, cb_ref, dtb_ref,
               alog_ref, de_ref, nw_ref, y_ref, ht_scr, cx_scr, cbc_scr, *, nb, c, out_lane0=0):
    R = nb * c
    sq = slice(gi * nb, (gi + 1) * nb)
    cw = cw_ref[...]
    cbias = cb_ref[...]
    nt = R // SUBLANES
    chained = nb == 1
    xraw = xs_ref[gi].reshape(nt, SUBLANES, SSD_WIDTH)
    bcraw = bc_ref[gi].reshape(nt, SUBLANES, SSD_BC)
    xs = _silu(_conv_taps(xraw, cx_scr[sq], cw[:, 0:SSD_WIDTH], cbias[:, 0:SSD_WIDTH], SSD_CONV, chained))
    bca = _silu(_conv_taps(bcraw, cbc_scr[sq], cw[:, SSD_WIDTH:SSD_CONV_CH],
                           cbias[:, SSD_WIDTH:SSD_CONV_CH], SSD_CONV, chained))
    cx_scr[sq] = xraw[nt - nb:nt]
    cbc_scr[sq] = bcraw[nt - nb:nt]
    xs = xs.reshape(R, SSD_WIDTH)
    bca = bca.reshape(R, SSD_BC)
    Bm = bca[:, 0:SSD_GROUPS * SSD_STATE]
    Cm = bca[:, SSD_GROUPS * SSD_STATE:SSD_BC]

    dt = _softplus(sm_ref[gi] + dtb_ref[...])
    la2 = dt * (NEG_LOG2E * jnp.exp(alog_ref[...]))
    cs = _dot_sel(sel_ref[0:2 * R], la2)
    cum = cs[0:R]
    stack = jnp.concatenate([cs, dt], axis=0)
    st_e = _dot(jnp.concatenate(_split3(stack), axis=1), ex_ref[...])
    cum_e = st_e[0:R]
    lmc_e = st_e[R:2 * R]
    dt_e = st_e[2 * R:3 * R]
    ecum_e = jnp.exp2(cum_e)
    xdt = xs * dt_e
    xw = (xdt * jnp.exp2(lmc_e)).astype(BF16)
    Bb = Bm.astype(BF16)
    Cb = Cm.astype(BF16)

    cumT2 = jnp.transpose(jnp.concatenate([cum, cum], axis=0))
    cmask2 = cm_ref[...] > 0.5
    rows = lax.broadcasted_iota(jnp.int32, (R, 1), 0)
    lane = lax.broadcasted_iota(jnp.int32, (1, 2 * R), 1)
    first = lane < R
    lo_half = lax.broadcasted_iota(jnp.int32, (1, 128), 1) < SSD_HEADDIM

    hpg = SSD_HEADS // SSD_GROUPS
    for hg in range(SSD_GROUPS):
        ns = slice(hg * SSD_STATE, (hg + 1) * SSD_STATE)
        gs = slice(hg * SSD_GROUP_WIDTH, (hg + 1) * SSD_GROUP_WIDTH)
        bgrp = Bb[:, ns]
        cb2 = _dot_nt(Cb[:, ns], jnp.concatenate([bgrp, bgrp], axis=0))
        yg = []
        for j in range(hpg // 2):
            ls = slice(hg * SSD_GROUP_WIDTH + j * 128, hg * SSD_GROUP_WIDTH + (j + 1) * 128)
            h0 = DTR_OFF + hg * hpg + 2 * j
            col = jnp.where(first, cum[:, h0:h0 + 1], cum[:, h0 + 1:h0 + 2])
            row = jnp.where(first, cumT2[h0:h0 + 1, :], cumT2[h0 + 1:h0 + 2, :])
            m2 = cb2 * jnp.exp2(jnp.where(cmask2, col - row, -jnp.inf))
            xpair = xdt[:, ls]
            x2 = jnp.concatenate([jnp.where(lo_half, xpair, 0.0).astype(BF16),
                                  jnp.where(lo_half, 0.0, xpair).astype(BF16)], axis=0)
            yg.append(_dot(m2.astype(BF16), x2))
        y_intra = jnp.concatenate(yg, axis=1)
        y_inter = jnp.zeros((R, SSD_GROUP_WIDTH), F32)
        for s in range(nb):
            if nb > 1:
                rm = jnp.logical_and(rows >= s * c, rows < (s + 1) * c).astype(F32)
                cg = (Cm[:, ns] * rm).astype(BF16)
                bg_ = (Bm[:, ns] * rm).astype(BF16)
            else:
                cg = Cb[:, ns]
                bg_ = Bb[:, ns]
            hT = ht_scr[gi * nb + s, :, gs]
            y_inter = y_inter + _dot(cg, hT.astype(BF16))
            upd = _dot_tn(bg_, xw[:, gs])
            dl = ecum_e[s * c + c - 1:s * c + c, gs]
            ht_scr[gi * nb + s, :, gs] = dl * hT + upd
        y = y_intra + y_inter * ecum_e[:, gs] + de_ref[:, gs] * xs[:, gs]
        y = y * _silu(z_ref[gi, :, gs])
        os_ = slice(out_lane0 + hg * SSD_GROUP_WIDTH, out_lane0 + (hg + 1) * SSD_GROUP_WIDTH)
        y_ref[gi, :, os_] = _rms(y, nw_ref[:, gs]).astype(BF16)


def _group_view(a, G):
    return a.reshape(G, a.shape[0] // G, a.shape[1])


def _mixer_kernel(q_ref, k_ref, v_ref, g_ref, z_ref, xs_ref, bc_ref, sm_ref, s0_ref, cst_ref, h0_ref,
                  sel_ref, lvl_ref, w2_ref, bg_ref, gn_ref, cm_ref, ex_ref, cw_ref, cb_ref, dtb_ref,
                  alog_ref, de_ref, nw_ref, mix_ref, sn_ref, hn_ref, *scratch, ns, nb, c, nlev):
    ci = pl.program_id(1)
    if len(scratch) == 4:
        s_scr, ht_scr, cx_scr, cbc_scr = scratch
        s_in = s_out = s_scr
    else:
        ht_scr, cx_scr, cbc_scr = scratch
        s_scr, s_in, s_out = None, s0_ref, sn_ref

    @pl.when(ci == 0)
    def _():
        if s_scr is not None:
            s_scr[...] = s0_ref[...]
        cx_scr[...] = jnp.zeros_like(cx_scr)
        cbc_scr[...] = jnp.zeros_like(cbc_scr)
        cx_scr[:, SUBLANES - (SSD_CONV - 1):SUBLANES, :] = cst_ref[:, :, 0:SSD_WIDTH]
        cbc_scr[:, SUBLANES - (SSD_CONV - 1):SUBLANES, :] = cst_ref[:, :, SSD_WIDTH:SSD_CONV_CH]
        for s in range(ns * nb):
            ht_scr[s] = jnp.transpose(h0_ref[s].reshape(SSD_WIDTH, SSD_STATE))

    for gi in range(ns):
        _gla_group(gi, q_ref, k_ref, v_ref, g_ref, sm_ref, s_in, s_out, sel_ref, lvl_ref, w2_ref, bg_ref,
                   gn_ref, mix_ref, nb=nb, c=c, nlev=nlev)
        _ssd_group(gi, z_ref, xs_ref, bc_ref, sm_ref, sel_ref, cm_ref, ex_ref, cw_ref, cb_ref, dtb_ref,
                   alog_ref, de_ref, nw_ref, mix_ref, ht_scr, cx_scr, cbc_scr, nb=nb, c=c,
                   out_lane0=GLA_WIDTH)

    @pl.when(ci == pl.num_programs(1) - 1)
    def _():
        if s_scr is not None:
            sn_ref[...] = s_scr[...]
        for s in range(ns * nb):
            hn_ref[s] = jnp.transpose(ht_scr[s]).reshape(SSD_HEADS, SSD_HEADDIM, SSD_STATE)


def _mixer(proj_a, proj_b, s_gla, s_conv, s_ssd, p, *, B, L, nb, c, ns):
    R = nb * c
    G = B // nb
    Lg = L * nb
    ncl = Lg // R
    nlev = int(np.log2(c))
    assert 2 * R == 128, "two heads share one 128-lane tile"
    assert c % SUBLANES == 0 and (nb == 1 or c == SUBLANES), "conv carry is one sublane tile per sequence"
    sel, lvl, causal = _chunk_consts(nb, c)
    sel = jnp.asarray(np.tile(sel, (1, 3)), BF16)
    lvl = jnp.asarray(np.tile(lvl, (1, 1, 2)), F32)
    causal = jnp.asarray(np.tile(causal, (1, 2)), F32)
    exn = np.zeros((SMALL_W, SSD_WIDTH), np.float32)
    for h in range(SSD_HEADS):
        exn[DTR_OFF + h, h * SSD_HEADDIM:(h + 1) * SSD_HEADDIM] = 1.0
    ex = jnp.asarray(np.tile(exn, (3, 1)), BF16)
    M = proj_a.shape[0]
    pa, pb = _group_view(proj_a, G), _group_view(proj_b, G)

    def rowblk(w, col):
        return pl.BlockSpec((ns, R, w), lambda bi, ci: (bi, ci, col // w))

    const2 = lambda a: pl.BlockSpec(a.shape, lambda bi, ci: (0,) * a.ndim)
    nsq = ns * nb
    sshape = (nsq, GLA_HEADS, GLA_HEAD_K, GLA_HEAD_V)
    sspec = pl.BlockSpec(sshape, lambda bi, ci: (bi, 0, 0, 0))
    hspec = pl.BlockSpec((nsq, SSD_HEADS, SSD_HEADDIM, SSD_STATE), lambda bi, ci: (bi, 0, 0, 0))
    consts = (sel, lvl, p['w2cat'], p['bg'], p['gn'], causal, ex, p['ssd_cw'], p['ssd_cb'], p['dtb'],
              p['alog'], p['de'], p['ssd_nw'])
    kern = functools.partial(_mixer_kernel, ns=ns, nb=nb, c=c, nlev=nlev)
    mix, s_new, h_new = pl.pallas_call(
        kern,
        grid=(G // ns, ncl),
        in_specs=[rowblk(GLA_KDIM, A_Q), rowblk(GLA_KDIM, A_K), rowblk(GLA_WIDTH, A_V),
                  rowblk(GLA_WIDTH, A_G), rowblk(SSD_WIDTH, B_Z), rowblk(SSD_WIDTH, B_XS),
                  rowblk(SSD_BC, B_BC), rowblk(SMALL_W, B_SMALL), sspec,
                  pl.BlockSpec((nsq, SSD_CONV - 1, SSD_CONV_CH), lambda bi, ci: (bi, 0, 0)), hspec]
                 + [const2(a) for a in consts],
        out_specs=[pl.BlockSpec((ns, R, GLA_WIDTH + SSD_WIDTH), lambda bi, ci: (bi, ci, 0)), sspec, hspec],
        out_shape=[jax.ShapeDtypeStruct((G, Lg, GLA_WIDTH + SSD_WIDTH), BF16),
                   jax.ShapeDtypeStruct((B, GLA_HEADS, GLA_HEAD_K, GLA_HEAD_V), F32),
                   jax.ShapeDtypeStruct((B, SSD_HEADS, SSD_HEADDIM, SSD_STATE), F32)],
        scratch_shapes=([pltpu.VMEM(sshape, F32)] if ncl > 1 else []) + [
            pltpu.VMEM((nsq, SSD_STATE, SSD_WIDTH), F32),
            pltpu.VMEM((nsq, SUBLANES, SSD_WIDTH), F32),
            pltpu.VMEM((nsq, SUBLANES, SSD_BC), F32)],
        compiler_params=_cparams(("parallel", "arbitrary")),
        name="mixer",
    )(pa, pa, pa, pa, pb, pb, pb, pb, s_gla, s_conv, s_ssd, *consts)
    return mix.reshape(M, GLA_WIDTH + SSD_WIDTH), s_new, h_new


def _outproj_kernel(m_ref, x_ref, w_ref, npost_ref, npre_ref, h_ref, xn_ref):
    rp = x_ref.shape[0] // ROW_PARTS
    for k in range(ROW_PARTS):
        rs = slice(k * rp, (k + 1) * rp)
        h = x_ref[rs, :] + _rms(_dot(m_ref[rs, :], w_ref[...]), npost_ref[...])
        h_ref[rs, :] = h
        xn_ref[rs, :] = _rms(h, npre_ref[...]).astype(BF16)


def _outproj(mix, x2d, w_out_bf16, npost, npre, tm):
    M, D = x2d.shape
    W = w_out_bf16.shape[0]
    return pl.pallas_call(
        _outproj_kernel,
        grid=(M // tm,),
        in_specs=[pl.BlockSpec((tm, W), lambda i: (i, 0)),
                  pl.BlockSpec((tm, D), lambda i: (i, 0)),
                  pl.BlockSpec((W, D), lambda i: (0, 0)),
                  pl.BlockSpec((1, D), lambda i: (0, 0)),
                  pl.BlockSpec((1, D), lambda i: (0, 0))],
        out_specs=[pl.BlockSpec((tm, D), lambda i: (i, 0)),
                   pl.BlockSpec((tm, D), lambda i: (i, 0))],
        out_shape=[jax.ShapeDtypeStruct((M, D), F32), jax.ShapeDtypeStruct((M, D), BF16)],
        compiler_params=_cparams(("parallel",)),
        name="outproj",
    )(mix, x2d, w_out_bf16, npost, npre)


FFN_HDR = SUBLANES
FFN_TF = 512
FFN_RBLK = 64
FFN_PART = 256


def _ffn_kernel(xn_ref, h_hbm, sa_ref, sb_ref, wa_ref, wb_ref, cwa_ref, cwb_ref, cba_ref, cbb_ref,
                wo_ref, npost_ref, y_ref, na_ref, nb_ref, ua_scr, ub_scr, act_scr, h_buf, h_sem, *carry,
                nseq, L, tps):
    i = pl.program_id(0)
    j = pl.program_id(1)
    tf = wa_ref.shape[1]
    tm = nseq * L
    nparts = tm // FFN_PART
    W1 = FFN_CONV - 1
    H = FFN_HDR
    first = i % tps == 0

    def h_copy():
        return pltpu.make_async_copy(h_hbm.at[pl.ds(i * tm, tm), :], h_buf, h_sem)

    @pl.when(j == 0)
    def _():
        h_copy().start()

    if tps > 1:
        ca_scr, cb_scr = carry

        @pl.when(jnp.logical_not(first))
        def _():
            ua_scr[:, 0:H, :] = ca_scr[j]
            ub_scr[:, 0:H, :] = cb_scr[j]

    @pl.when(first)
    def _():
        ua_scr[:, H - W1:H, :] = sa_ref[...]
        ub_scr[:, H - W1:H, :] = sb_ref[...]

    @pl.when(j == 0)
    def _():
        y_ref[...] = jnp.zeros_like(y_ref)

    if nseq == 1:
        parts = [(0, 1, k * FFN_PART, (k + 1) * FFN_PART) for k in range(nparts)]
    else:
        sp = FFN_PART // L
        parts = [(k * sp, (k + 1) * sp, 0, L) for k in range(nparts)]
    tile_rows = lambda q0, q1, r0, r1: slice(q0 * L + r0, (q1 - 1) * L + r1)

    for (q0, q1, r0, r1) in parts:
        rs = tile_rows(q0, q1, r0, r1)
        ua_scr[q0:q1, H + r0:H + r1, :] = _dot(xn_ref[rs, :], wa_ref[...]).reshape(q1 - q0, r1 - r0, tf)
        ub_scr[q0:q1, H + r0:H + r1, :] = _dot(xn_ref[rs, :], wb_ref[...]).reshape(q1 - q0, r1 - r0, tf)

    spread = lambda row: jnp.broadcast_to(row, (SUBLANES, tf))
    wa8 = [spread(cwa_ref[t:t + 1, :]) for t in range(FFN_CONV)]
    wb8 = [spread(cwb_ref[t:t + 1, :]) for t in range(FFN_CONV)]
    ba8 = spread(cba_ref[...])
    bb8 = spread(cbb_ref[...])
    for (q0, q1, r0, r1) in parts:
        if nseq == 1:
            blocks = [(0, 1, r, r + FFN_RBLK) for r in range(r0, r1, FFN_RBLK)]
        else:
            qb = FFN_RBLK // L
            blocks = [(q, q + qb, 0, L) for q in range(q0, q1, qb)]
        for (bq0, bq1, br0, br1) in blocks:
            def conv(u_scr, w8, b8):
                tap = lambda d: u_scr[bq0:bq1, H - d + br0:H - d + br1, :].reshape(-1, SUBLANES, tf)
                out = b8 + w8[W1] * tap(0)
                for t in range(W1):
                    out = out + w8[t] * tap(W1 - t)
                return out
            a = conv(ua_scr, wa8, ba8)
            b = conv(ub_scr, wb8, bb8)
            row0 = bq0 * L + br0
            act_scr[row0:row0 + FFN_RBLK, :] = (_silu(a) * b).reshape(FFN_RBLK, tf).astype(BF16)
        rs = tile_rows(q0, q1, r0, r1)
        y_ref[rs, :] += _dot(act_scr[rs, :], wo_ref[...])

    na_ref[...] = ua_scr[:, H + L - W1:H + L, :]
    nb_ref[...] = ub_scr[:, H + L - W1:H + L, :]
    if tps > 1:
        ca_scr[j] = ua_scr[:, L:L + H, :]
        cb_scr[j] = ub_scr[:, L:L + H, :]

    @pl.when(j == pl.num_programs(1) - 1)
    def _():
        h_copy().wait()
        y_ref[...] = h_buf[...] + _rms(y_ref[...], npost_ref[...])


def _ffn(xn2, h2d, st, w_in_bf16, cw, cb, w_out_bf16, npost, *, nseq, L, tps):
    M, D = h2d.shape
    F = w_out_bf16.shape[0]
    tf = FFN_TF
    nj = F // tf
    tm = nseq * L
    W1 = FFN_CONV - 1
    assert tm % FFN_PART == 0 and FFN_PART % FFN_RBLK == 0 and (nseq == 1 or FFN_RBLK % L == 0)
    kern = functools.partial(_ffn_kernel, nseq=nseq, L=L, tps=tps)
    stspec_a = pl.BlockSpec((nseq, W1, tf), lambda i, j: (i // tps, 0, j))
    stspec_b = pl.BlockSpec((nseq, W1, tf), lambda i, j: (i // tps, 0, j + nj))
    return pl.pallas_call(
        kern,
        grid=(M // tm, nj),
        in_specs=[pl.BlockSpec((tm, D), lambda i, j: (i, 0), pipeline_mode=pl.Buffered(1)),
                  pl.BlockSpec(memory_space=pl.ANY),
                  stspec_a, stspec_b,
                  pl.BlockSpec((D, tf), lambda i, j: (0, j)),
                  pl.BlockSpec((D, tf), lambda i, j: (0, j + nj)),
                  pl.BlockSpec((FFN_CONV, tf), lambda i, j: (0, j)),
                  pl.BlockSpec((FFN_CONV, tf), lambda i, j: (0, j + nj)),
                  pl.BlockSpec((1, tf), lambda i, j: (0, j)),
                  pl.BlockSpec((1, tf), lambda i, j: (0, j + nj)),
                  pl.BlockSpec((tf, D), lambda i, j: (j, 0)),
                  pl.BlockSpec((1, D), lambda i, j: (0, 0))],
        out_specs=[pl.BlockSpec((tm, D), lambda i, j: (i, 0)),
                   pl.BlockSpec((nseq, W1, tf), lambda i, j: (i, 0, j)),
                   pl.BlockSpec((nseq, W1, tf), lambda i, j: (i, 0, j))],
        out_shape=[jax.ShapeDtypeStruct((M, D), F32),
                   jax.ShapeDtypeStruct((M // L, W1, F), F32),
                   jax.ShapeDtypeStruct((M // L, W1, F), F32)],
        scratch_shapes=[pltpu.VMEM((nseq, FFN_HDR + L, tf), F32),
                        pltpu.VMEM((nseq, FFN_HDR + L, tf), F32),
                        pltpu.VMEM((tm, tf), BF16),
                        pltpu.VMEM((tm, D), F32),
                        pltpu.SemaphoreType.DMA(())] + (
                            [pltpu.VMEM((nj, nseq, FFN_HDR, tf), F32)] * 2 if tps > 1 else []),
        compiler_params=_cparams(("arbitrary", "arbitrary")),
        name="ffn",
    )(xn2, h2d, st, st, w_in_bf16, w_in_bf16, cw, cw, cb, cb, w_out_bf16, npost)


def _layer(x, s_gla, s_ssd, s_conv, s_ffn, p, *, nb, c, ns, ffn_nseq, ffn_L, ffn_tps):
    B, L, D = x.shape
    M = B * L
    x2d = x.reshape(M, D)
    proj_a = _inproj(x2d, p['n_mix_pre'], p['w_in'], 512, tail=False)
    proj_b = _inproj(x2d, p['n_mix_pre'], p['w_in'], 512, tail=True)
    mix, g_new, h_new = _mixer(proj_a, proj_b, s_gla, s_conv, s_ssd, p, B=B, L=L, nb=nb, c=c, ns=ns)
    hres, xn2 = _outproj(mix, x2d, p['w_out'], p['n_mix_post'], p['n_ffn_pre'], 512)
    y, fa, fb = _ffn(xn2, hres, s_ffn, p['ffn_w_in'], p['ffn_cw'], p['ffn_cb'], p['ffn_w_out'],
                     p['n_ffn_post'], nseq=ffn_nseq, L=ffn_L, tps=ffn_tps)
    c_new = proj_b.reshape(B, L, -1)[:, L - (SSD_CONV - 1):, B_XS:B_XS + SSD_CONV_CH]
    f_new = jnp.concatenate([fa, fb], axis=-1)[ffn_tps - 1::ffn_tps]
    return y.reshape(B, L, D), g_new, h_new, c_new, f_new


def _prep_params(l, norm_mix_pre, norm_mix_post, norm_ffn_pre, norm_ffn_post, w_in, gla_w_gate2,
                 gla_b_gate, gla_norm, ssd_conv_w, ssd_conv_b, ssd_dt_bias, ssd_A_log, ssd_D, ssd_norm,
                 w_out, ffn_w_in, ffn_conv_w, ffn_conv_b, ffn_w_out):
    assert w_in.shape[2] == A_W + IN_TAIL
    w2p = jnp.zeros((SMALL_W, GLA_KDIM), F32).at[LR_OFF:LR_OFF + GLA_LOWRANK].set(gla_w_gate2[l])
    w2_hi = w2p.astype(BF16)
    w2_lo = (w2p - w2_hi.astype(F32)).astype(BF16)
    w2cat = jnp.concatenate([w2_hi, w2_hi, w2_lo], axis=0)
    pad_small = lambda v: jnp.zeros((1, SMALL_W), F32).at[0, DTR_OFF:DTR_OFF + SSD_HEADS].set(v)
    row = lambda v: v.reshape(1, -1)
    return dict(
        n_mix_pre=row(norm_mix_pre[l]), n_mix_post=row(norm_mix_post[l]),
        n_ffn_pre=row(norm_ffn_pre[l]), n_ffn_post=row(norm_ffn_post[l]),
        w_in=w_in[l].astype(BF16), w2cat=w2cat, bg=row(gla_b_gate[l]), gn=row(gla_norm[l]),
        ssd_cw=ssd_conv_w[l], ssd_cb=row(ssd_conv_b[l]),
        dtb=pad_small(ssd_dt_bias[l]), alog=pad_small(ssd_A_log[l]),
        de=row(jnp.repeat(ssd_D[l], SSD_HEADDIM)), ssd_nw=row(ssd_norm[l]),
        w_out=w_out[l].astype(BF16), ffn_w_in=ffn_w_in[l].astype(BF16),
        ffn_cw=ffn_conv_w[l], ffn_cb=row(ffn_conv_b[l]), ffn_w_out=ffn_w_out[l].astype(BF16))


def kernel(x_prompt, x_sample, state_gla, state_ssd, state_ssd_conv, state_ffn_conv, norm_mix_pre,
           norm_mix_post, norm_ffn_pre, norm_ffn_post, w_in, gla_w_gate2, gla_b_gate, gla_norm,
           ssd_conv_w, ssd_conv_b, ssd_dt_bias, ssd_A_log, ssd_D, ssd_norm, w_out, ffn_w_in,
           ffn_conv_w, ffn_conv_b, ffn_w_out):
    depth = w_in.shape[0]
    xp, xs = x_prompt, x_sample
    Bp, Lp, D = xp.shape
    Bs, Ls, _ = xs.shape
    F2 = ffn_w_in.shape[2]
    outs = [[] for _ in range(8)]
    for l in range(depth):
        p = _prep_params(l, norm_mix_pre, norm_mix_post, norm_ffn_pre, norm_ffn_post, w_in,
                         gla_w_gate2, gla_b_gate, gla_norm, ssd_conv_w, ssd_conv_b, ssd_dt_bias,
                         ssd_A_log, ssd_D, ssd_norm, w_out, ffn_w_in, ffn_conv_w, ffn_conv_b, ffn_w_out)
        cp = CHUNK
        xp, g1, h1, c1, f1 = _layer(
            xp,
            jnp.zeros((Bp, GLA_HEADS, GLA_HEAD_K, GLA_HEAD_V), F32),
            jnp.zeros((Bp, SSD_HEADS, SSD_HEADDIM, SSD_STATE), F32),
            jnp.zeros((Bp, SSD_CONV - 1, SSD_CONV_CH), F32),
            jnp.zeros((Bp, FFN_CONV - 1, F2), F32),
            p, nb=1, c=cp, ns=Bp, ffn_nseq=1, ffn_L=1024, ffn_tps=Lp // 1024)
        nbs = CHUNK // Ls
        xs, g2, h2, c2, f2 = _layer(
            xs, state_gla[l], state_ssd[l], state_ssd_conv[l], state_ffn_conv[l],
            p, nb=nbs, c=Ls, ns=1, ffn_nseq=512 // Ls, ffn_L=Ls, ffn_tps=1)
        for lst, val in zip(outs, (g1, h1, c1, f1, g2, h2, c2, f2)):
            lst.append(val)
    return (xp, xs) + tuple(jnp.stack(o) for o in outs)
```

```python
import functools

import numpy as np
import jax
import jax.numpy as jnp
from jax import lax
from jax.experimental import pallas as pl
from jax.experimental.pallas import tpu as pltpu

F32 = jnp.float32
BF16 = jnp.bfloat16
EPS = 1e-6

GLA_HEADS = 4
GLA_HEAD_K = 128
GLA_HEAD_V = 256
GLA_KDIM = GLA_HEADS * GLA_HEAD_K
GLA_WIDTH = GLA_HEADS * GLA_HEAD_V
GLA_LOWRANK = 16
GLA_GATE_NORMALIZER = 16.0
SSD_HEADS = 16
SSD_HEADDIM = 64
SSD_STATE = 128
SSD_GROUPS = 2
SSD_WIDTH = SSD_HEADS * SSD_HEADDIM
SSD_GROUP_WIDTH = SSD_WIDTH // SSD_GROUPS
SSD_CONV = 4
SSD_BC = 2 * SSD_GROUPS * SSD_STATE
SSD_CONV_CH = SSD_WIDTH + SSD_BC
FFN_CONV = 3
CHUNK = 64

A_Q = 0
A_K = GLA_KDIM
A_V = 2 * GLA_KDIM
A_G = A_V + GLA_WIDTH
A_W = A_G + GLA_WIDTH
IN_TAIL = GLA_LOWRANK + SSD_WIDTH + SSD_CONV_CH + SSD_HEADS
B_Z = 0
B_XS = SSD_WIDTH
B_BC = B_XS + SSD_WIDTH
B_SMALL = B_BC + SSD_BC
SMALL_W = 128
DTR_OFF = 0
LR_OFF = SSD_HEADS
B_W = B_SMALL + SMALL_W

LANES = 128
SUBLANES = 8
VMEM_LIMIT = 56 * 1024 * 1024

PROJ_TM = 512
ROW_PARTS = 4
FFN_TM_LONG = 1024
FFN_TM_SHORT = 512


def _group_tiles(B, L):
    if L >= CHUNK:
        lt = min(L, FFN_TM_LONG)
        return dict(nb=1, c=CHUNK, ns=B, ffn_nseq=1, ffn_L=lt, ffn_tps=L // lt)
    return dict(nb=CHUNK // L, c=L, ns=1, ffn_nseq=FFN_TM_SHORT // L, ffn_L=L, ffn_tps=1)


def _cparams(sem):
    return pltpu.CompilerParams(dimension_semantics=sem, vmem_limit_bytes=VMEM_LIMIT)


def _split3(x):
    hi = x.astype(BF16)
    r = x - hi.astype(F32)
    mid = r.astype(BF16)
    lo = (r - mid.astype(F32)).astype(BF16)
    return hi, mid, lo


def _dot(a, b):
    return jnp.dot(a, b, preferred_element_type=F32)


def _dot_nt(a, b):
    return lax.dot_general(a, b, (((1,), (1,)), ((), ())), preferred_element_type=F32)


def _dot_tn(a, b):
    return lax.dot_general(a, b, (((0,), (0,)), ((), ())), preferred_element_type=F32)


def _dot_sel(sel3_bf16, x_f32):
    return _dot(sel3_bf16, jnp.concatenate(_split3(x_f32), axis=0))


NEG_LOG2E = -1.4426950408889634


def _silu(x):
    return x / (1.0 + jnp.exp2(x * NEG_LOG2E))


def _softplus(x):
    return jnp.maximum(x, 0.0) + jnp.log1p(jnp.exp(-jnp.abs(x)))


def _rms(x, w):
    return x * lax.rsqrt(jnp.mean(x * x, axis=-1, keepdims=True) + EPS) * w


def _chunk_consts(nb, c):
    R = nb * c
    idx = np.arange(R)
    seq, pos = idx // c, idx % c
    same = seq[:, None] == seq[None, :]
    t, u = pos[:, None], pos[None, :]
    blocks = [same & (u <= t), same & (u > t)]
    masks = [np.eye(R, dtype=bool)]
    m = c // 2
    while m >= 1:
        blk = pos // (2 * m)
        rho = blk * 2 * m + m - 1
        upper = pos > rho
        a_up = upper[:, None] & (u > rho[:, None]) & (u <= t)
        a_lo = (~upper)[:, None] & (u > t) & (u <= rho[:, None])
        blocks.append(same & (a_up | a_lo))
        masks.append(same & upper[:, None] & (~upper)[None, :] & (blk[:, None] == blk[None, :]))
        m //= 2
    sel = np.concatenate(blocks, 0).astype(np.float32)
    lvl = np.stack(masks).astype(np.float32)
    causal = (same & (u <= t)).astype(np.float32)
    return sel, lvl, causal


INPROJ_RCH = 256


def _inproj_kernel(x_ref, nw_ref, w_ref, o_ref, *scratch, tail):
    if tail:
        w_scr, = scratch

        @pl.when(pl.program_id(0) == 0)
        def _():
            zpad = jnp.zeros((INPROJ_RCH, B_W - IN_TAIL), BF16)
            for r in range(0, w_ref.shape[0], INPROJ_RCH):
                blk = w_ref[r:r + INPROJ_RCH, :]
                w_scr[r:r + INPROJ_RCH, :] = jnp.concatenate(
                    [blk[:, GLA_LOWRANK:IN_TAIL], blk[:, 0:GLA_LOWRANK], zpad], axis=1)

        w = w_scr
    else:
        w = w_ref
    rp = x_ref.shape[0] // ROW_PARTS
    for k in range(ROW_PARTS):
        rs = slice(k * rp, (k + 1) * rp)
        xn = _rms(x_ref[rs, :], nw_ref[...]).astype(BF16)
        o_ref[rs, :] = _dot(xn, w[...])


def _inproj(x2d, nw, w_bf16, *, tail):
    M, D = x2d.shape
    tm = PROJ_TM
    n_out = B_W if tail else A_W
    return pl.pallas_call(
        functools.partial(_inproj_kernel, tail=tail),
        grid=(M // tm,),
        in_specs=[pl.BlockSpec((tm, D), lambda i: (i, 0)),
                  pl.BlockSpec((1, D), lambda i: (0, 0)),
                  pl.BlockSpec((D, A_W), lambda i: (0, 1 if tail else 0), pipeline_mode=pl.Buffered(1))],
        out_specs=pl.BlockSpec((tm, n_out), lambda i: (i, 0)),
        out_shape=jax.ShapeDtypeStruct((M, n_out), F32),
        scratch_shapes=[pltpu.VMEM((D, B_W), BF16)] if tail else [],
        compiler_params=_cparams(("arbitrary",)),
        name="inproj",
    )(x2d, nw, w_bf16)


def _gla_group(gi, q_ref, k_ref, v_ref, g_ref, sm_ref, s_in, s_out, sel_ref, lvl_ref, w2_ref, bg_ref,
               gn_ref, o_ref, *, nb, c, nlev):
    R = nb * c
    sm = sm_ref[gi]
    sm_hi = sm.astype(BF16)
    sm_lo = (sm - sm_hi.astype(F32)).astype(BF16)
    zg = _dot(jnp.concatenate([sm_hi, sm_lo, sm_hi], axis=1), w2_ref[...]) + bg_ref[...]
    lg2 = -_softplus(-zg) * (-NEG_LOG2E / GLA_GATE_NORMALIZER)
    E = _dot_sel(sel_ref[...], lg2)
    eb = jnp.exp2(E[0:R])
    q = q_ref[gi] * (GLA_HEAD_K ** -0.5)
    k = k_ref[gi]
    qe = q * eb
    ke = k * jnp.exp2(E[R:2 * R])
    ql, kl = [q.astype(BF16)], [k.astype(BF16)]
    for l in range(nlev):
        p = jnp.exp2(E[(2 + l) * R:(3 + l) * R])
        ql.append((q * p).astype(BF16))
        kl.append((k * p).astype(BF16))
    vb = v_ref[gi].astype(BF16)
    g = g_ref[gi]
    gn = gn_ref[...]
    rows = lax.broadcasted_iota(jnp.int32, (R, 1), 0)

    zk = jnp.zeros((R, GLA_HEAD_K), BF16)
    zv = jnp.zeros((R, GLA_HEAD_V), BF16)
    o_intra = []
    for hp in range(GLA_HEADS // 2):
        k0 = slice(2 * hp * GLA_HEAD_K, (2 * hp + 1) * GLA_HEAD_K)
        k1 = slice((2 * hp + 1) * GLA_HEAD_K, (2 * hp + 2) * GLA_HEAD_K)
        k01 = slice(2 * hp * GLA_HEAD_K, (2 * hp + 2) * GLA_HEAD_K)
        att2 = jnp.zeros((R, 2 * R), F32)
        for l in range(nlev + 1):
            kbd = jnp.concatenate([jnp.concatenate([kl[l][:, k0], zk], axis=1),
                                   jnp.concatenate([zk, kl[l][:, k1]], axis=1)], axis=0)
            att2 = att2 + _dot_nt(ql[l][:, k01], kbd) * lvl_ref[l]
        v0 = vb[:, 2 * hp * GLA_HEAD_V:(2 * hp + 1) * GLA_HEAD_V]
        v1 = vb[:, (2 * hp + 1) * GLA_HEAD_V:(2 * hp + 2) * GLA_HEAD_V]
        vbd = jnp.concatenate([jnp.concatenate([v0, zv], axis=1),
                               jnp.concatenate([zv, v1], axis=1)], axis=0)
        o2 = _dot(att2.astype(BF16), vbd)
        o_intra += [o2[:, 0:GLA_HEAD_V], o2[:, GLA_HEAD_V:2 * GLA_HEAD_V]]

    for h in range(GLA_HEADS):
        ks = slice(h * GLA_HEAD_K, (h + 1) * GLA_HEAD_K)
        vs = slice(h * GLA_HEAD_V, (h + 1) * GLA_HEAD_V)
        o = o_intra[h]
        for s in range(nb):
            if nb > 1:
                rm = jnp.logical_and(rows >= s * c, rows < (s + 1) * c).astype(F32)
                qs = (qe[:, ks] * rm).astype(BF16)
                kd = (ke[:, ks] * rm).astype(BF16)
            else:
                qs = qe[:, ks].astype(BF16)
                kd = ke[:, ks].astype(BF16)
            S = s_in[gi * nb + s, h]
            o = o + _dot(qs, S.astype(BF16))
            upd = _dot_tn(kd, vb[:, vs])
            d = eb[s * c + c - 1:s * c + c, ks]
            dcol = jnp.transpose(jnp.broadcast_to(d, (GLA_HEAD_K, GLA_HEAD_K)))
            s_out[gi * nb + s, h] = jnp.concatenate([dcol, dcol], axis=1) * S + upd
        gh = g[:, vs]
        o_ref[gi, :, vs] = (_rms(o, gn) * _silu(gh)).astype(BF16)


def _conv_taps(u8, prev8, w, bias, width, chained):
    sub = lax.broadcasted_iota(jnp.int32, (1, SUBLANES, 1), 1)
    acc = bias + w[width - 1:width] * u8
    for j in range(width - 1):
        d = width - 1 - j
        r = pltpu.roll(u8, d, axis=1)
        pr = pltpu.roll(prev8, d, axis=1)
        if chained:
            pr = jnp.concatenate([pr, r[:-1]], axis=0)
        acc = acc + w[j:j + 1] * jnp.where(sub < d, pr, r)
    return acc


def _ssd_group(gi, z_ref, xs_ref, bc_ref, sm_ref, sel_ref, cm_ref, ex_ref, cw_ref, cb_ref, dtb_ref,
               alog_ref, de_ref, nw_ref, y_ref, ht_scr, cx_scr, cbc_scr, *, nb, c, out_lane0=0):
    R = nb * c
    sq = slice(gi * nb, (gi + 1) * nb)
    cw = cw_ref[...]
    cbias = cb_ref[...]
    nt = R // SUBLANES
    chained = nb == 1
    xraw = xs_ref[gi].reshape(nt, SUBLANES, SSD_WIDTH)
    bcraw = bc_ref[gi].reshape(nt, SUBLANES, SSD_BC)
    xs = _silu(_conv_taps(xraw, cx_scr[sq], cw[:, 0:SSD_WIDTH], cbias[:, 0:SSD_WIDTH], SSD_CONV, chained))
    bca = _silu(_conv_taps(bcraw, cbc_scr[sq], cw[:, SSD_WIDTH:SSD_CONV_CH],
                           cbias[:, SSD_WIDTH:SSD_CONV_CH], SSD_CONV, chained))
    cx_scr[sq] = xraw[nt - nb:nt]
    cbc_scr[sq] = bcraw[nt - nb:nt]
    xs = xs.reshape(R, SSD_WIDTH)
    bca = bca.reshape(R, SSD_BC)
    Bm = bca[:, 0:SSD_GROUPS * SSD_STATE]
    Cm = bca[:, SSD_GROUPS * SSD_STATE:SSD_BC]

    dt = _softplus(sm_ref[gi] + dtb_ref[...])
    la2 = dt * (NEG_LOG2E * jnp.exp(alog_ref[...]))
    cs = _dot_sel(sel_ref[0:2 * R], la2)
    cum = cs[0:R]
    stack = jnp.concatenate([cs, dt], axis=0)
    st_e = _dot(jnp.concatenate(_split3(stack), axis=1), ex_ref[...])
    cum_e = st_e[0:R]
    lmc_e = st_e[R:2 * R]
    dt_e = st_e[2 * R:3 * R]
    ecum_e = jnp.exp2(cum_e)
    xdt = xs * dt_e
    xw = (xdt * jnp.exp2(lmc_e)).astype(BF16)
    Bb = Bm.astype(BF16)
    Cb = Cm.astype(BF16)

    cumT2 = jnp.transpose(jnp.concatenate([cum, cum], axis=0))
    cmask2 = cm_ref[...] > 0.5
    rows = lax.broadcasted_iota(jnp.int32, (R, 1), 0)
    lane = lax.broadcasted_iota(jnp.int32, (1, 2 * R), 1)
    first = lane < R
    lo_half = lax.broadcasted_iota(jnp.int32, (1, 2 * SSD_HEADDIM), 1) < SSD_HEADDIM

    hpg = SSD_HEADS // SSD_GROUPS
    for hg in range(SSD_GROUPS):
        ns = slice(hg * SSD_STATE, (hg + 1) * SSD_STATE)
        gs = slice(hg * SSD_GROUP_WIDTH, (hg + 1) * SSD_GROUP_WIDTH)
        bgrp = Bb[:, ns]
        cb2 = _dot_nt(Cb[:, ns], jnp.concatenate([bgrp, bgrp], axis=0))
        yg = []
        for j in range(hpg // 2):
            l0 = hg * SSD_GROUP_WIDTH + 2 * j * SSD_HEADDIM
            h0 = DTR_OFF + hg * hpg + 2 * j
            col = jnp.where(first, cum[:, h0:h0 + 1], cum[:, h0 + 1:h0 + 2])
            row = jnp.where(first, cumT2[h0:h0 + 1, :], cumT2[h0 + 1:h0 + 2, :])
            m2 = cb2 * jnp.exp2(jnp.where(cmask2, col - row, -jnp.inf))
            xpair = xdt[:, l0:l0 + 2 * SSD_HEADDIM]
            x2 = jnp.concatenate([jnp.where(lo_half, xpair, 0.0).astype(BF16),
                                  jnp.where(lo_half, 0.0, xpair).astype(BF16)], axis=0)
            yg.append(_dot(m2.astype(BF16), x2))
        y_intra = jnp.concatenate(yg, axis=1)
        y_inter = jnp.zeros((R, SSD_GROUP_WIDTH), F32)
        for s in range(nb):
            if nb > 1:
                rm = jnp.logical_and(rows >= s * c, rows < (s + 1) * c).astype(F32)
                cg = (Cm[:, ns] * rm).astype(BF16)
                bg_ = (Bm[:, ns] * rm).astype(BF16)
            else:
                cg = Cb[:, ns]
                bg_ = Bb[:, ns]
            hT = ht_scr[gi * nb + s, :, gs]
            y_inter = y_inter + _dot(cg, hT.astype(BF16))
            upd = _dot_tn(bg_, xw[:, gs])
            dl = ecum_e[s * c + c - 1:s * c + c, gs]
            ht_scr[gi * nb + s, :, gs] = dl * hT + upd
        y = y_intra + y_inter * ecum_e[:, gs] + de_ref[:, gs] * xs[:, gs]
        y = y * _silu(z_ref[gi, :, gs])
        os_ = slice(out_lane0 + hg * SSD_GROUP_WIDTH, out_lane0 + (hg + 1) * SSD_GROUP_WIDTH)
        y_ref[gi, :, os_] = _rms(y, nw_ref[:, gs]).astype(BF16)


def _group_view(a, G):
    return a.reshape(G, a.shape[0] // G, a.shape[1])


def _mixer_kernel(q_ref, k_ref, v_ref, g_ref, z_ref, xs_ref, bc_ref, sm_ref, s0_ref, cst_ref, h0_ref,
                  sel_ref, lvl_ref, w2_ref, bg_ref, gn_ref, cm_ref, ex_ref, cw_ref, cb_ref, dtb_ref,
                  alog_ref, de_ref, nw_ref, mix_ref, sn_ref, hn_ref, *scratch, ns, nb, c, nlev):
    ci = pl.program_id(1)
    if len(scratch) == 4:
        s_scr, ht_scr, cx_scr, cbc_scr = scratch
        s_in = s_out = s_scr
    else:
        ht_scr, cx_scr, cbc_scr = scratch
        s_scr, s_in, s_out = None, s0_ref, sn_ref

    @pl.when(ci == 0)
    def _():
        if s_scr is not None:
            s_scr[...] = s0_ref[...]
        cx_scr[...] = jnp.zeros_like(cx_scr)
        cbc_scr[...] = jnp.zeros_like(cbc_scr)
        cx_scr[:, SUBLANES - (SSD_CONV - 1):SUBLANES, :] = cst_ref[:, :, 0:SSD_WIDTH]
        cbc_scr[:, SUBLANES - (SSD_CONV - 1):SUBLANES, :] = cst_ref[:, :, SSD_WIDTH:SSD_CONV_CH]
        for s in range(ns * nb):
            ht_scr[s] = jnp.transpose(h0_ref[s].reshape(SSD_WIDTH, SSD_STATE))

    for gi in range(ns):
        _gla_group(gi, q_ref, k_ref, v_ref, g_ref, sm_ref, s_in, s_out, sel_ref, lvl_ref, w2_ref, bg_ref,
                   gn_ref, mix_ref, nb=nb, c=c, nlev=nlev)
        _ssd_group(gi, z_ref, xs_ref, bc_ref, sm_ref, sel_ref, cm_ref, ex_ref, cw_ref, cb_ref, dtb_ref,
                   alog_ref, de_ref, nw_ref, mix_ref, ht_scr, cx_scr, cbc_scr, nb=nb, c=c,
                   out_lane0=GLA_WIDTH)

    @pl.when(ci == pl.num_programs(1) - 1)
    def _():
        if s_scr is not None:
            sn_ref[...] = s_scr[...]
        for s in range(ns * nb):
            hn_ref[s] = jnp.transpose(ht_scr[s]).reshape(SSD_HEADS, SSD_HEADDIM, SSD_STATE)


def _mixer(proj_a, proj_b, s_gla, s_conv, s_ssd, p, *, B, L, nb, c, ns):
    R = nb * c
    G = B // nb
    Lg = L * nb
    ncl = Lg // R
    nlev = int(np.log2(c))
    assert 2 * R == LANES, "two heads share one 128-lane tile"
    assert c % SUBLANES == 0 and (nb == 1 or c == SUBLANES), "conv carry is one sublane tile per sequence"
    sel, lvl, causal = _chunk_consts(nb, c)
    sel = jnp.asarray(np.tile(sel, (1, 3)), BF16)
    lvl = jnp.asarray(np.tile(lvl, (1, 1, 2)), F32)
    causal = jnp.asarray(np.tile(causal, (1, 2)), F32)
    exn = np.zeros((SMALL_W, SSD_WIDTH), np.float32)
    for h in range(SSD_HEADS):
        exn[DTR_OFF + h, h * SSD_HEADDIM:(h + 1) * SSD_HEADDIM] = 1.0
    ex = jnp.asarray(np.tile(exn, (3, 1)), BF16)
    M = proj_a.shape[0]
    pa, pb = _group_view(proj_a, G), _group_view(proj_b, G)

    def rowblk(w, col):
        return pl.BlockSpec((ns, R, w), lambda bi, ci: (bi, ci, col // w))

    const2 = lambda a: pl.BlockSpec(a.shape, lambda bi, ci: (0,) * a.ndim)
    nsq = ns * nb
    sshape = (nsq, GLA_HEADS, GLA_HEAD_K, GLA_HEAD_V)
    sspec = pl.BlockSpec(sshape, lambda bi, ci: (bi, 0, 0, 0))
    hspec = pl.BlockSpec((nsq, SSD_HEADS, SSD_HEADDIM, SSD_STATE), lambda bi, ci: (bi, 0, 0, 0))
    consts = (sel, lvl, p['w2cat'], p['bg'], p['gn'], causal, ex, p['ssd_cw'], p['ssd_cb'], p['dtb'],
              p['alog'], p['de'], p['ssd_nw'])
    kern = functools.partial(_mixer_kernel, ns=ns, nb=nb, c=c, nlev=nlev)
    mix, s_new, h_new = pl.pallas_call(
        kern,
        grid=(G // ns, ncl),
        in_specs=[rowblk(GLA_KDIM, A_Q), rowblk(GLA_KDIM, A_K), rowblk(GLA_WIDTH, A_V),
                  rowblk(GLA_WIDTH, A_G), rowblk(SSD_WIDTH, B_Z), rowblk(SSD_WIDTH, B_XS),
                  rowblk(SSD_BC, B_BC), rowblk(SMALL_W, B_SMALL), sspec,
                  pl.BlockSpec((nsq, SSD_CONV - 1, SSD_CONV_CH), lambda bi, ci: (bi, 0, 0)), hspec]
                 + [const2(a) for a in consts],
        out_specs=[pl.BlockSpec((ns, R, GLA_WIDTH + SSD_WIDTH), lambda bi, ci: (bi, ci, 0)), sspec, hspec],
        out_shape=[jax.ShapeDtypeStruct((G, Lg, GLA_WIDTH + SSD_WIDTH), BF16),
                   jax.ShapeDtypeStruct((B, GLA_HEADS, GLA_HEAD_K, GLA_HEAD_V), F32),
                   jax.ShapeDtypeStruct((B, SSD_HEADS, SSD_HEADDIM, SSD_STATE), F32)],
        scratch_shapes=([pltpu.VMEM(sshape, F32)] if ncl > 1 else []) + [
            pltpu.VMEM((nsq, SSD_STATE, SSD_WIDTH), F32),
            pltpu.VMEM((nsq, SUBLANES, SSD_WIDTH), F32),
            pltpu.VMEM((nsq, SUBLANES, SSD_BC), F32)],
        compiler_params=_cparams(("parallel", "arbitrary")),
        name="mixer",
    )(pa, pa, pa, pa, pb, pb, pb, pb, s_gla, s_conv, s_ssd, *consts)
    return mix.reshape(M, GLA_WIDTH + SSD_WIDTH), s_new, h_new


def _outproj_kernel(m_ref, x_ref, w_ref, npost_ref, npre_ref, h_ref, xn_ref):
    rp = x_ref.shape[0] // ROW_PARTS
    for k in range(ROW_PARTS):
        rs = slice(k * rp, (k + 1) * rp)
        h = x_ref[rs, :] + _rms(_dot(m_ref[rs, :], w_ref[...]), npost_ref[...])
        h_ref[rs, :] = h
        xn_ref[rs, :] = _rms(h, npre_ref[...]).astype(BF16)


def _outproj(mix, x2d, w_out_bf16, npost, npre):
    M, D = x2d.shape
    W = w_out_bf16.shape[0]
    tm = PROJ_TM
    return pl.pallas_call(
        _outproj_kernel,
        grid=(M // tm,),
        in_specs=[pl.BlockSpec((tm, W), lambda i: (i, 0)),
                  pl.BlockSpec((tm, D), lambda i: (i, 0)),
                  pl.BlockSpec((W, D), lambda i: (0, 0)),
                  pl.BlockSpec((1, D), lambda i: (0, 0)),
                  pl.BlockSpec((1, D), lambda i: (0, 0))],
        out_specs=[pl.BlockSpec((tm, D), lambda i: (i, 0)),
                   pl.BlockSpec((tm, D), lambda i: (i, 0))],
        out_shape=[jax.ShapeDtypeStruct((M, D), F32), jax.ShapeDtypeStruct((M, D), BF16)],
        compiler_params=_cparams(("parallel",)),
        name="outproj",
    )(mix, x2d, w_out_bf16, npost, npre)


FFN_HDR = SUBLANES
FFN_TF = 512
FFN_RBLK = 64
FFN_PART = 256


def _ffn_kernel(xn_ref, h_hbm, sa_ref, sb_ref, wa_ref, wb_ref, cwa_ref, cwb_ref, cba_ref, cbb_ref,
                wo_ref, npost_ref, y_ref, na_ref, nb_ref, ua_scr, ub_scr, act_scr, h_buf, h_sem, *carry,
                nseq, L, tps):
    i = pl.program_id(0)
    j = pl.program_id(1)
    tf = wa_ref.shape[1]
    tm = nseq * L
    nparts = tm // FFN_PART
    W1 = FFN_CONV - 1
    H = FFN_HDR
    first = i % tps == 0

    def h_copy():
        return pltpu.make_async_copy(h_hbm.at[pl.ds(i * tm, tm), :], h_buf, h_sem)

    @pl.when(j == 0)
    def _():
        h_copy().start()

    if tps > 1:
        ca_scr, cb_scr = carry

        @pl.when(jnp.logical_not(first))
        def _():
            ua_scr[:, 0:H, :] = ca_scr[j]
            ub_scr[:, 0:H, :] = cb_scr[j]

    @pl.when(first)
    def _():
        ua_scr[:, H - W1:H, :] = sa_ref[...]
        ub_scr[:, H - W1:H, :] = sb_ref[...]

    @pl.when(j == 0)
    def _():
        y_ref[...] = jnp.zeros_like(y_ref)

    if nseq == 1:
        parts = [(0, 1, k * FFN_PART, (k + 1) * FFN_PART) for k in range(nparts)]
    else:
        sp = FFN_PART // L
        parts = [(k * sp, (k + 1) * sp, 0, L) for k in range(nparts)]
    tile_rows = lambda q0, q1, r0, r1: slice(q0 * L + r0, (q1 - 1) * L + r1)

    for (q0, q1, r0, r1) in parts:
        rs = tile_rows(q0, q1, r0, r1)
        ua_scr[q0:q1, H + r0:H + r1, :] = _dot(xn_ref[rs, :], wa_ref[...]).reshape(q1 - q0, r1 - r0, tf)
        ub_scr[q0:q1, H + r0:H + r1, :] = _dot(xn_ref[rs, :], wb_ref[...]).reshape(q1 - q0, r1 - r0, tf)

    spread = lambda row: jnp.broadcast_to(row, (SUBLANES, tf))
    wa8 = [spread(cwa_ref[t:t + 1, :]) for t in range(FFN_CONV)]
    wb8 = [spread(cwb_ref[t:t + 1, :]) for t in range(FFN_CONV)]
    ba8 = spread(cba_ref[...])
    bb8 = spread(cbb_ref[...])
    for (q0, q1, r0, r1) in parts:
        if nseq == 1:
            blocks = [(0, 1, r, r + FFN_RBLK) for r in range(r0, r1, FFN_RBLK)]
        else:
            qb = FFN_RBLK // L
            blocks = [(q, q + qb, 0, L) for q in range(q0, q1, qb)]
        for (bq0, bq1, br0, br1) in blocks:
            def conv(u_scr, w8, b8):
                tap = lambda d: u_scr[bq0:bq1, H - d + br0:H - d + br1, :].reshape(-1, SUBLANES, tf)
                out = b8 + w8[W1] * tap(0)
                for t in range(W1):
                    out = out + w8[t] * tap(W1 - t)
                return out
            a = conv(ua_scr, wa8, ba8)
            b = conv(ub_scr, wb8, bb8)
            row0 = bq0 * L + br0
            act_scr[row0:row0 + FFN_RBLK, :] = (_silu(a) * b).reshape(FFN_RBLK, tf).astype(BF16)
        rs = tile_rows(q0, q1, r0, r1)
        y_ref[rs, :] += _dot(act_scr[rs, :], wo_ref[...])

    na_ref[...] = ua_scr[:, H + L - W1:H + L, :]
    nb_ref[...] = ub_scr[:, H + L - W1:H + L, :]
    if tps > 1:
        ca_scr[j] = ua_scr[:, L:L + H, :]
        cb_scr[j] = ub_scr[:, L:L + H, :]

    @pl.when(j == pl.num_programs(1) - 1)
    def _():
        h_copy().wait()
        y_ref[...] = h_buf[...] + _rms(y_ref[...], npost_ref[...])


def _ffn(xn2, h2d, st, w_in_bf16, cw, cb, w_out_bf16, npost, *, nseq, L, tps):
    M, D = h2d.shape
    F = w_out_bf16.shape[0]
    tf = FFN_TF
    nj = F // tf
    tm = nseq * L
    W1 = FFN_CONV - 1
    assert tm % FFN_PART == 0 and FFN_PART % FFN_RBLK == 0 and (nseq == 1 or FFN_RBLK % L == 0)
    kern = functools.partial(_ffn_kernel, nseq=nseq, L=L, tps=tps)
    stspec_a = pl.BlockSpec((nseq, W1, tf), lambda i, j: (i // tps, 0, j))
    stspec_b = pl.BlockSpec((nseq, W1, tf), lambda i, j: (i // tps, 0, j + nj))
    return pl.pallas_call(
        kern,
        grid=(M // tm, nj),
        in_specs=[pl.BlockSpec((tm, D), lambda i, j: (i, 0), pipeline_mode=pl.Buffered(1)),
                  pl.BlockSpec(memory_space=pl.ANY),
                  stspec_a, stspec_b,
                  pl.BlockSpec((D, tf), lambda i, j: (0, j)),
                  pl.BlockSpec((D, tf), lambda i, j: (0, j + nj)),
                  pl.BlockSpec((FFN_CONV, tf), lambda i, j: (0, j)),
                  pl.BlockSpec((FFN_CONV, tf), lambda i, j: (0, j + nj)),
                  pl.BlockSpec((1, tf), lambda i, j: (0, j)),
                  pl.BlockSpec((1, tf), lambda i, j: (0, j + nj)),
                  pl.BlockSpec((tf, D), lambda i, j: (j, 0)),
                  pl.BlockSpec((1, D), lambda i, j: (0, 0))],
        out_specs=[pl.BlockSpec((tm, D), lambda i, j: (i, 0)),
                   pl.BlockSpec((nseq, W1, tf), lambda i, j: (i, 0, j)),
                   pl.BlockSpec((nseq, W1, tf), lambda i, j: (i, 0, j))],
        out_shape=[jax.ShapeDtypeStruct((M, D), F32),
                   jax.ShapeDtypeStruct((M // L, W1, F), F32),
                   jax.ShapeDtypeStruct((M // L, W1, F), F32)],
        scratch_shapes=[pltpu.VMEM((nseq, FFN_HDR + L, tf), F32),
                        pltpu.VMEM((nseq, FFN_HDR + L, tf), F32),
                        pltpu.VMEM((tm, tf), BF16),
                        pltpu.VMEM((tm, D), F32),
                        pltpu.SemaphoreType.DMA(())] + (
                            [pltpu.VMEM((nj, nseq, FFN_HDR, tf), F32)] * 2 if tps > 1 else []),
        compiler_params=_cparams(("arbitrary", "arbitrary")),
        name="ffn",
    )(xn2, h2d, st, st, w_in_bf16, w_in_bf16, cw, cw, cb, cb, w_out_bf16, npost)


def _layer(x, s_gla, s_ssd, s_conv, s_ffn, p):
    B, L, D = x.shape
    M = B * L
    t = _group_tiles(B, L)
    x2d = x.reshape(M, D)
    proj_a = _inproj(x2d, p['n_mix_pre'], p['w_in'], tail=False)
    proj_b = _inproj(x2d, p['n_mix_pre'], p['w_in'], tail=True)
    mix, g_new, h_new = _mixer(proj_a, proj_b, s_gla, s_conv, s_ssd, p, B=B, L=L,
                               nb=t['nb'], c=t['c'], ns=t['ns'])
    hres, xn2 = _outproj(mix, x2d, p['w_out'], p['n_mix_post'], p['n_ffn_pre'])
    y, fa, fb = _ffn(xn2, hres, s_ffn, p['ffn_w_in'], p['ffn_cw'], p['ffn_cb'], p['ffn_w_out'],
                     p['n_ffn_post'], nseq=t['ffn_nseq'], L=t['ffn_L'], tps=t['ffn_tps'])
    c_new = proj_b.reshape(B, L, -1)[:, L - (SSD_CONV - 1):, B_XS:B_XS + SSD_CONV_CH]
    f_new = jnp.concatenate([fa, fb], axis=-1)[t['ffn_tps'] - 1::t['ffn_tps']]
    return y.reshape(B, L, D), g_new, h_new, c_new, f_new


def _prep_params(l, norm_mix_pre, norm_mix_post, norm_ffn_pre, norm_ffn_post, w_in, gla_w_gate2,
                 gla_b_gate, gla_norm, ssd_conv_w, ssd_conv_b, ssd_dt_bias, ssd_A_log, ssd_D, ssd_norm,
                 w_out, ffn_w_in, ffn_conv_w, ffn_conv_b, ffn_w_out):
    assert w_in.shape[2] == A_W + IN_TAIL
    w2p = jnp.zeros((SMALL_W, GLA_KDIM), F32).at[LR_OFF:LR_OFF + GLA_LOWRANK].set(gla_w_gate2[l])
    w2_hi = w2p.astype(BF16)
    w2_lo = (w2p - w2_hi.astype(F32)).astype(BF16)
    w2cat = jnp.concatenate([w2_hi, w2_hi, w2_lo], axis=0)
    pad_small = lambda v: jnp.zeros((1, SMALL_W), F32).at[0, DTR_OFF:DTR_OFF + SSD_HEADS].set(v)
    row = lambda v: v.reshape(1, -1)
    return dict(
        n_mix_pre=row(norm_mix_pre[l]), n_mix_post=row(norm_mix_post[l]),
        n_ffn_pre=row(norm_ffn_pre[l]), n_ffn_post=row(norm_ffn_post[l]),
        w_in=w_in[l].astype(BF16), w2cat=w2cat, bg=row(gla_b_gate[l]), gn=row(gla_norm[l]),
        ssd_cw=ssd_conv_w[l], ssd_cb=row(ssd_conv_b[l]),
        dtb=pad_small(ssd_dt_bias[l]), alog=pad_small(ssd_A_log[l]),
        de=row(jnp.repeat(ssd_D[l], SSD_HEADDIM)), ssd_nw=row(ssd_norm[l]),
        w_out=w_out[l].astype(BF16), ffn_w_in=ffn_w_in[l].astype(BF16),
        ffn_cw=ffn_conv_w[l], ffn_cb=row(ffn_conv_b[l]), ffn_w_out=ffn_w_out[l].astype(BF16))


def kernel(x_prompt, x_sample, state_gla, state_ssd, state_ssd_conv, state_ffn_conv, norm_mix_pre,
           norm_mix_post, norm_ffn_pre, norm_ffn_post, w_in, gla_w_gate2, gla_b_gate, gla_norm,
           ssd_conv_w, ssd_conv_b, ssd_dt_bias, ssd_A_log, ssd_D, ssd_norm, w_out, ffn_w_in,
           ffn_conv_w, ffn_conv_b, ffn_w_out):
    depth = w_in.shape[0]
    xp, xs = x_prompt, x_sample
    Bp = xp.shape[0]
    F2 = ffn_w_in.shape[2]
    outs = [[] for _ in range(8)]
    for l in range(depth):
        p = _prep_params(l, norm_mix_pre, norm_mix_post, norm_ffn_pre, norm_ffn_post, w_in,
                         gla_w_gate2, gla_b_gate, gla_norm, ssd_conv_w, ssd_conv_b, ssd_dt_bias,
                         ssd_A_log, ssd_D, ssd_norm, w_out, ffn_w_in, ffn_conv_w, ffn_conv_b, ffn_w_out)
        xp, g1, h1, c1, f1 = _layer(
            xp,
            jnp.zeros((Bp, GLA_HEADS, GLA_HEAD_K, GLA_HEAD_V), F32),
            jnp.zeros((Bp, SSD_HEADS, SSD_HEADDIM, SSD_STATE), F32),
            jnp.zeros((Bp, SSD_CONV - 1, SSD_CONV_CH), F32),
            jnp.zeros((Bp, FFN_CONV - 1, F2), F32),
            p)
        xs, g2, h2, c2, f2 = _layer(
            xs, state_gla[l], state_ssd[l], state_ssd_conv[l], state_ffn_conv[l], p)
        for lst, val in zip(outs, (g1, h1, c1, f1, g2, h2, c2, f2)):
            lst.append(val)
    return (xp, xs) + tuple(jnp.stack(o) for o in outs)
```

```python
import functools

import numpy as np
import jax
import jax.numpy as jnp
from jax import lax
from jax.experimental import pallas as pl
from jax.experimental.pallas import tpu as pltpu

F32 = jnp.float32
BF16 = jnp.bfloat16
EPS = 1e-6

GLA_HEADS = 4
GLA_HEAD_K = 128
GLA_HEAD_V = 256
GLA_KDIM = GLA_HEADS * GLA_HEAD_K
GLA_WIDTH = GLA_HEADS * GLA_HEAD_V
GLA_LOWRANK = 16
GLA_GATE_NORMALIZER = 16.0
SSD_HEADS = 16
SSD_HEADDIM = 64
SSD_STATE = 128
SSD_GROUPS = 2
SSD_WIDTH = SSD_HEADS * SSD_HEADDIM
SSD_GROUP_WIDTH = SSD_WIDTH // SSD_GROUPS
SSD_CONV = 4
SSD_BC = 2 * SSD_GROUPS * SSD_STATE
SSD_CONV_CH = SSD_WIDTH + SSD_BC
FFN_CONV = 3
CHUNK = 64

A_Q = 0
A_K = GLA_KDIM
A_V = 2 * GLA_KDIM
A_G = A_V + GLA_WIDTH
A_W = A_G + GLA_WIDTH
IN_TAIL = GLA_LOWRANK + SSD_WIDTH + SSD_CONV_CH + SSD_HEADS
B_Z = 0
B_XS = SSD_WIDTH
B_BC = B_XS + SSD_WIDTH
B_SMALL = B_BC + SSD_BC
SMALL_W = 128
DTR_OFF = 0
LR_OFF = SSD_HEADS
B_W = B_SMALL + SMALL_W

LANES = 128
SUBLANES = 8
VMEM_LIMIT = 56 * 1024 * 1024

PROJ_TM = 512
ROW_PARTS = 4
FFN_TM_LONG = 1024
FFN_TM_SHORT = 512


def _group_tiles(B, L):
    if L >= CHUNK:
        lt = min(L, FFN_TM_LONG)
        return dict(nb=1, c=CHUNK, ns=B, ffn_nseq=1, ffn_L=lt, ffn_tps=L // lt)
    return dict(nb=CHUNK // L, c=L, ns=1, ffn_nseq=FFN_TM_SHORT // L, ffn_L=L, ffn_tps=1)


def _cparams(sem, **kw):
    return pltpu.CompilerParams(dimension_semantics=sem, vmem_limit_bytes=VMEM_LIMIT, **kw)


def _split3(x):
    hi = x.astype(BF16)
    r = x - hi.astype(F32)
    mid = r.astype(BF16)
    lo = (r - mid.astype(F32)).astype(BF16)
    return hi, mid, lo


def _dot(a, b):
    return jnp.dot(a, b, preferred_element_type=F32)


def _dot_nt(a, b):
    return lax.dot_general(a, b, (((1,), (1,)), ((), ())), preferred_element_type=F32)


def _dot_tn(a, b):
    return lax.dot_general(a, b, (((0,), (0,)), ((), ())), preferred_element_type=F32)


def _dot_sel(sel3_bf16, x_f32):
    return _dot(sel3_bf16, jnp.concatenate(_split3(x_f32), axis=0))


NEG_LOG2E = -1.4426950408889634


def _silu(x):
    return x / (1.0 + jnp.exp2(x * NEG_LOG2E))


def _softplus(x):
    return jnp.maximum(x, 0.0) + jnp.log1p(jnp.exp(-jnp.abs(x)))


def _rms(x, w):
    return x * lax.rsqrt(jnp.mean(x * x, axis=-1, keepdims=True) + EPS) * w


def _chunk_consts(nb, c):
    R = nb * c
    idx = np.arange(R)
    seq, pos = idx // c, idx % c
    same = seq[:, None] == seq[None, :]
    t, u = pos[:, None], pos[None, :]
    blocks = [same & (u <= t), same & (u > t)]
    masks = [np.eye(R, dtype=bool)]
    m = c // 2
    while m >= 1:
        blk = pos // (2 * m)
        rho = blk * 2 * m + m - 1
        upper = pos > rho
        a_up = upper[:, None] & (u > rho[:, None]) & (u <= t)
        a_lo = (~upper)[:, None] & (u > t) & (u <= rho[:, None])
        blocks.append(same & (a_up | a_lo))
        masks.append(same & upper[:, None] & (~upper)[None, :] & (blk[:, None] == blk[None, :]))
        m //= 2
    sel = np.concatenate(blocks, 0).astype(np.float32)
    lvl = np.stack(masks).astype(np.float32)
    causal = (same & (u <= t)).astype(np.float32)
    return sel, lvl, causal


INPROJ_RCH = 256


def _inproj_kernel(x_ref, nw_ref, w_ref, o_ref, *scratch, tail):
    if tail:
        w_scr, = scratch

        @pl.when(pl.program_id(0) == 0)
        def _():
            zpad = jnp.zeros((INPROJ_RCH, B_W - IN_TAIL), BF16)
            for r in range(0, w_ref.shape[0], INPROJ_RCH):
                blk = w_ref[r:r + INPROJ_RCH, :]
                w_scr[r:r + INPROJ_RCH, :] = jnp.concatenate(
                    [blk[:, GLA_LOWRANK:IN_TAIL], blk[:, 0:GLA_LOWRANK], zpad], axis=1)

        w = w_scr
    else:
        w = w_ref
    rp = x_ref.shape[0] // ROW_PARTS
    for k in range(ROW_PARTS):
        rs = slice(k * rp, (k + 1) * rp)
        xn = _rms(x_ref[rs, :], nw_ref[...]).astype(BF16)
        o_ref[rs, :] = _dot(xn, w[...])


def _inproj(x2d, nw, w_bf16, *, tail):
    M, D = x2d.shape
    tm = PROJ_TM
    n_out = B_W if tail else A_W
    return pl.pallas_call(
        functools.partial(_inproj_kernel, tail=tail),
        grid=(M // tm,),
        in_specs=[pl.BlockSpec((tm, D), lambda i: (i, 0)),
                  pl.BlockSpec((1, D), lambda i: (0, 0)),
                  pl.BlockSpec((D, A_W), lambda i: (0, 1 if tail else 0), pipeline_mode=pl.Buffered(1))],
        out_specs=pl.BlockSpec((tm, n_out), lambda i: (i, 0)),
        out_shape=jax.ShapeDtypeStruct((M, n_out), F32),
        scratch_shapes=[pltpu.VMEM((D, B_W), BF16)] if tail else [],
        compiler_params=_cparams(("arbitrary",)),
        name="inproj",
    )(x2d, nw, w_bf16)


def _gla_group(gi, q_ref, k_ref, v_ref, g_ref, sm_ref, s_in, s_out, sel_ref, lvl_ref, w2_ref, bg_ref,
               gn_ref, o_ref, *, nb, c, nlev):
    R = nb * c
    sm = sm_ref[gi]
    sm_hi = sm.astype(BF16)
    sm_lo = (sm - sm_hi.astype(F32)).astype(BF16)
    zg = _dot(jnp.concatenate([sm_hi, sm_lo, sm_hi], axis=1), w2_ref[...]) + bg_ref[...]
    lg2 = -_softplus(-zg) * (-NEG_LOG2E / GLA_GATE_NORMALIZER)
    E = _dot_sel(sel_ref[...], lg2)
    eb = jnp.exp2(E[0:R])
    q = q_ref[gi] * (GLA_HEAD_K ** -0.5)
    k = k_ref[gi]
    qe = q * eb
    ke = k * jnp.exp2(E[R:2 * R])
    ql, kl = [q.astype(BF16)], [k.astype(BF16)]
    for l in range(nlev):
        p = jnp.exp2(E[(2 + l) * R:(3 + l) * R])
        ql.append((q * p).astype(BF16))
        kl.append((k * p).astype(BF16))
    vb = v_ref[gi].astype(BF16)
    g = g_ref[gi]
    gn = gn_ref[...]
    rows = lax.broadcasted_iota(jnp.int32, (R, 1), 0)

    zk = jnp.zeros((R, GLA_HEAD_K), BF16)
    zv = jnp.zeros((R, GLA_HEAD_V), BF16)
    o_intra = []
    for hp in range(GLA_HEADS // 2):
        k0 = slice(2 * hp * GLA_HEAD_K, (2 * hp + 1) * GLA_HEAD_K)
        k1 = slice((2 * hp + 1) * GLA_HEAD_K, (2 * hp + 2) * GLA_HEAD_K)
        k01 = slice(2 * hp * GLA_HEAD_K, (2 * hp + 2) * GLA_HEAD_K)
        att2 = jnp.zeros((R, 2 * R), F32)
        for l in range(nlev + 1):
            kbd = jnp.concatenate([jnp.concatenate([kl[l][:, k0], zk], axis=1),
                                   jnp.concatenate([zk, kl[l][:, k1]], axis=1)], axis=0)
            att2 = att2 + _dot_nt(ql[l][:, k01], kbd) * lvl_ref[l]
        v0 = vb[:, 2 * hp * GLA_HEAD_V:(2 * hp + 1) * GLA_HEAD_V]
        v1 = vb[:, (2 * hp + 1) * GLA_HEAD_V:(2 * hp + 2) * GLA_HEAD_V]
        vbd = jnp.concatenate([jnp.concatenate([v0, zv], axis=1),
                               jnp.concatenate([zv, v1], axis=1)], axis=0)
        o2 = _dot(att2.astype(BF16), vbd)
        o_intra += [o2[:, 0:GLA_HEAD_V], o2[:, GLA_HEAD_V:2 * GLA_HEAD_V]]

    for h in range(GLA_HEADS):
        ks = slice(h * GLA_HEAD_K, (h + 1) * GLA_HEAD_K)
        vs = slice(h * GLA_HEAD_V, (h + 1) * GLA_HEAD_V)
        o = o_intra[h]
        for s in range(nb):
            if nb > 1:
                rm = jnp.logical_and(rows >= s * c, rows < (s + 1) * c).astype(F32)
                qs = (qe[:, ks] * rm).astype(BF16)
                kd = (ke[:, ks] * rm).astype(BF16)
            else:
                qs = qe[:, ks].astype(BF16)
                kd = ke[:, ks].astype(BF16)
            S = s_in[gi * nb + s, h]
            o = o + _dot(qs, S.astype(BF16))
            upd = _dot_tn(kd, vb[:, vs])
            d = eb[s * c + c - 1:s * c + c, ks]
            dcol = jnp.transpose(jnp.broadcast_to(d, (GLA_HEAD_K, GLA_HEAD_K)))
            s_out[gi * nb + s, h] = jnp.concatenate([dcol, dcol], axis=1) * S + upd
        gh = g[:, vs]
        o_ref[gi, :, vs] = (_rms(o, gn) * _silu(gh)).astype(BF16)


def _conv_taps(u8, prev8, w, bias, width, chained):
    sub = lax.broadcasted_iota(jnp.int32, (1, SUBLANES, 1), 1)
    acc = bias + w[width - 1:width] * u8
    for j in range(width - 1):
        d = width - 1 - j
        r = pltpu.roll(u8, d, axis=1)
        pr = pltpu.roll(prev8, d, axis=1)
        if chained:
            pr = jnp.concatenate([pr, r[:-1]], axis=0)
        acc = acc + w[j:j + 1] * jnp.where(sub < d, pr, r)
    return acc


def _ssd_group(gi, z_ref, xs_ref, bc_ref, sm_ref, sel_ref, cm_ref, ex_ref, cw_ref, cb_ref, dtb_ref,
               alog_ref, de_ref, nw_ref, y_ref, ht_scr, cx_scr, cbc_scr, *, nb, c, out_lane0=0):
    R = nb * c
    sq = slice(gi * nb, (gi + 1) * nb)
    cw = cw_ref[...]
    cbias = cb_ref[...]
    nt = R // SUBLANES
    chained = nb == 1
    xraw = xs_ref[gi].reshape(nt, SUBLANES, SSD_WIDTH)
    bcraw = bc_ref[gi].reshape(nt, SUBLANES, SSD_BC)
    xs = _silu(_conv_taps(xraw, cx_scr[sq], cw[:, 0:SSD_WIDTH], cbias[:, 0:SSD_WIDTH], SSD_CONV, chained))
    bca = _silu(_conv_taps(bcraw, cbc_scr[sq], cw[:, SSD_WIDTH:SSD_CONV_CH],
                           cbias[:, SSD_WIDTH:SSD_CONV_CH], SSD_CONV, chained))
    cx_scr[sq] = xraw[nt - nb:nt]
    cbc_scr[sq] = bcraw[nt - nb:nt]
    xs = xs.reshape(R, SSD_WIDTH)
    bca = bca.reshape(R, SSD_BC)
    Bm = bca[:, 0:SSD_GROUPS * SSD_STATE]
    Cm = bca[:, SSD_GROUPS * SSD_STATE:SSD_BC]

    dt = _softplus(sm_ref[gi] + dtb_ref[...])
    la2 = dt * (NEG_LOG2E * jnp.exp(alog_ref[...]))
    cs = _dot_sel(sel_ref[0:2 * R], la2)
    cum = cs[0:R]
    stack = jnp.concatenate([cs, dt], axis=0)
    st_e = _dot(jnp.concatenate(_split3(stack), axis=1), ex_ref[...])
    cum_e = st_e[0:R]
    lmc_e = st_e[R:2 * R]
    dt_e = st_e[2 * R:3 * R]
    ecum_e = jnp.exp2(cum_e)
    xdt = xs * dt_e
    xw = (xdt * jnp.exp2(lmc_e)).astype(BF16)
    Bb = Bm.astype(BF16)
    Cb = Cm.astype(BF16)

    cumT2 = jnp.transpose(jnp.concatenate([cum, cum], axis=0))
    cmask2 = cm_ref[...] > 0.5
    rows = lax.broadcasted_iota(jnp.int32, (R, 1), 0)
    lane = lax.broadcasted_iota(jnp.int32, (1, 2 * R), 1)
    first = lane < R
    lo_half = lax.broadcasted_iota(jnp.int32, (1, 2 * SSD_HEADDIM), 1) < SSD_HEADDIM

    hpg = SSD_HEADS // SSD_GROUPS
    for hg in range(SSD_GROUPS):
        ns = slice(hg * SSD_STATE, (hg + 1) * SSD_STATE)
        gs = slice(hg * SSD_GROUP_WIDTH, (hg + 1) * SSD_GROUP_WIDTH)
        bgrp = Bb[:, ns]
        cb2 = _dot_nt(Cb[:, ns], jnp.concatenate([bgrp, bgrp], axis=0))
        yg = []
        for j in range(hpg // 2):
            l0 = hg * SSD_GROUP_WIDTH + 2 * j * SSD_HEADDIM
            h0 = DTR_OFF + hg * hpg + 2 * j
            col = jnp.where(first, cum[:, h0:h0 + 1], cum[:, h0 + 1:h0 + 2])
            row = jnp.where(first, cumT2[h0:h0 + 1, :], cumT2[h0 + 1:h0 + 2, :])
            m2 = cb2 * jnp.exp2(jnp.where(cmask2, col - row, -jnp.inf))
            xpair = xdt[:, l0:l0 + 2 * SSD_HEADDIM]
            x2 = jnp.concatenate([jnp.where(lo_half, xpair, 0.0).astype(BF16),
                                  jnp.where(lo_half, 0.0, xpair).astype(BF16)], axis=0)
            yg.append(_dot(m2.astype(BF16), x2))
        y_intra = jnp.concatenate(yg, axis=1)
        y_inter = jnp.zeros((R, SSD_GROUP_WIDTH), F32)
        for s in range(nb):
            if nb > 1:
                rm = jnp.logical_and(rows >= s * c, rows < (s + 1) * c).astype(F32)
                cg = (Cm[:, ns] * rm).astype(BF16)
                bg_ = (Bm[:, ns] * rm).astype(BF16)
            else:
                cg = Cb[:, ns]
                bg_ = Bb[:, ns]
            hT = ht_scr[gi * nb + s, :, gs]
            y_inter = y_inter + _dot(cg, hT.astype(BF16))
            upd = _dot_tn(bg_, xw[:, gs])
            dl = ecum_e[s * c + c - 1:s * c + c, gs]
            ht_scr[gi * nb + s, :, gs] = dl * hT + upd
        y = y_intra + y_inter * ecum_e[:, gs] + de_ref[:, gs] * xs[:, gs]
        y = y * _silu(z_ref[gi, :, gs])
        os_ = slice(out_lane0 + hg * SSD_GROUP_WIDTH, out_lane0 + (hg + 1) * SSD_GROUP_WIDTH)
        y_ref[gi, :, os_] = _rms(y, nw_ref[:, gs]).astype(BF16)


def _group_view(a, G):
    return a.reshape(G, a.shape[0] // G, a.shape[1])


MIXER_NIN = 24
MIXER_OUT_PARTS = 2


def _mixer_kernel(*refs, **kw):
    mix_ref, sn_ref, hn_ref = refs[MIXER_NIN:MIXER_NIN + 3]
    _mixer_body(refs[:MIXER_NIN], mix_ref, sn_ref, hn_ref, refs[MIXER_NIN + 3:], **kw)


def _mixer_out_kernel(*refs, **kw):
    x_ref, wout_ref, npost_ref, npre_ref = refs[MIXER_NIN:MIXER_NIN + 4]
    h_ref, xn_ref, sn_ref, hn_ref = refs[MIXER_NIN + 4:MIXER_NIN + 8]
    mix_scr = refs[MIXER_NIN + 8]
    _mixer_body(refs[:MIXER_NIN], mix_scr, sn_ref, hn_ref, refs[MIXER_NIN + 9:], **kw)
    ns, R, W = mix_scr.shape
    D = x_ref.shape[-1]
    gp = max(ns // MIXER_OUT_PARTS, 1)
    for g0 in range(0, ns, gp):
        gs = slice(g0, g0 + gp)
        mix = _dot(mix_scr[gs].reshape(gp * R, W), wout_ref[...])
        h = x_ref[gs].reshape(gp * R, D) + _rms(mix, npost_ref[...])
        h_ref[gs] = h.reshape(gp, R, D)
        xn_ref[gs] = _rms(h, npre_ref[...]).astype(BF16).reshape(gp, R, D)


def _mixer_body(ins, mix_ref, sn_ref, hn_ref, scratch, *, ns, nb, c, nlev):
    (q_ref, k_ref, v_ref, g_ref, z_ref, xs_ref, bc_ref, sm_ref, s0_ref, cst_ref, h0_ref,
     sel_ref, lvl_ref, w2_ref, bg_ref, gn_ref, cm_ref, ex_ref, cw_ref, cb_ref, dtb_ref,
     alog_ref, de_ref, nw_ref) = ins
    ci = pl.program_id(1)
    if len(scratch) == 4:
        s_scr, ht_scr, cx_scr, cbc_scr = scratch
        s_in = s_out = s_scr
    else:
        ht_scr, cx_scr, cbc_scr = scratch
        s_scr, s_in, s_out = None, s0_ref, sn_ref

    @pl.when(ci == 0)
    def _():
        if s_scr is not None:
            s_scr[...] = s0_ref[...]
        cx_scr[...] = jnp.zeros_like(cx_scr)
        cbc_scr[...] = jnp.zeros_like(cbc_scr)
        cx_scr[:, SUBLANES - (SSD_CONV - 1):SUBLANES, :] = cst_ref[:, :, 0:SSD_WIDTH]
        cbc_scr[:, SUBLANES - (SSD_CONV - 1):SUBLANES, :] = cst_ref[:, :, SSD_WIDTH:SSD_CONV_CH]
        for s in range(ns * nb):
            ht_scr[s] = jnp.transpose(h0_ref[s].reshape(SSD_WIDTH, SSD_STATE))

    for gi in range(ns):
        _gla_group(gi, q_ref, k_ref, v_ref, g_ref, sm_ref, s_in, s_out, sel_ref, lvl_ref, w2_ref, bg_ref,
                   gn_ref, mix_ref, nb=nb, c=c, nlev=nlev)
        _ssd_group(gi, z_ref, xs_ref, bc_ref, sm_ref, sel_ref, cm_ref, ex_ref, cw_ref, cb_ref, dtb_ref,
                   alog_ref, de_ref, nw_ref, mix_ref, ht_scr, cx_scr, cbc_scr, nb=nb, c=c,
                   out_lane0=GLA_WIDTH)

    @pl.when(ci == pl.num_programs(1) - 1)
    def _():
        if s_scr is not None:
            sn_ref[...] = s_scr[...]
        for s in range(ns * nb):
            hn_ref[s] = jnp.transpose(ht_scr[s]).reshape(SSD_HEADS, SSD_HEADDIM, SSD_STATE)


def _mixer(proj_a, proj_b, s_gla, s_conv, s_ssd, p, *, B, L, nb, c, ns, out=None):
    R = nb * c
    G = B // nb
    Lg = L * nb
    ncl = Lg // R
    nlev = int(np.log2(c))
    assert 2 * R == LANES, "two heads share one 128-lane tile"
    assert c % SUBLANES == 0 and (nb == 1 or c == SUBLANES), "conv carry is one sublane tile per sequence"
    sel, lvl, causal = _chunk_consts(nb, c)
    sel = jnp.asarray(np.tile(sel, (1, 3)), BF16)
    lvl = jnp.asarray(np.tile(lvl, (1, 1, 2)), F32)
    causal = jnp.asarray(np.tile(causal, (1, 2)), F32)
    exn = np.zeros((SMALL_W, SSD_WIDTH), np.float32)
    for h in range(SSD_HEADS):
        exn[DTR_OFF + h, h * SSD_HEADDIM:(h + 1) * SSD_HEADDIM] = 1.0
    ex = jnp.asarray(np.tile(exn, (3, 1)), BF16)
    M = proj_a.shape[0]
    pa, pb = _group_view(proj_a, G), _group_view(proj_b, G)

    def rowblk(w, col):
        return pl.BlockSpec((ns, R, w), lambda bi, ci: (bi, ci, col // w))

    const2 = lambda a: pl.BlockSpec(a.shape, lambda bi, ci: (0,) * a.ndim)
    nsq = ns * nb
    sshape = (nsq, GLA_HEADS, GLA_HEAD_K, GLA_HEAD_V)
    sspec = pl.BlockSpec(sshape, lambda bi, ci: (bi, 0, 0, 0))
    hspec = pl.BlockSpec((nsq, SSD_HEADS, SSD_HEADDIM, SSD_STATE), lambda bi, ci: (bi, 0, 0, 0))
    consts = (sel, lvl, p['w2cat'], p['bg'], p['gn'], causal, ex, p['ssd_cw'], p['ssd_cb'], p['dtb'],
              p['alog'], p['de'], p['ssd_nw'])
    kw = dict(ns=ns, nb=nb, c=c, nlev=nlev)
    mixw = GLA_WIDTH + SSD_WIDTH
    in_specs = [rowblk(GLA_KDIM, A_Q), rowblk(GLA_KDIM, A_K), rowblk(GLA_WIDTH, A_V),
                rowblk(GLA_WIDTH, A_G), rowblk(SSD_WIDTH, B_Z), rowblk(SSD_WIDTH, B_XS),
                rowblk(SSD_BC, B_BC), rowblk(SMALL_W, B_SMALL), sspec,
                pl.BlockSpec((nsq, SSD_CONV - 1, SSD_CONV_CH), lambda bi, ci: (bi, 0, 0)), hspec
                ] + [const2(a) for a in consts]
    operands = (pa, pa, pa, pa, pb, pb, pb, pb, s_gla, s_conv, s_ssd, *consts)
    state_shapes = [jax.ShapeDtypeStruct((B, GLA_HEADS, GLA_HEAD_K, GLA_HEAD_V), F32),
                    jax.ShapeDtypeStruct((B, SSD_HEADS, SSD_HEADDIM, SSD_STATE), F32)]
    scratch = ([pltpu.VMEM(sshape, F32)] if ncl > 1 else []) + [
        pltpu.VMEM((nsq, SSD_STATE, SSD_WIDTH), F32),
        pltpu.VMEM((nsq, SUBLANES, SSD_WIDTH), F32),
        pltpu.VMEM((nsq, SUBLANES, SSD_BC), F32)]
    rows_spec = lambda w: pl.BlockSpec((ns, R, w), lambda bi, ci: (bi, ci, 0))
    if out is None:
        mix, s_new, h_new = pl.pallas_call(
            functools.partial(_mixer_kernel, **kw),
            grid=(G // ns, ncl),
            in_specs=in_specs,
            out_specs=[rows_spec(mixw), sspec, hspec],
            out_shape=[jax.ShapeDtypeStruct((G, Lg, mixw), BF16)] + state_shapes,
            scratch_shapes=scratch,
            compiler_params=_cparams(("parallel", "arbitrary")),
            name="mixer",
        )(*operands)
        return mix.reshape(M, mixw), s_new, h_new
    x2d, w_out, npost, npre = out
    D = x2d.shape[1]
    hres, xn2, s_new, h_new = pl.pallas_call(
        functools.partial(_mixer_out_kernel, **kw),
        grid=(G // ns, ncl),
        in_specs=in_specs + [rows_spec(D),
                             pl.BlockSpec(w_out.shape, lambda bi, ci: (0, 0), pipeline_mode=pl.Buffered(1)),
                             const2(npost), const2(npre)],
        out_specs=[rows_spec(D), rows_spec(D), sspec, hspec],
        out_shape=[jax.ShapeDtypeStruct((G, Lg, D), F32), jax.ShapeDtypeStruct((G, Lg, D), BF16)] + state_shapes,
        scratch_shapes=[pltpu.VMEM((ns, R, mixw), BF16)] + scratch,
        compiler_params=_cparams(("parallel", "arbitrary")),
        name="mixer_out",
    )(*operands, _group_view(x2d, G), w_out, npost, npre)
    return (hres.reshape(M, D), xn2.reshape(M, D)), s_new, h_new


def _outproj_kernel(m_ref, x_ref, w_ref, npost_ref, npre_ref, h_ref, xn_ref):
    rp = x_ref.shape[0] // ROW_PARTS
    for k in range(ROW_PARTS):
        rs = slice(k * rp, (k + 1) * rp)
        h = x_ref[rs, :] + _rms(_dot(m_ref[rs, :], w_ref[...]), npost_ref[...])
        h_ref[rs, :] = h
        xn_ref[rs, :] = _rms(h, npre_ref[...]).astype(BF16)


def _outproj(mix, x2d, w_out_bf16, npost, npre):
    M, D = x2d.shape
    W = w_out_bf16.shape[0]
    tm = PROJ_TM
    return pl.pallas_call(
        _outproj_kernel,
        grid=(M // tm,),
        in_specs=[pl.BlockSpec((tm, W), lambda i: (i, 0)),
                  pl.BlockSpec((tm, D), lambda i: (i, 0)),
                  pl.BlockSpec((W, D), lambda i: (0, 0)),
                  pl.BlockSpec((1, D), lambda i: (0, 0)),
                  pl.BlockSpec((1, D), lambda i: (0, 0))],
        out_specs=[pl.BlockSpec((tm, D), lambda i: (i, 0)),
                   pl.BlockSpec((tm, D), lambda i: (i, 0))],
        out_shape=[jax.ShapeDtypeStruct((M, D), F32), jax.ShapeDtypeStruct((M, D), BF16)],
        compiler_params=_cparams(("parallel",)),
        name="outproj",
    )(mix, x2d, w_out_bf16, npost, npre)


FFN_HDR = SUBLANES
FFN_TF = 512
FFN_RBLK = 64
FFN_PART = 256


def _ffn_kernel(xn_ref, h_hbm, sa_ref, sb_ref, wa_ref, wb_ref, cwa_ref, cwb_ref, cba_ref, cbb_ref,
                wo_ref, npost_ref, y_ref, na_ref, nb_ref, ua_scr, ub_scr, act_scr, h_buf, h_sem, *carry,
                nseq, L, tps):
    i = pl.program_id(0)
    j = pl.program_id(1)
    tf = wa_ref.shape[1]
    tm = nseq * L
    nparts = tm // FFN_PART
    W1 = FFN_CONV - 1
    H = FFN_HDR
    first = i % tps == 0

    def h_copy():
        return pltpu.make_async_copy(h_hbm.at[pl.ds(i * tm, tm), :], h_buf, h_sem)

    @pl.when(j == 0)
    def _():
        h_copy().start()

    if tps > 1:
        ca_scr, cb_scr = carry

        @pl.when(jnp.logical_not(first))
        def _():
            ua_scr[:, 0:H, :] = ca_scr[j]
            ub_scr[:, 0:H, :] = cb_scr[j]

    @pl.when(first)
    def _():
        ua_scr[:, H - W1:H, :] = sa_ref[...]
        ub_scr[:, H - W1:H, :] = sb_ref[...]

    @pl.when(j == 0)
    def _():
        y_ref[...] = jnp.zeros_like(y_ref)

    if nseq == 1:
        parts = [(0, 1, k * FFN_PART, (k + 1) * FFN_PART) for k in range(nparts)]
    else:
        sp = FFN_PART // L
        parts = [(k * sp, (k + 1) * sp, 0, L) for k in range(nparts)]
    tile_rows = lambda q0, q1, r0, r1: slice(q0 * L + r0, (q1 - 1) * L + r1)

    for (q0, q1, r0, r1) in parts:
        rs = tile_rows(q0, q1, r0, r1)
        ua_scr[q0:q1, H + r0:H + r1, :] = _dot(xn_ref[rs, :], wa_ref[...]).reshape(q1 - q0, r1 - r0, tf)
        ub_scr[q0:q1, H + r0:H + r1, :] = _dot(xn_ref[rs, :], wb_ref[...]).reshape(q1 - q0, r1 - r0, tf)

    spread = lambda row: jnp.broadcast_to(row, (SUBLANES, tf))
    wa8 = [spread(cwa_ref[t:t + 1, :]) for t in range(FFN_CONV)]
    wb8 = [spread(cwb_ref[t:t + 1, :]) for t in range(FFN_CONV)]
    ba8 = spread(cba_ref[...])
    bb8 = spread(cbb_ref[...])
    for (q0, q1, r0, r1) in parts:
        if nseq == 1:
            blocks = [(0, 1, r, r + FFN_RBLK) for r in range(r0, r1, FFN_RBLK)]
        else:
            qb = FFN_RBLK // L
            blocks = [(q, q + qb, 0, L) for q in range(q0, q1, qb)]
        for (bq0, bq1, br0, br1) in blocks:
            def conv(u_scr, w8, b8):
                tap = lambda d: u_scr[bq0:bq1, H - d + br0:H - d + br1, :].reshape(-1, SUBLANES, tf)
                out = b8 + w8[W1] * tap(0)
                for t in range(W1):
                    out = out + w8[t] * tap(W1 - t)
                return out
            a = conv(ua_scr, wa8, ba8)
            b = conv(ub_scr, wb8, bb8)
            row0 = bq0 * L + br0
            act_scr[row0:row0 + FFN_RBLK, :] = (_silu(a) * b).reshape(FFN_RBLK, tf).astype(BF16)
        rs = tile_rows(q0, q1, r0, r1)
        y_ref[rs, :] += _dot(act_scr[rs, :], wo_ref[...])

    na_ref[...] = ua_scr[:, H + L - W1:H + L, :]
    nb_ref[...] = ub_scr[:, H + L - W1:H + L, :]
    if tps > 1:
        ca_scr[j] = ua_scr[:, L:L + H, :]
        cb_scr[j] = ub_scr[:, L:L + H, :]

    @pl.when(j == pl.num_programs(1) - 1)
    def _():
        h_copy().wait()
        y_ref[...] = h_buf[...] + _rms(y_ref[...], npost_ref[...])


def _ffn(xn2, h2d, st, w_in_bf16, cw, cb, w_out_bf16, npost, *, nseq, L, tps):
    M, D = h2d.shape
    F = w_out_bf16.shape[0]
    tf = FFN_TF
    nj = F // tf
    tm = nseq * L
    W1 = FFN_CONV - 1
    assert tm % FFN_PART == 0 and FFN_PART % FFN_RBLK == 0 and (nseq == 1 or FFN_RBLK % L == 0)
    kern = functools.partial(_ffn_kernel, nseq=nseq, L=L, tps=tps)
    stspec_a = pl.BlockSpec((nseq, W1, tf), lambda i, j: (i // tps, 0, j))
    stspec_b = pl.BlockSpec((nseq, W1, tf), lambda i, j: (i // tps, 0, j + nj))
    return pl.pallas_call(
        kern,
        grid=(M // tm, nj),
        in_specs=[pl.BlockSpec((tm, D), lambda i, j: (i, 0), pipeline_mode=pl.Buffered(1)),
                  pl.BlockSpec(memory_space=pl.ANY),
                  stspec_a, stspec_b,
                  pl.BlockSpec((D, tf), lambda i, j: (0, j)),
                  pl.BlockSpec((D, tf), lambda i, j: (0, j + nj)),
                  pl.BlockSpec((FFN_CONV, tf), lambda i, j: (0, j)),
                  pl.BlockSpec((FFN_CONV, tf), lambda i, j: (0, j + nj)),
                  pl.BlockSpec((1, tf), lambda i, j: (0, j)),
                  pl.BlockSpec((1, tf), lambda i, j: (0, j + nj)),
                  pl.BlockSpec((tf, D), lambda i, j: (j, 0)),
                  pl.BlockSpec((1, D), lambda i, j: (0, 0))],
        out_specs=[pl.BlockSpec((tm, D), lambda i, j: (i, 0)),
                   pl.BlockSpec((nseq, W1, tf), lambda i, j: (i, 0, j)),
                   pl.BlockSpec((nseq, W1, tf), lambda i, j: (i, 0, j))],
        out_shape=[jax.ShapeDtypeStruct((M, D), F32),
                   jax.ShapeDtypeStruct((M // L, W1, F), F32),
                   jax.ShapeDtypeStruct((M // L, W1, F), F32)],
        scratch_shapes=[pltpu.VMEM((nseq, FFN_HDR + L, tf), F32),
                        pltpu.VMEM((nseq, FFN_HDR + L, tf), F32),
                        pltpu.VMEM((tm, tf), BF16),
                        pltpu.VMEM((tm, D), F32),
                        pltpu.SemaphoreType.DMA(())] + (
                            [pltpu.VMEM((nj, nseq, FFN_HDR, tf), F32)] * 2 if tps > 1 else []),
        compiler_params=_cparams(("arbitrary", "arbitrary")),
        name="ffn",
    )(xn2, h2d, st, st, w_in_bf16, w_in_bf16, cw, cw, cb, cb, w_out_bf16, npost)


def _layer(x, s_gla, s_ssd, s_conv, s_ffn, p):
    B, L, D = x.shape
    M = B * L
    t = _group_tiles(B, L)
    x2d = x.reshape(M, D)
    proj_a = _inproj(x2d, p['n_mix_pre'], p['w_in'], tail=False)
    proj_b = _inproj(x2d, p['n_mix_pre'], p['w_in'], tail=True)
    if t['nb'] == 1:
        (hres, xn2), g_new, h_new = _mixer(proj_a, proj_b, s_gla, s_conv, s_ssd, p, B=B, L=L,
                                           nb=t['nb'], c=t['c'], ns=t['ns'],
                                           out=(x2d, p['w_out'], p['n_mix_post'], p['n_ffn_pre']))
    else:
        mix, g_new, h_new = _mixer(proj_a, proj_b, s_gla, s_conv, s_ssd, p, B=B, L=L,
                                   nb=t['nb'], c=t['c'], ns=t['ns'])
        hres, xn2 = _outproj(mix, x2d, p['w_out'], p['n_mix_post'], p['n_ffn_pre'])
    y, fa, fb = _ffn(xn2, hres, s_ffn, p['ffn_w_in'], p['ffn_cw'], p['ffn_cb'], p['ffn_w_out'],
                     p['n_ffn_post'], nseq=t['ffn_nseq'], L=t['ffn_L'], tps=t['ffn_tps'])
    c_new = proj_b.reshape(B, L, -1)[:, L - (SSD_CONV - 1):, B_XS:B_XS + SSD_CONV_CH]
    f_new = jnp.concatenate([fa, fb], axis=-1)[t['ffn_tps'] - 1::t['ffn_tps']]
    return y.reshape(B, L, D), g_new, h_new, c_new, f_new


def _prep_params(l, norm_mix_pre, norm_mix_post, norm_ffn_pre, norm_ffn_post, w_in, gla_w_gate2,
                 gla_b_gate, gla_norm, ssd_conv_w, ssd_conv_b, ssd_dt_bias, ssd_A_log, ssd_D, ssd_norm,
                 w_out, ffn_w_in, ffn_conv_w, ffn_conv_b, ffn_w_out):
    assert w_in.shape[2] == A_W + IN_TAIL
    w2p = jnp.zeros((SMALL_W, GLA_KDIM), F32).at[LR_OFF:LR_OFF + GLA_LOWRANK].set(gla_w_gate2[l])
    w2_hi = w2p.astype(BF16)
    w2_lo = (w2p - w2_hi.astype(F32)).astype(BF16)
    w2cat = jnp.concatenate([w2_hi, w2_hi, w2_lo], axis=0)
    pad_small = lambda v: jnp.zeros((1, SMALL_W), F32).at[0, DTR_OFF:DTR_OFF + SSD_HEADS].set(v)
    row = lambda v: v.reshape(1, -1)
    return dict(
        n_mix_pre=row(norm_mix_pre[l]), n_mix_post=row(norm_mix_post[l]),
        n_ffn_pre=row(norm_ffn_pre[l]), n_ffn_post=row(norm_ffn_post[l]),
        w_in=w_in[l].astype(BF16), w2cat=w2cat, bg=row(gla_b_gate[l]), gn=row(gla_norm[l]),
        ssd_cw=ssd_conv_w[l], ssd_cb=row(ssd_conv_b[l]),
        dtb=pad_small(ssd_dt_bias[l]), alog=pad_small(ssd_A_log[l]),
        de=row(jnp.repeat(ssd_D[l], SSD_HEADDIM)), ssd_nw=row(ssd_norm[l]),
        w_out=w_out[l].astype(BF16), ffn_w_in=ffn_w_in[l].astype(BF16),
        ffn_cw=ffn_conv_w[l], ffn_cb=row(ffn_conv_b[l]), ffn_w_out=ffn_w_out[l].astype(BF16))


def kernel(x_prompt, x_sample, state_gla, state_ssd, state_ssd_conv, state_ffn_conv, norm_mix_pre,
           norm_mix_post, norm_ffn_pre, norm_ffn_post, w_in, gla_w_gate2, gla_b_gate, gla_norm,
           ssd_conv_w, ssd_conv_b, ssd_dt_bias, ssd_A_log, ssd_D, ssd_norm, w_out, ffn_w_in,
           ffn_conv_w, ffn_conv_b, ffn_w_out):
    depth = w_in.shape[0]
    xp, xs = x_prompt, x_sample
    Bp = xp.shape[0]
    F2 = ffn_w_in.shape[2]
    outs = [[] for _ in range(8)]
    for l in range(depth):
        p = _prep_params(l, norm_mix_pre, norm_mix_post, norm_ffn_pre, norm_ffn_post, w_in,
                         gla_w_gate2, gla_b_gate, gla_norm, ssd_conv_w, ssd_conv_b, ssd_dt_bias,
                         ssd_A_log, ssd_D, ssd_norm, w_out, ffn_w_in, ffn_conv_w, ffn_conv_b, ffn_w_out)
        xp, g1, h1, c1, f1 = _layer(
            xp,
            jnp.zeros((Bp, GLA_HEADS, GLA_HEAD_K, GLA_HEAD_V), F32),
            jnp.zeros((Bp, SSD_HEADS, SSD_HEADDIM, SSD_STATE), F32),
            jnp.zeros((Bp, SSD_CONV - 1, SSD_CONV_CH), F32),
            jnp.zeros((Bp, FFN_CONV - 1, F2), F32),
            p)
        xs, g2, h2, c2, f2 = _layer(
            xs, state_gla[l], state_ssd[l], state_ssd_conv[l], state_ffn_conv[l], p)
        for lst, val in zip(outs, (g1, h1, c1, f1, g2, h2, c2, f2)):
            lst.append(val)
    return (xp, xs) + tuple(jnp.stack(o) for o in outs)
```

```python
import functools

import numpy as np
import jax
import jax.numpy as jnp
from jax import lax
from jax.experimental import pallas as pl
from jax.experimental.pallas import tpu as pltpu

F32 = jnp.float32
BF16 = jnp.bfloat16
EPS = 1e-6

GLA_HEADS = 4
GLA_HEAD_K = 128
GLA_HEAD_V = 256
GLA_KDIM = GLA_HEADS * GLA_HEAD_K
GLA_WIDTH = GLA_HEADS * GLA_HEAD_V
GLA_LOWRANK = 16
GLA_GATE_NORMALIZER = 16.0
SSD_HEADS = 16
SSD_HEADDIM = 64
SSD_STATE = 128
SSD_GROUPS = 2
SSD_WIDTH = SSD_HEADS * SSD_HEADDIM
SSD_GROUP_WIDTH = SSD_WIDTH // SSD_GROUPS
SSD_CONV = 4
SSD_BC = 2 * SSD_GROUPS * SSD_STATE
SSD_CONV_CH = SSD_WIDTH + SSD_BC
FFN_CONV = 3
CHUNK = 64

A_Q = 0
A_K = GLA_KDIM
A_V = 2 * GLA_KDIM
A_G = A_V + GLA_WIDTH
A_W = A_G + GLA_WIDTH
IN_TAIL = GLA_LOWRANK + SSD_WIDTH + SSD_CONV_CH + SSD_HEADS
B_Z = 0
B_XS = SSD_WIDTH
B_BC = B_XS + SSD_WIDTH
B_SMALL = B_BC + SSD_BC
SMALL_W = 128
DTR_OFF = 0
LR_OFF = SSD_HEADS
B_W = B_SMALL + SMALL_W

LANES = 128
SUBLANES = 8
VMEM_LIMIT = 56 * 1024 * 1024

PROJ_TM = 512
ROW_PARTS = 4
FFN_TM_LONG = 1024
FFN_TM_SHORT = 512


def _group_tiles(B, L):
    if L >= CHUNK:
        lt = min(L, FFN_TM_LONG)
        return dict(nb=1, c=CHUNK, ns=B, ffn_nseq=1, ffn_L=lt, ffn_tps=L // lt)
    return dict(nb=CHUNK // L, c=L, ns=1, ffn_nseq=FFN_TM_SHORT // L, ffn_L=L, ffn_tps=1)


def _cparams(sem):
    return pltpu.CompilerParams(dimension_semantics=sem, vmem_limit_bytes=VMEM_LIMIT)


def _split3(x):
    hi = x.astype(BF16)
    r = x - hi.astype(F32)
    mid = r.astype(BF16)
    lo = (r - mid.astype(F32)).astype(BF16)
    return hi, mid, lo


def _dot(a, b):
    return jnp.dot(a, b, preferred_element_type=F32)


def _dot_nt(a, b):
    return lax.dot_general(a, b, (((1,), (1,)), ((), ())), preferred_element_type=F32)


def _dot_tn(a, b):
    return lax.dot_general(a, b, (((0,), (0,)), ((), ())), preferred_element_type=F32)


def _dot_sel(sel3_bf16, x_f32):
    return _dot(sel3_bf16, jnp.concatenate(_split3(x_f32), axis=0))


NEG_LOG2E = -1.4426950408889634


def _silu(x):
    return x / (1.0 + jnp.exp2(x * NEG_LOG2E))


def _softplus(x):
    return jnp.maximum(x, 0.0) + jnp.log1p(jnp.exp(-jnp.abs(x)))


def _rms(x, w):
    return x * lax.rsqrt(jnp.mean(x * x, axis=-1, keepdims=True) + EPS) * w


def _chunk_consts(nb, c):
    R = nb * c
    idx = np.arange(R)
    seq, pos = idx // c, idx % c
    same = seq[:, None] == seq[None, :]
    t, u = pos[:, None], pos[None, :]
    blocks = [same & (u <= t), same & (u > t)]
    masks = [np.eye(R, dtype=bool)]
    m = c // 2
    while m >= 1:
        blk = pos // (2 * m)
        rho = blk * 2 * m + m - 1
        upper = pos > rho
        a_up = upper[:, None] & (u > rho[:, None]) & (u <= t)
        a_lo = (~upper)[:, None] & (u > t) & (u <= rho[:, None])
        blocks.append(same & (a_up | a_lo))
        masks.append(same & upper[:, None] & (~upper)[None, :] & (blk[:, None] == blk[None, :]))
        m //= 2
    sel = np.concatenate(blocks, 0).astype(np.float32)
    lvl = np.stack(masks).astype(np.float32)
    causal = (same & (u <= t)).astype(np.float32)
    return sel, lvl, causal


INPROJ_RCH = 256


def _inproj_kernel(x_ref, nw_ref, w_ref, o_ref, *scratch, tail):
    if tail:
        w_scr, = scratch

        @pl.when(pl.program_id(0) == 0)
        def _():
            zpad = jnp.zeros((INPROJ_RCH, B_W - IN_TAIL), BF16)
            for r in range(0, w_ref.shape[0], INPROJ_RCH):
                blk = w_ref[r:r + INPROJ_RCH, :]
                w_scr[r:r + INPROJ_RCH, :] = jnp.concatenate(
                    [blk[:, GLA_LOWRANK:IN_TAIL], blk[:, 0:GLA_LOWRANK], zpad], axis=1)

        w = w_scr
    else:
        w = w_ref
    rp = x_ref.shape[0] // ROW_PARTS
    for k in range(ROW_PARTS):
        rs = slice(k * rp, (k + 1) * rp)
        xn = _rms(x_ref[rs, :], nw_ref[...]).astype(BF16)
        o_ref[rs, :] = _dot(xn, w[...])


INPROJ_TM = 256
INPROJ_PARTS = 2


def _inproj_both_kernel(x_ref, nw_ref, wa_ref, wt_ref, oa_ref, ob_ref, w_scr):
    @pl.when(pl.program_id(0) == 0)
    def _():
        zpad = jnp.zeros((INPROJ_RCH, B_W - IN_TAIL), BF16)
        for r in range(0, wt_ref.shape[0], INPROJ_RCH):
            blk = wt_ref[r:r + INPROJ_RCH, :]
            w_scr[r:r + INPROJ_RCH, :] = jnp.concatenate(
                [blk[:, GLA_LOWRANK:IN_TAIL], blk[:, 0:GLA_LOWRANK], zpad], axis=1)

    rp = x_ref.shape[0] // INPROJ_PARTS
    for k in range(INPROJ_PARTS):
        rs = slice(k * rp, (k + 1) * rp)
        xn = _rms(x_ref[rs, :], nw_ref[...]).astype(BF16)
        oa_ref[rs, :] = _dot(xn, wa_ref[...])
        ob_ref[rs, :] = _dot(xn, w_scr[...])


def _inproj_both(x2d, nw, w_bf16):
    M, D = x2d.shape
    tm = INPROJ_TM
    return pl.pallas_call(
        _inproj_both_kernel,
        grid=(M // tm,),
        in_specs=[pl.BlockSpec((tm, D), lambda i: (i, 0)),
                  pl.BlockSpec((1, D), lambda i: (0, 0)),
                  pl.BlockSpec((D, A_W), lambda i: (0, 0), pipeline_mode=pl.Buffered(1)),
                  pl.BlockSpec((D, A_W), lambda i: (0, 1), pipeline_mode=pl.Buffered(1))],
        out_specs=[pl.BlockSpec((tm, A_W), lambda i: (i, 0)), pl.BlockSpec((tm, B_W), lambda i: (i, 0))],
        out_shape=[jax.ShapeDtypeStruct((M, A_W), F32), jax.ShapeDtypeStruct((M, B_W), F32)],
        scratch_shapes=[pltpu.VMEM((D, B_W), BF16)],
        compiler_params=_cparams(("arbitrary",)),
        name="inproj",
    )(x2d, nw, w_bf16, w_bf16)


def _inproj(x2d, nw, w_bf16, *, tail):
    M, D = x2d.shape
    tm = PROJ_TM
    n_out = B_W if tail else A_W
    return pl.pallas_call(
        functools.partial(_inproj_kernel, tail=tail),
        grid=(M // tm,),
        in_specs=[pl.BlockSpec((tm, D), lambda i: (i, 0)),
                  pl.BlockSpec((1, D), lambda i: (0, 0)),
                  pl.BlockSpec((D, A_W), lambda i: (0, 1 if tail else 0), pipeline_mode=pl.Buffered(1))],
        out_specs=pl.BlockSpec((tm, n_out), lambda i: (i, 0)),
        out_shape=jax.ShapeDtypeStruct((M, n_out), F32),
        scratch_shapes=[pltpu.VMEM((D, B_W), BF16)] if tail else [],
        compiler_params=_cparams(("arbitrary",)),
        name="inproj",
    )(x2d, nw, w_bf16)


def _gla_group(gi, q_ref, k_ref, v_ref, g_ref, sm_ref, s_in, s_out, sel_ref, lvl_ref, w2_ref, bg_ref,
               gn_ref, o_ref, *, nb, c, nlev):
    R = nb * c
    sm = sm_ref[gi]
    sm_hi = sm.astype(BF16)
    sm_lo = (sm - sm_hi.astype(F32)).astype(BF16)
    zg = _dot(jnp.concatenate([sm_hi, sm_lo, sm_hi], axis=1), w2_ref[...]) + bg_ref[...]
    lg2 = -_softplus(-zg) * (-NEG_LOG2E / GLA_GATE_NORMALIZER)
    E = _dot_sel(sel_ref[...], lg2)
    eb = jnp.exp2(E[0:R])
    q = q_ref[gi] * (GLA_HEAD_K ** -0.5)
    k = k_ref[gi]
    qe = q * eb
    ke = k * jnp.exp2(E[R:2 * R])
    ql, kl = [q.astype(BF16)], [k.astype(BF16)]
    for l in range(nlev):
        p = jnp.exp2(E[(2 + l) * R:(3 + l) * R])
        ql.append((q * p).astype(BF16))
        kl.append((k * p).astype(BF16))
    vb = v_ref[gi].astype(BF16)
    g = g_ref[gi]
    gn = gn_ref[...]
    rows = lax.broadcasted_iota(jnp.int32, (R, 1), 0)

    zk = jnp.zeros((R, GLA_HEAD_K), BF16)
    zv = jnp.zeros((R, GLA_HEAD_V), BF16)
    o_intra = []
    for hp in range(GLA_HEADS // 2):
        k0 = slice(2 * hp * GLA_HEAD_K, (2 * hp + 1) * GLA_HEAD_K)
        k1 = slice((2 * hp + 1) * GLA_HEAD_K, (2 * hp + 2) * GLA_HEAD_K)
        k01 = slice(2 * hp * GLA_HEAD_K, (2 * hp + 2) * GLA_HEAD_K)
        att2 = jnp.zeros((R, 2 * R), F32)
        for l in range(nlev + 1):
            kbd = jnp.concatenate([jnp.concatenate([kl[l][:, k0], zk], axis=1),
                                   jnp.concatenate([zk, kl[l][:, k1]], axis=1)], axis=0)
            att2 = att2 + _dot_nt(ql[l][:, k01], kbd) * lvl_ref[l]
        v0 = vb[:, 2 * hp * GLA_HEAD_V:(2 * hp + 1) * GLA_HEAD_V]
        v1 = vb[:, (2 * hp + 1) * GLA_HEAD_V:(2 * hp + 2) * GLA_HEAD_V]
        vbd = jnp.concatenate([jnp.concatenate([v0, zv], axis=1),
                               jnp.concatenate([zv, v1], axis=1)], axis=0)
        o2 = _dot(att2.astype(BF16), vbd)
        o_intra += [o2[:, 0:GLA_HEAD_V], o2[:, GLA_HEAD_V:2 * GLA_HEAD_V]]

    for h in range(GLA_HEADS):
        ks = slice(h * GLA_HEAD_K, (h + 1) * GLA_HEAD_K)
        vs = slice(h * GLA_HEAD_V, (h + 1) * GLA_HEAD_V)
        o = o_intra[h]
        for s in range(nb):
            if nb > 1:
                rm = jnp.logical_and(rows >= s * c, rows < (s + 1) * c).astype(F32)
                qs = (qe[:, ks] * rm).astype(BF16)
                kd = (ke[:, ks] * rm).astype(BF16)
            else:
                qs = qe[:, ks].astype(BF16)
                kd = ke[:, ks].astype(BF16)
            S = s_in[gi * nb + s, h]
            o = o + _dot(qs, S.astype(BF16))
            upd = _dot_tn(kd, vb[:, vs])
            d = eb[s * c + c - 1:s * c + c, ks]
            dcol = jnp.transpose(jnp.broadcast_to(d, (GLA_HEAD_K, GLA_HEAD_K)))
            s_out[gi * nb + s, h] = jnp.concatenate([dcol, dcol], axis=1) * S + upd
        gh = g[:, vs]
        o_ref[gi, :, vs] = (_rms(o, gn) * _silu(gh)).astype(BF16)


def _conv_taps(u8, prev8, w, bias, width, chained):
    sub = lax.broadcasted_iota(jnp.int32, (1, SUBLANES, 1), 1)
    acc = bias + w[width - 1:width] * u8
    for j in range(width - 1):
        d = width - 1 - j
        r = pltpu.roll(u8, d, axis=1)
        pr = pltpu.roll(prev8, d, axis=1)
        if chained:
            pr = jnp.concatenate([pr, r[:-1]], axis=0)
        acc = acc + w[j:j + 1] * jnp.where(sub < d, pr, r)
    return acc


def _ssd_group(gi, z_ref, xs_ref, bc_ref, sm_ref, sel_ref, cm_ref, ex_ref, cw_ref, cb_ref, dtb_ref,
               alog_ref, de_ref, nw_ref, y_ref, ht_scr, cx_scr, cbc_scr, *, nb, c, out_lane0=0):
    R = nb * c
    sq = slice(gi * nb, (gi + 1) * nb)
    cw = cw_ref[...]
    cbias = cb_ref[...]
    nt = R // SUBLANES
    chained = nb == 1
    xraw = xs_ref[gi].reshape(nt, SUBLANES, SSD_WIDTH)
    bcraw = bc_ref[gi].reshape(nt, SUBLANES, SSD_BC)
    xs = _silu(_conv_taps(xraw, cx_scr[sq], cw[:, 0:SSD_WIDTH], cbias[:, 0:SSD_WIDTH], SSD_CONV, chained))
    bca = _silu(_conv_taps(bcraw, cbc_scr[sq], cw[:, SSD_WIDTH:SSD_CONV_CH],
                           cbias[:, SSD_WIDTH:SSD_CONV_CH], SSD_CONV, chained))
    cx_scr[sq] = xraw[nt - nb:nt]
    cbc_scr[sq] = bcraw[nt - nb:nt]
    xs = xs.reshape(R, SSD_WIDTH)
    bca = bca.reshape(R, SSD_BC)
    Bm = bca[:, 0:SSD_GROUPS * SSD_STATE]
    Cm = bca[:, SSD_GROUPS * SSD_STATE:SSD_BC]

    dt = _softplus(sm_ref[gi] + dtb_ref[...])
    la2 = dt * (NEG_LOG2E * jnp.exp(alog_ref[...]))
    cs = _dot_sel(sel_ref[0:2 * R], la2)
    cum = cs[0:R]
    stack = jnp.concatenate([cs, dt], axis=0)
    st_e = _dot(jnp.concatenate(_split3(stack), axis=1), ex_ref[...])
    cum_e = st_e[0:R]
    lmc_e = st_e[R:2 * R]
    dt_e = st_e[2 * R:3 * R]
    ecum_e = jnp.exp2(cum_e)
    xdt = xs * dt_e
    xw = (xdt * jnp.exp2(lmc_e)).astype(BF16)
    Bb = Bm.astype(BF16)
    Cb = Cm.astype(BF16)

    cumT2 = jnp.transpose(jnp.concatenate([cum, cum], axis=0))
    cmask2 = cm_ref[...] > 0.5
    rows = lax.broadcasted_iota(jnp.int32, (R, 1), 0)
    lane = lax.broadcasted_iota(jnp.int32, (1, 2 * R), 1)
    first = lane < R
    lo_half = lax.broadcasted_iota(jnp.int32, (1, 2 * SSD_HEADDIM), 1) < SSD_HEADDIM

    hpg = SSD_HEADS // SSD_GROUPS
    for hg in range(SSD_GROUPS):
        ns = slice(hg * SSD_STATE, (hg + 1) * SSD_STATE)
        gs = slice(hg * SSD_GROUP_WIDTH, (hg + 1) * SSD_GROUP_WIDTH)
        bgrp = Bb[:, ns]
        cb2 = _dot_nt(Cb[:, ns], jnp.concatenate([bgrp, bgrp], axis=0))
        yg = []
        for j in range(hpg // 2):
            l0 = hg * SSD_GROUP_WIDTH + 2 * j * SSD_HEADDIM
            h0 = DTR_OFF + hg * hpg + 2 * j
            col = jnp.where(first, cum[:, h0:h0 + 1], cum[:, h0 + 1:h0 + 2])
            row = jnp.where(first, cumT2[h0:h0 + 1, :], cumT2[h0 + 1:h0 + 2, :])
            m2 = cb2 * jnp.exp2(jnp.where(cmask2, col - row, -jnp.inf))
            xpair = xdt[:, l0:l0 + 2 * SSD_HEADDIM]
            x2 = jnp.concatenate([jnp.where(lo_half, xpair, 0.0).astype(BF16),
                                  jnp.where(lo_half, 0.0, xpair).astype(BF16)], axis=0)
            yg.append(_dot(m2.astype(BF16), x2))
        y_intra = jnp.concatenate(yg, axis=1)
        y_inter = jnp.zeros((R, SSD_GROUP_WIDTH), F32)
        for s in range(nb):
            if nb > 1:
                rm = jnp.logical_and(rows >= s * c, rows < (s + 1) * c).astype(F32)
                cg = (Cm[:, ns] * rm).astype(BF16)
                bg_ = (Bm[:, ns] * rm).astype(BF16)
            else:
                cg = Cb[:, ns]
                bg_ = Bb[:, ns]
            hT = ht_scr[gi * nb + s, :, gs]
            y_inter = y_inter + _dot(cg, hT.astype(BF16))
            upd = _dot_tn(bg_, xw[:, gs])
            dl = ecum_e[s * c + c - 1:s * c + c, gs]
            ht_scr[gi * nb + s, :, gs] = dl * hT + upd
        y = y_intra + y_inter * ecum_e[:, gs] + de_ref[:, gs] * xs[:, gs]
        y = y * _silu(z_ref[gi, :, gs])
        os_ = slice(out_lane0 + hg * SSD_GROUP_WIDTH, out_lane0 + (hg + 1) * SSD_GROUP_WIDTH)
        y_ref[gi, :, os_] = _rms(y, nw_ref[:, gs]).astype(BF16)


def _group_view(a, G):
    return a.reshape(G, a.shape[0] // G, a.shape[1])


def _mixer_kernel(q_ref, k_ref, v_ref, g_ref, z_ref, xs_ref, bc_ref, sm_ref, s0_ref, cst_ref, h0_ref,
                  sel_ref, lvl_ref, w2_ref, bg_ref, gn_ref, cm_ref, ex_ref, cw_ref, cb_ref, dtb_ref,
                  alog_ref, de_ref, nw_ref, mix_ref, sn_ref, hn_ref, *scratch, ns, nb, c, nlev):
    ci = pl.program_id(1)
    if len(scratch) == 4:
        s_scr, ht_scr, cx_scr, cbc_scr = scratch
        s_in = s_out = s_scr
    else:
        ht_scr, cx_scr, cbc_scr = scratch
        s_scr, s_in, s_out = None, s0_ref, sn_ref

    @pl.when(ci == 0)
    def _():
        if s_scr is not None:
            s_scr[...] = s0_ref[...]
        cx_scr[...] = jnp.zeros_like(cx_scr)
        cbc_scr[...] = jnp.zeros_like(cbc_scr)
        cx_scr[:, SUBLANES - (SSD_CONV - 1):SUBLANES, :] = cst_ref[:, :, 0:SSD_WIDTH]
        cbc_scr[:, SUBLANES - (SSD_CONV - 1):SUBLANES, :] = cst_ref[:, :, SSD_WIDTH:SSD_CONV_CH]
        for s in range(ns * nb):
            ht_scr[s] = jnp.transpose(h0_ref[s].reshape(SSD_WIDTH, SSD_STATE))

    for gi in range(ns):
        _gla_group(gi, q_ref, k_ref, v_ref, g_ref, sm_ref, s_in, s_out, sel_ref, lvl_ref, w2_ref, bg_ref,
                   gn_ref, mix_ref, nb=nb, c=c, nlev=nlev)
        _ssd_group(gi, z_ref, xs_ref, bc_ref, sm_ref, sel_ref, cm_ref, ex_ref, cw_ref, cb_ref, dtb_ref,
                   alog_ref, de_ref, nw_ref, mix_ref, ht_scr, cx_scr, cbc_scr, nb=nb, c=c,
                   out_lane0=GLA_WIDTH)

    @pl.when(ci == pl.num_programs(1) - 1)
    def _():
        if s_scr is not None:
            sn_ref[...] = s_scr[...]
        for s in range(ns * nb):
            hn_ref[s] = jnp.transpose(ht_scr[s]).reshape(SSD_HEADS, SSD_HEADDIM, SSD_STATE)


def _mixer(proj_a, proj_b, s_gla, s_conv, s_ssd, p, *, B, L, nb, c, ns):
    R = nb * c
    G = B // nb
    Lg = L * nb
    ncl = Lg // R
    nlev = int(np.log2(c))
    assert 2 * R == LANES, "two heads share one 128-lane tile"
    assert c % SUBLANES == 0 and (nb == 1 or c == SUBLANES), "conv carry is one sublane tile per sequence"
    sel, lvl, causal = _chunk_consts(nb, c)
    sel = jnp.asarray(np.tile(sel, (1, 3)), BF16)
    lvl = jnp.asarray(np.tile(lvl, (1, 1, 2)), F32)
    causal = jnp.asarray(np.tile(causal, (1, 2)), F32)
    exn = np.zeros((SMALL_W, SSD_WIDTH), np.float32)
    for h in range(SSD_HEADS):
        exn[DTR_OFF + h, h * SSD_HEADDIM:(h + 1) * SSD_HEADDIM] = 1.0
    ex = jnp.asarray(np.tile(exn, (3, 1)), BF16)
    M = proj_a.shape[0]
    pa, pb = _group_view(proj_a, G), _group_view(proj_b, G)

    def rowblk(w, col):
        return pl.BlockSpec((ns, R, w), lambda bi, ci: (bi, ci, col // w))

    const2 = lambda a: pl.BlockSpec(a.shape, lambda bi, ci: (0,) * a.ndim)
    nsq = ns * nb
    sshape = (nsq, GLA_HEADS, GLA_HEAD_K, GLA_HEAD_V)
    sspec = pl.BlockSpec(sshape, lambda bi, ci: (bi, 0, 0, 0))
    hspec = pl.BlockSpec((nsq, SSD_HEADS, SSD_HEADDIM, SSD_STATE), lambda bi, ci: (bi, 0, 0, 0))
    consts = (sel, lvl, p['w2cat'], p['bg'], p['gn'], causal, ex, p['ssd_cw'], p['ssd_cb'], p['dtb'],
              p['alog'], p['de'], p['ssd_nw'])
    kern = functools.partial(_mixer_kernel, ns=ns, nb=nb, c=c, nlev=nlev)
    mix, s_new, h_new = pl.pallas_call(
        kern,
        grid=(G // ns, ncl),
        in_specs=[rowblk(GLA_KDIM, A_Q), rowblk(GLA_KDIM, A_K), rowblk(GLA_WIDTH, A_V),
                  rowblk(GLA_WIDTH, A_G), rowblk(SSD_WIDTH, B_Z), rowblk(SSD_WIDTH, B_XS),
                  rowblk(SSD_BC, B_BC), rowblk(SMALL_W, B_SMALL), sspec,
                  pl.BlockSpec((nsq, SSD_CONV - 1, SSD_CONV_CH), lambda bi, ci: (bi, 0, 0)), hspec]
                 + [const2(a) for a in consts],
        out_specs=[pl.BlockSpec((ns, R, GLA_WIDTH + SSD_WIDTH), lambda bi, ci: (bi, ci, 0)), sspec, hspec],
        out_shape=[jax.ShapeDtypeStruct((G, Lg, GLA_WIDTH + SSD_WIDTH), BF16),
                   jax.ShapeDtypeStruct((B, GLA_HEADS, GLA_HEAD_K, GLA_HEAD_V), F32),
                   jax.ShapeDtypeStruct((B, SSD_HEADS, SSD_HEADDIM, SSD_STATE), F32)],
        scratch_shapes=([pltpu.VMEM(sshape, F32)] if ncl > 1 else []) + [
            pltpu.VMEM((nsq, SSD_STATE, SSD_WIDTH), F32),
            pltpu.VMEM((nsq, SUBLANES, SSD_WIDTH), F32),
            pltpu.VMEM((nsq, SUBLANES, SSD_BC), F32)],
        compiler_params=_cparams(("parallel", "arbitrary")),
        name="mixer",
    )(pa, pa, pa, pa, pb, pb, pb, pb, s_gla, s_conv, s_ssd, *consts)
    return mix.reshape(M, GLA_WIDTH + SSD_WIDTH), s_new, h_new


def _outproj_kernel(m_ref, x_ref, w_ref, npost_ref, npre_ref, h_ref, xn_ref):
    rp = x_ref.shape[0] // ROW_PARTS
    for k in range(ROW_PARTS):
        rs = slice(k * rp, (k + 1) * rp)
        h = x_ref[rs, :] + _rms(_dot(m_ref[rs, :], w_ref[...]), npost_ref[...])
        h_ref[rs, :] = h
        xn_ref[rs, :] = _rms(h, npre_ref[...]).astype(BF16)


def _outproj(mix, x2d, w_out_bf16, npost, npre):
    M, D = x2d.shape
    W = w_out_bf16.shape[0]
    tm = PROJ_TM
    return pl.pallas_call(
        _outproj_kernel,
        grid=(M // tm,),
        in_specs=[pl.BlockSpec((tm, W), lambda i: (i, 0)),
                  pl.BlockSpec((tm, D), lambda i: (i, 0)),
                  pl.BlockSpec((W, D), lambda i: (0, 0)),
                  pl.BlockSpec((1, D), lambda i: (0, 0)),
                  pl.BlockSpec((1, D), lambda i: (0, 0))],
        out_specs=[pl.BlockSpec((tm, D), lambda i: (i, 0)),
                   pl.BlockSpec((tm, D), lambda i: (i, 0))],
        out_shape=[jax.ShapeDtypeStruct((M, D), F32), jax.ShapeDtypeStruct((M, D), BF16)],
        compiler_params=_cparams(("parallel",)),
        name="outproj",
    )(mix, x2d, w_out_bf16, npost, npre)


FFN_HDR = SUBLANES
FFN_TF = 512
FFN_RBLK = 64
FFN_PART = 256


def _ffn_kernel(xn_ref, h_hbm, sa_ref, sb_ref, wa_ref, wb_ref, cwa_ref, cwb_ref, cba_ref, cbb_ref,
                wo_ref, npost_ref, y_ref, na_ref, nb_ref, ua_scr, ub_scr, act_scr, h_buf, h_sem, *carry,
                nseq, L, tps):
    i = pl.program_id(0)
    j = pl.program_id(1)
    tf = wa_ref.shape[1]
    tm = nseq * L
    nparts = tm // FFN_PART
    W1 = FFN_CONV - 1
    H = FFN_HDR
    first = i % tps == 0

    def h_copy():
        return pltpu.make_async_copy(h_hbm.at[pl.ds(i * tm, tm), :], h_buf, h_sem)

    @pl.when(j == 0)
    def _():
        h_copy().start()

    if tps > 1:
        ca_scr, cb_scr = carry

        @pl.when(jnp.logical_not(first))
        def _():
            ua_scr[:, 0:H, :] = ca_scr[j]
            ub_scr[:, 0:H, :] = cb_scr[j]

    @pl.when(first)
    def _():
        ua_scr[:, H - W1:H, :] = sa_ref[...]
        ub_scr[:, H - W1:H, :] = sb_ref[...]

    @pl.when(j == 0)
    def _():
        y_ref[...] = jnp.zeros_like(y_ref)

    if nseq == 1:
        parts = [(0, 1, k * FFN_PART, (k + 1) * FFN_PART) for k in range(nparts)]
    else:
        sp = FFN_PART // L
        parts = [(k * sp, (k + 1) * sp, 0, L) for k in range(nparts)]
    tile_rows = lambda q0, q1, r0, r1: slice(q0 * L + r0, (q1 - 1) * L + r1)

    for (q0, q1, r0, r1) in parts:
        rs = tile_rows(q0, q1, r0, r1)
        ua_scr[q0:q1, H + r0:H + r1, :] = _dot(xn_ref[rs, :], wa_ref[...]).reshape(q1 - q0, r1 - r0, tf)
        ub_scr[q0:q1, H + r0:H + r1, :] = _dot(xn_ref[rs, :], wb_ref[...]).reshape(q1 - q0, r1 - r0, tf)

    spread = lambda row: jnp.broadcast_to(row, (SUBLANES, tf))
    wa8 = [spread(cwa_ref[t:t + 1, :]) for t in range(FFN_CONV)]
    wb8 = [spread(cwb_ref[t:t + 1, :]) for t in range(FFN_CONV)]
    ba8 = spread(cba_ref[...])
    bb8 = spread(cbb_ref[...])
    for (q0, q1, r0, r1) in parts:
        if nseq == 1:
            blocks = [(0, 1, r, r + FFN_RBLK) for r in range(r0, r1, FFN_RBLK)]
        else:
            qb = FFN_RBLK // L
            blocks = [(q, q + qb, 0, L) for q in range(q0, q1, qb)]
        for (bq0, bq1, br0, br1) in blocks:
            def conv(u_scr, w8, b8):
                tap = lambda d: u_scr[bq0:bq1, H - d + br0:H - d + br1, :].reshape(-1, SUBLANES, tf)
                out = b8 + w8[W1] * tap(0)
                for t in range(W1):
                    out = out + w8[t] * tap(W1 - t)
                return out
            a = conv(ua_scr, wa8, ba8)
            b = conv(ub_scr, wb8, bb8)
            row0 = bq0 * L + br0
            act_scr[row0:row0 + FFN_RBLK, :] = (_silu(a) * b).reshape(FFN_RBLK, tf).astype(BF16)
        rs = tile_rows(q0, q1, r0, r1)
        y_ref[rs, :] += _dot(act_scr[rs, :], wo_ref[...])

    na_ref[...] = ua_scr[:, H + L - W1:H + L, :]
    nb_ref[...] = ub_scr[:, H + L - W1:H + L, :]
    if tps > 1:
        ca_scr[j] = ua_scr[:, L:L + H, :]
        cb_scr[j] = ub_scr[:, L:L + H, :]

    @pl.when(j == pl.num_programs(1) - 1)
    def _():
        h_copy().wait()
        y_ref[...] = h_buf[...] + _rms(y_ref[...], npost_ref[...])


def _ffn(xn2, h2d, st, w_in_bf16, cw, cb, w_out_bf16, npost, *, nseq, L, tps):
    M, D = h2d.shape
    F = w_out_bf16.shape[0]
    tf = FFN_TF
    nj = F // tf
    tm = nseq * L
    W1 = FFN_CONV - 1
    assert tm % FFN_PART == 0 and FFN_PART % FFN_RBLK == 0 and (nseq == 1 or FFN_RBLK % L == 0)
    kern = functools.partial(_ffn_kernel, nseq=nseq, L=L, tps=tps)
    stspec_a = pl.BlockSpec((nseq, W1, tf), lambda i, j: (i // tps, 0, j))
    stspec_b = pl.BlockSpec((nseq, W1, tf), lambda i, j: (i // tps, 0, j + nj))
    return pl.pallas_call(
        kern,
        grid=(M // tm, nj),
        in_specs=[pl.BlockSpec((tm, D), lambda i, j: (i, 0), pipeline_mode=pl.Buffered(1)),
                  pl.BlockSpec(memory_space=pl.ANY),
                  stspec_a, stspec_b,
                  pl.BlockSpec((D, tf), lambda i, j: (0, j)),
                  pl.BlockSpec((D, tf), lambda i, j: (0, j + nj)),
                  pl.BlockSpec((FFN_CONV, tf), lambda i, j: (0, j)),
                  pl.BlockSpec((FFN_CONV, tf), lambda i, j: (0, j + nj)),
                  pl.BlockSpec((1, tf), lambda i, j: (0, j)),
                  pl.BlockSpec((1, tf), lambda i, j: (0, j + nj)),
                  pl.BlockSpec((tf, D), lambda i, j: (j, 0)),
                  pl.BlockSpec((1, D), lambda i, j: (0, 0))],
        out_specs=[pl.BlockSpec((tm, D), lambda i, j: (i, 0)),
                   pl.BlockSpec((nseq, W1, tf), lambda i, j: (i, 0, j)),
                   pl.BlockSpec((nseq, W1, tf), lambda i, j: (i, 0, j))],
        out_shape=[jax.ShapeDtypeStruct((M, D), F32),
                   jax.ShapeDtypeStruct((M // L, W1, F), F32),
                   jax.ShapeDtypeStruct((M // L, W1, F), F32)],
        scratch_shapes=[pltpu.VMEM((nseq, FFN_HDR + L, tf), F32),
                        pltpu.VMEM((nseq, FFN_HDR + L, tf), F32),
                        pltpu.VMEM((tm, tf), BF16),
                        pltpu.VMEM((tm, D), F32),
                        pltpu.SemaphoreType.DMA(())] + (
                            [pltpu.VMEM((nj, nseq, FFN_HDR, tf), F32)] * 2 if tps > 1 else []),
        compiler_params=_cparams(("arbitrary", "arbitrary")),
        name="ffn",
    )(xn2, h2d, st, st, w_in_bf16, w_in_bf16, cw, cw, cb, cb, w_out_bf16, npost)


def _layer(x, s_gla, s_ssd, s_conv, s_ffn, p):
    B, L, D = x.shape
    M = B * L
    t = _group_tiles(B, L)
    x2d = x.reshape(M, D)
    proj_a, proj_b = _inproj_both(x2d, p['n_mix_pre'], p['w_in'])
    mix, g_new, h_new = _mixer(proj_a, proj_b, s_gla, s_conv, s_ssd, p, B=B, L=L,
                               nb=t['nb'], c=t['c'], ns=t['ns'])
    hres, xn2 = _outproj(mix, x2d, p['w_out'], p['n_mix_post'], p['n_ffn_pre'])
    y, fa, fb = _ffn(xn2, hres, s_ffn, p['ffn_w_in'], p['ffn_cw'], p['ffn_cb'], p['ffn_w_out'],
                     p['n_ffn_post'], nseq=t['ffn_nseq'], L=t['ffn_L'], tps=t['ffn_tps'])
    c_new = proj_b.reshape(B, L, -1)[:, L - (SSD_CONV - 1):, B_XS:B_XS + SSD_CONV_CH]
    f_new = jnp.concatenate([fa, fb], axis=-1)[t['ffn_tps'] - 1::t['ffn_tps']]
    return y.reshape(B, L, D), g_new, h_new, c_new, f_new


def _prep_params(l, norm_mix_pre, norm_mix_post, norm_ffn_pre, norm_ffn_post, w_in, gla_w_gate2,
                 gla_b_gate, gla_norm, ssd_conv_w, ssd_conv_b, ssd_dt_bias, ssd_A_log, ssd_D, ssd_norm,
                 w_out, ffn_w_in, ffn_conv_w, ffn_conv_b, ffn_w_out):
    assert w_in.shape[2] == A_W + IN_TAIL
    w2p = jnp.zeros((SMALL_W, GLA_KDIM), F32).at[LR_OFF:LR_OFF + GLA_LOWRANK].set(gla_w_gate2[l])
    w2_hi = w2p.astype(BF16)
    w2_lo = (w2p - w2_hi.astype(F32)).astype(BF16)
    w2cat = jnp.concatenate([w2_hi, w2_hi, w2_lo], axis=0)
    pad_small = lambda v: jnp.zeros((1, SMALL_W), F32).at[0, DTR_OFF:DTR_OFF + SSD_HEADS].set(v)
    row = lambda v: v.reshape(1, -1)
    return dict(
        n_mix_pre=row(norm_mix_pre[l]), n_mix_post=row(norm_mix_post[l]),
        n_ffn_pre=row(norm_ffn_pre[l]), n_ffn_post=row(norm_ffn_post[l]),
        w_in=w_in[l].astype(BF16), w2cat=w2cat, bg=row(gla_b_gate[l]), gn=row(gla_norm[l]),
        ssd_cw=ssd_conv_w[l], ssd_cb=row(ssd_conv_b[l]),
        dtb=pad_small(ssd_dt_bias[l]), alog=pad_small(ssd_A_log[l]),
        de=row(jnp.repeat(ssd_D[l], SSD_HEADDIM)), ssd_nw=row(ssd_norm[l]),
        w_out=w_out[l].astype(BF16), ffn_w_in=ffn_w_in[l].astype(BF16),
        ffn_cw=ffn_conv_w[l], ffn_cb=row(ffn_conv_b[l]), ffn_w_out=ffn_w_out[l].astype(BF16))


def kernel(x_prompt, x_sample, state_gla, state_ssd, state_ssd_conv, state_ffn_conv, norm_mix_pre,
           norm_mix_post, norm_ffn_pre, norm_ffn_post, w_in, gla_w_gate2, gla_b_gate, gla_norm,
           ssd_conv_w, ssd_conv_b, ssd_dt_bias, ssd_A_log, ssd_D, ssd_norm, w_out, ffn_w_in,
           ffn_conv_w, ffn_conv_b, ffn_w_out):
    depth = w_in.shape[0]
    xp, xs = x_prompt, x_sample
    Bp = xp.shape[0]
    F2 = ffn_w_in.shape[2]
    outs = [[] for _ in range(8)]
    for l in range(depth):
        p = _prep_params(l, norm_mix_pre, norm_mix_post, norm_ffn_pre, norm_ffn_post, w_in,
                         gla_w_gate2, gla_b_gate, gla_norm, ssd_conv_w, ssd_conv_b, ssd_dt_bias,
                         ssd_A_log, ssd_D, ssd_norm, w_out, ffn_w_in, ffn_conv_w, ffn_conv_b, ffn_w_out)
        xp, g1, h1, c1, f1 = _layer(
            xp,
            jnp.zeros((Bp, GLA_HEADS, GLA_HEAD_K, GLA_HEAD_V), F32),
            jnp.zeros((Bp, SSD_HEADS, SSD_HEADDIM, SSD_STATE), F32),
            jnp.zeros((Bp, SSD_CONV - 1, SSD_CONV_CH), F32),
            jnp.zeros((Bp, FFN_CONV - 1, F2), F32),
            p)
        xs, g2, h2, c2, f2 = _layer(
            xs, state_gla[l], state_ssd[l], state_ssd_conv[l], state_ffn_conv[l], p)
        for lst, val in zip(outs, (g1, h1, c1, f1, g2, h2, c2, f2)):
            lst.append(val)
    return (xp, xs) + tuple(jnp.stack(o) for o in outs)
```
